```python
import math
import jax
import jax.numpy as jnp
from jax import lax
import numpy as np

D_MODEL = 2048
BATCH = 8
SEQ = 2048
DEPTH = 2

ROPE_THETA = 500000.0
Q_BLOCK = 128
NORM_EPS = 1e-5
MAX_POS_OFFSET = 4096

DIFF_HEADS = 4
DIFF_HALF_DIM = 64
DIFF_V_DIM = 2 * DIFF_HALF_DIM
DIFF_ROT = DIFF_HALF_DIM // 4

MLA_HEADS = 8
MLA_Q_RANK = 512
MLA_KV_RANK = 256
MLA_NOPE = 128
MLA_ROPE = 64
MLA_V = 128

MOBA_HEADS = 4
MOBA_HEAD_DIM = 128
MOBA_BLOCK = 256
MOBA_TOPK = 3
MOBA_ROT = MOBA_HEAD_DIM // 4
MOBA_Q_CHUNK = 128

A_WIDTH = DIFF_HEADS * DIFF_V_DIM
B_WIDTH = MLA_HEADS * MLA_V
C_WIDTH = MOBA_HEADS * MOBA_HEAD_DIM
MIX_WIDTH = A_WIDTH + B_WIDTH + C_WIDTH
N_BRANCHES = 3

IN_SIZES = (
    DIFF_HEADS * 2 * DIFF_HALF_DIM,
    DIFF_HEADS * 2 * DIFF_HALF_DIM,
    DIFF_HEADS * DIFF_V_DIM,
    MLA_Q_RANK,
    MLA_KV_RANK,
    MLA_ROPE,
    C_WIDTH, C_WIDTH, C_WIDTH,
    N_BRANCHES * D_MODEL,
)
IN_COLS = sum(IN_SIZES)

N_EXPERTS = 64
N_GROUPS = 8
TOPK_GROUPS = 4
TOPK = 8
EXPERT_FF = 512
SHARED_FF = 512
ROUTED_SCALE = 2.5
MOE_TOKEN_BLOCK = 1024

DEEPNORM_ALPHA = (2 * DEPTH) ** 0.25
DEEPNORM_BETA = (8 * DEPTH) ** -0.25

kernel_name = 'hybrid_gated_diff_mla_moba_moe'

F32 = jnp.float32


def _split(a, sizes):
    outs, off = [], 0
    for s in sizes:
        outs.append(a[..., off:off + s])
        off += s
    return outs


def _layer_norm(x, g, b):
    xf = x.astype(F32)
    mu = xf.mean(-1, keepdims=True)
    var = jnp.square(xf - mu).mean(-1, keepdims=True)
    return ((xf - mu) * lax.rsqrt(var + NORM_EPS) * g + b).astype(x.dtype)


def _rms_norm(x, g):
    xf = x.astype(F32)
    return (xf * lax.rsqrt(jnp.mean(jnp.square(xf), -1, keepdims=True) + NORM_EPS) * g).astype(x.dtype)


def _rope_tables(positions, rot_dim):
    inv_freq = ROPE_THETA ** (-jnp.arange(0, rot_dim, 2, dtype=F32) / rot_dim)
    ang = positions.astype(F32)[..., None] * inv_freq
    return jnp.cos(ang)[:, :, None, :], jnp.sin(ang)[:, :, None, :]


def _apply_rope(x, cs):
    cos, sin = cs
    half = cos.shape[-1]
    x1 = x[..., :half].astype(F32)
    x2 = x[..., half:2 * half].astype(F32)
    rot = jnp.concatenate([x1 * cos - x2 * sin, x2 * cos + x1 * sin], -1).astype(x.dtype)
    return jnp.concatenate([rot, x[..., 2 * half:]], -1)


def _heads(t):
    return t.transpose(0, 2, 1, 3)


def _causal_softmax(s, q_start):
    qpos = q_start + jnp.arange(s.shape[-2])
    kpos = jnp.arange(s.shape[-1])
    return jax.nn.softmax(jnp.where(kpos[None, :] <= qpos[:, None], s, -jnp.inf), axis=-1)


def _sweep_query_blocks(block_fn, *qs):
    b, h, s, _ = qs[0].shape
    n = s // Q_BLOCK
    blocks = tuple(q.reshape(b, h, n, Q_BLOCK, q.shape[-1]).transpose(2, 0, 1, 3, 4) for q in qs)
    out = lax.map(lambda a: block_fn(a[0] * Q_BLOCK, *a[1:]), (jnp.arange(n),) + blocks)
    return out.transpose(1, 2, 0, 3, 4).reshape(b, h, s, out.shape[-1])


def _causal_attention(q, k, v, scale):
    kf, vf = k.astype(F32), v.astype(F32)

    def block(q_start, qb):
        p = _causal_softmax(jnp.einsum('bhqd,bhkd->bhqk', qb.astype(F32), kf) * scale, q_start)
        return jnp.einsum('bhqk,bhkd->bhqd', p, vf)

    return _sweep_query_blocks(block, q)


def _diff_attention(q1, q2, k1, k2, v, lam):
    scale = DIFF_HALF_DIM ** -0.5
    k1f, k2f, vf = k1.astype(F32), k2.astype(F32), v.astype(F32)

    def block(q_start, q1b, q2b):
        p1 = _causal_softmax(jnp.einsum('bhqd,bhkd->bhqk', q1b.astype(F32), k1f) * scale, q_start)
        p2 = _causal_softmax(jnp.einsum('bhqd,bhkd->bhqk', q2b.astype(F32), k2f) * scale, q_start)
        return jnp.einsum('bhqk,bhkd->bhqd', p1 - lam * p2, vf)

    return _sweep_query_blocks(block, q1, q2)


def _moba_attention(q, k, v):
    b, h, s, dh = q.shape
    nb = -(-s // MOBA_BLOCK)
    n_sel = min(MOBA_TOPK, nb)
    pad = nb * MOBA_BLOCK - s
    padw = ((0, 0), (0, 0), (0, pad), (0, 0))
    kb = jnp.pad(k.astype(F32), padw).reshape(b, h, nb, MOBA_BLOCK, dh)
    vb = jnp.pad(v.astype(F32), padw).reshape(b, h, nb, MOBA_BLOCK, dh)
    qf = q.astype(F32)
    scale = dh ** -0.5
    k_mean = kb.mean(axis=3)
    gate = jnp.einsum('bhsd,bhnd->bhsn', qf, k_mean)
    q_block = jnp.arange(s) // MOBA_BLOCK
    fully_past = jnp.arange(nb)[None, :] < q_block[:, None]
    gate = jnp.where(fully_past, gate, -jnp.inf)
    gate_sel, sel = lax.top_k(gate, n_sel)
    valid = jnp.isfinite(gate_sel)
    head_idx = jnp.arange(h)[:, None, None]
    n_chunks = s // MOBA_Q_CHUNK

    def per_batch(args):
        q_b, kb_b, vb_b, sel_b, valid_b = args

        def per_chunk(c):
            q0 = c * MOBA_Q_CHUNK
            qc = lax.dynamic_slice_in_dim(q_b, q0, MOBA_Q_CHUNK, axis=1)
            sc = lax.dynamic_slice_in_dim(sel_b, q0, MOBA_Q_CHUNK, axis=1)
            vc = lax.dynamic_slice_in_dim(valid_b, q0, MOBA_Q_CHUNK, axis=1)
            k_sel = kb_b[head_idx, sc]
            v_sel = vb_b[head_idx, sc]
            s_sel = jnp.einsum('hqd,hqnld->hqnl', qc, k_sel) * scale
            s_sel = jnp.where(vc[..., None], s_sel, -jnp.inf).reshape(h, MOBA_Q_CHUNK, n_sel * MOBA_BLOCK)
            own = q0 // MOBA_BLOCK
            k_own = lax.dynamic_index_in_dim(kb_b, own, axis=1, keepdims=False)
            v_own = lax.dynamic_index_in_dim(vb_b, own, axis=1, keepdims=False)
            s_own = jnp.einsum('hqd,hld->hql', qc, k_own) * scale
            qpos = q0 + jnp.arange(MOBA_Q_CHUNK)
            kpos = own * MOBA_BLOCK + jnp.arange(MOBA_BLOCK)
            s_own = jnp.where(kpos[None, :] <= qpos[:, None], s_own, -jnp.inf)
            p = jax.nn.softmax(jnp.concatenate([s_sel, s_own], -1), axis=-1)
            p_sel = p[..., :n_sel * MOBA_BLOCK].reshape(h, MOBA_Q_CHUNK, n_sel, MOBA_BLOCK)
            p_own = p[..., n_sel * MOBA_BLOCK:]
            return (jnp.einsum('hqnl,hqnld->hqd', p_sel, v_sel)
                    + jnp.einsum('hql,hld->hqd', p_own, v_own))

        out = lax.map(per_chunk, jnp.arange(n_chunks))
        return out.transpose(1, 0, 2, 3).reshape(h, s, dh)

    return lax.map(per_batch, (qf, kb, vb, sel, valid))


def _token_mixer(x, ropes, layer, w_in, lq1, lk1, lq2, lk2, subln_g, q_norm_g, w_uq,
                 kv_norm_g, w_ukv, w_branch, w_out):
    b, s, _ = x.shape
    rope_diff, rope_mla, rope_moba = ropes
    dq, dk, dv, cq, ckv, kpe, mq, mk, mv, gates = _split(x @ w_in, IN_SIZES)

    q12 = dq.reshape(b, s, DIFF_HEADS, 2, DIFF_HALF_DIM)
    k12 = dk.reshape(b, s, DIFF_HEADS, 2, DIFF_HALF_DIM)
    q1 = _apply_rope(q12[..., 0, :], rope_diff)
    q2 = _apply_rope(q12[..., 1, :], rope_diff)
    k1 = _apply_rope(k12[..., 0, :], rope_diff)
    k2 = _apply_rope(k12[..., 1, :], rope_diff)
    lambda_init = 0.8 - 0.6 * math.exp(-0.3 * layer)
    lam = (jnp.exp(jnp.sum(lq1.astype(F32) * lk1.astype(F32)))
           - jnp.exp(jnp.sum(lq2.astype(F32) * lk2.astype(F32))) + lambda_init)
    o_a = _diff_attention(_heads(q1), _heads(q2), _heads(k1), _heads(k2),
                          _heads(dv.reshape(b, s, DIFF_HEADS, DIFF_V_DIM)), lam)
    o_a = _rms_norm(o_a, subln_g) * (1.0 - lambda_init)
    o_a = o_a.transpose(0, 2, 1, 3).reshape(b, s, A_WIDTH).astype(x.dtype)

    q = (_rms_norm(cq, q_norm_g) @ w_uq).reshape(b, s, MLA_HEADS, MLA_NOPE + MLA_ROPE)
    q = jnp.concatenate([q[..., :MLA_NOPE], _apply_rope(q[..., MLA_NOPE:], rope_mla)], -1)
    kv = (_rms_norm(ckv, kv_norm_g) @ w_ukv).reshape(b, s, MLA_HEADS, MLA_NOPE + MLA_V)
    k_pe = _apply_rope(kpe[:, :, None, :], rope_mla)
    k = jnp.concatenate([kv[..., :MLA_NOPE],
                         jnp.broadcast_to(k_pe, (b, s, MLA_HEADS, MLA_ROPE))], -1)
    o_b = _causal_attention(_heads(q), _heads(k), _heads(kv[..., MLA_NOPE:]),
                            (MLA_NOPE + MLA_ROPE) ** -0.5)
    o_b = o_b.transpose(0, 2, 1, 3).reshape(b, s, B_WIDTH).astype(x.dtype)

    qm = _apply_rope(mq.reshape(b, s, MOBA_HEADS, MOBA_HEAD_DIM), rope_moba)
    km = _apply_rope(mk.reshape(b, s, MOBA_HEADS, MOBA_HEAD_DIM), rope_moba)
    o_c = _moba_attention(_heads(qm), _heads(km), _heads(mv.reshape(b, s, MOBA_HEADS, MOBA_HEAD_DIM)))
    o_c = o_c.transpose(0, 2, 1, 3).reshape(b, s, C_WIDTH).astype(x.dtype)

    g = jax.nn.sigmoid(gates.astype(F32)).reshape(b, s, N_BRANCHES, D_MODEL)
    y = (g[..., 0, :] * (o_a @ w_branch[:A_WIDTH])
         + g[..., 1, :] * (o_b @ w_branch[A_WIDTH:A_WIDTH + B_WIDTH])
         + g[..., 2, :] * (o_c @ w_branch[A_WIDTH + B_WIDTH:]))
    return y.astype(x.dtype) @ w_out


def _moe_ffn(x, router_w, router_bias, w_gate, w_up, w_down, ws_gate, ws_up, ws_down):
    b, s, d = x.shape
    t = b * s
    xt = x.reshape(t, d)
    scores = jax.nn.sigmoid((xt @ router_w).astype(F32))
    choice = scores + router_bias.astype(F32)
    grp = choice.reshape(t, N_GROUPS, N_EXPERTS // N_GROUPS)
    grp_score = lax.top_k(grp, 2)[0].sum(-1)
    _, grp_idx = lax.top_k(grp_score, TOPK_GROUPS)
    grp_keep = jax.nn.one_hot(grp_idx, N_GROUPS, dtype=F32).sum(1) > 0
    choice = jnp.where(jnp.repeat(grp_keep, N_EXPERTS // N_GROUPS, axis=1), choice, -jnp.inf)
    _, idx = lax.top_k(choice, TOPK)
    w = jnp.take_along_axis(scores, idx, axis=-1)
    w = w / w.sum(-1, keepdims=True) * ROUTED_SCALE
    gates = jnp.zeros((t, N_EXPERTS), F32).at[jnp.arange(t)[:, None], idx].set(w)
    c = math.gcd(t, MOE_TOKEN_BLOCK)

    def chunk(args):
        xc, gc = args
        hid = jax.nn.silu(jnp.einsum('cd,edf->cef', xc, w_gate)) * jnp.einsum('cd,edf->cef', xc, w_up)
        return jnp.einsum('cef,efd->cd', hid * gc[..., None].astype(hid.dtype), w_down)

    routed = lax.map(chunk, (xt.reshape(t // c, c, d), gates.reshape(t // c, c, N_EXPERTS)))
    shared = (jax.nn.silu(x @ ws_gate) * (x @ ws_up)) @ ws_down
    return (routed.reshape(b, s, d) + shared).astype(x.dtype)


def setup_inputs(seed: int = 0) -> dict:
    key = jax.random.key(seed)
    ks = jax.random.split(key, 32)
    L, D = DEPTH, D_MODEL

    def nrm(k, shape, scale):
        return jax.random.normal(k, shape, F32) * scale

    def gain(k, shape):
        return 1.0 + 0.02 * jax.random.normal(k, shape, F32)

    x = jax.random.normal(ks[0], (BATCH, SEQ, D), F32)
    positions = (jax.random.randint(ks[1], (BATCH, 1), 0, MAX_POS_OFFSET, dtype=jnp.int32)
                 + jnp.arange(SEQ, dtype=jnp.int32)[None, :])
    w_branch = jnp.concatenate([nrm(ks[12], (L, A_WIDTH, D), A_WIDTH ** -0.5),
                                nrm(ks[13], (L, B_WIDTH, D), B_WIDTH ** -0.5),
                                nrm(ks[14], (L, C_WIDTH, D), C_WIDTH ** -0.5)], axis=1)
    return {
        'x': x,
        'positions': positions,
        'w_in': nrm(ks[2], (L, D, IN_COLS), D ** -0.5),
        'diff_lambda_q1': nrm(ks[3], (L, DIFF_HALF_DIM), 0.1),
        'diff_lambda_k1': nrm(ks[4], (L, DIFF_HALF_DIM), 0.1),
        'diff_lambda_q2': nrm(ks[5], (L, DIFF_HALF_DIM), 0.1),
        'diff_lambda_k2': nrm(ks[6], (L, DIFF_HALF_DIM), 0.1),
        'diff_subln_g': gain(ks[7], (L, DIFF_V_DIM)),
        'mla_q_norm_g': gain(ks[8], (L, MLA_Q_RANK)),
        'mla_w_uq': nrm(ks[9], (L, MLA_Q_RANK, MLA_HEADS * (MLA_NOPE + MLA_ROPE)), MLA_Q_RANK ** -0.5),
        'mla_kv_norm_g': gain(ks[10], (L, MLA_KV_RANK)),
        'mla_w_ukv': nrm(ks[11], (L, MLA_KV_RANK, MLA_HEADS * (MLA_NOPE + MLA_V)), MLA_KV_RANK ** -0.5),
        'w_branch': w_branch,
        'w_out': nrm(ks[15], (L, D, D), D ** -0.5 * DEEPNORM_BETA),
        'ln_mix_g': gain(ks[16], (L, D)),
        'ln_mix_b': nrm(ks[17], (L, D), 0.02),
        'router_w': nrm(ks[18], (L, D, N_EXPERTS), D ** -0.5),
        'router_bias': nrm(ks[19], (L, N_EXPERTS), 0.01),
        'expert_w_gate': nrm(ks[20], (L, N_EXPERTS, D, EXPERT_FF), D ** -0.5),
        'expert_w_up': nrm(ks[21], (L, N_EXPERTS, D, EXPERT_FF), D ** -0.5),
        'expert_w_down': nrm(ks[22], (L, N_EXPERTS, EXPERT_FF, D), EXPERT_FF ** -0.5 * DEEPNORM_BETA),
        'shared_w_gate': nrm(ks[23], (L, D, SHARED_FF), D ** -0.5),
        'shared_w_up': nrm(ks[24], (L, D, SHARED_FF), D ** -0.5),
        'shared_w_down': nrm(ks[25], (L, SHARED_FF, D), SHARED_FF ** -0.5 * DEEPNORM_BETA),
        'ln_ffn_g': gain(ks[26], (L, D)),
        'ln_ffn_b': nrm(ks[27], (L, D), 0.02),
    }


def reference(x, positions, w_in, diff_lambda_q1, diff_lambda_k1, diff_lambda_q2, diff_lambda_k2,
              diff_subln_g, mla_q_norm_g, mla_w_uq, mla_kv_norm_g, mla_w_ukv, w_branch, w_out,
              ln_mix_g, ln_mix_b, router_w, router_bias, expert_w_gate, expert_w_up, expert_w_down,
              shared_w_gate, shared_w_up, shared_w_down, ln_ffn_g, ln_ffn_b):
    ropes = (_rope_tables(positions, DIFF_ROT),
             _rope_tables(positions, MLA_ROPE),
             _rope_tables(positions, MOBA_ROT))
    for l in range(DEPTH):
        mix = _token_mixer(x, ropes, l, w_in[l], diff_lambda_q1[l], diff_lambda_k1[l],
                           diff_lambda_q2[l], diff_lambda_k2[l], diff_subln_g[l], mla_q_norm_g[l],
                           mla_w_uq[l], mla_kv_norm_g[l], mla_w_ukv[l], w_branch[l], w_out[l])
        x = _layer_norm(DEEPNORM_ALPHA * x + mix, ln_mix_g[l], ln_mix_b[l])
        ffn = _moe_ffn(x, router_w[l], router_bias[l], expert_w_gate[l], expert_w_up[l],
                       expert_w_down[l], shared_w_gate[l], shared_w_up[l], shared_w_down[l])
        x = _layer_norm(DEEPNORM_ALPHA * x + ffn, ln_ffn_g[l], ln_ffn_b[l])
    return x
```

```python
import functools
import math

import numpy as np
import jax
import jax.numpy as jnp
from jax import lax
from jax.experimental import pallas as pl
from jax.experimental.pallas import tpu as pltpu

F32 = jnp.float32
BF16 = jnp.bfloat16
I32 = jnp.int32
U32 = jnp.uint32

D_MODEL = 2048
DEPTH = 2
ROPE_THETA = 500000.0
NORM_EPS = 1e-5

DIFF_HEADS = 4
DIFF_HALF_DIM = 64
DIFF_ROT = 16
MLA_HEADS = 8
MLA_Q_RANK = 512
MLA_KV_RANK = 256
MLA_NOPE = 128
MLA_ROPE = 64
MLA_V = 128
MOBA_HEADS = 4
MOBA_HEAD_DIM = 128
MOBA_BLOCK = 256
MOBA_TOPK = 3
MOBA_ROT = 32
A_WIDTH = 512
B_WIDTH = 1024
C_WIDTH = 512

N_EXPERTS = 64
N_GROUPS = 8
GROUP_SIZE = N_EXPERTS // N_GROUPS
TOPK_GROUPS = 4
TOPK = 8
EXPERT_FF = 512
ROUTED_SCALE = 2.5

DEEPNORM_ALPHA = (2 * DEPTH) ** 0.25

LANES = 128
NEG_BIG = -1e30

PROJ_TN = 512
MODE_PLAIN, MODE_ROPE_DIFF, MODE_ROPE_MLA, MODE_ROPE_MOBA, MODE_RMS, MODE_CKV, MODE_SIGMOID = range(7)
_ROPE_OF_MODE = {MODE_ROPE_DIFF: (0, DIFF_ROT // 2), MODE_ROPE_MLA: (1, MLA_ROPE // 2),
                 MODE_ROPE_MOBA: (2, MOBA_ROT // 2)}
_ROPE_PERIOD = (DIFF_HALF_DIM, MLA_ROPE, MOBA_HEAD_DIM)
_ROPE_ROT = (DIFF_ROT, MLA_ROPE, MOBA_ROT)

IN_TILE_MODES = ([MODE_ROPE_DIFF, MODE_ROPE_DIFF, MODE_PLAIN, MODE_RMS, MODE_CKV,
                  MODE_ROPE_MOBA, MODE_ROPE_MOBA, MODE_PLAIN] + [MODE_SIGMOID] * 12)
IN_COLS_PADDED = PROJ_TN * len(IN_TILE_MODES)

VMEM_LIMIT = 56 * 1024 * 1024

ATT_TQ = 256
EXPERT_TM = 256
ROW_TM = 256
ROUTER_TM = 1024


def _cparams(sem):
    return pltpu.CompilerParams(dimension_semantics=sem, vmem_limit_bytes=VMEM_LIMIT)


def _rope_table_kernel(pos_ref, c_ref, cos_ref, sin_ref):
    pos = pos_ref[...].astype(F32)
    for p in range(3):
        ang = pos * c_ref[p:p + 1, :]
        cos_ref[p] = jnp.cos(ang)
        sin_ref[p] = jnp.sin(ang) * c_ref[3 + p:4 + p, :]


def _rope_tables(positions):
    t = positions.size
    lane = jnp.arange(LANES)
    rows = []
    signs = []
    for period, rot in zip(_ROPE_PERIOD, _ROPE_ROT):
        half = rot // 2
        inv_freq = ROPE_THETA ** (-jnp.arange(0, rot, 2, dtype=F32) / rot)
        cp = lane % period
        active = cp < rot
        rows.append(jnp.where(active, inv_freq[cp % half], 0.0))
        signs.append(jnp.where(active, jnp.where(cp < half, -1.0, 1.0), 0.0))
    consts = jnp.stack(rows + signs + [jnp.zeros((LANES,), F32)] * 2).astype(F32)
    tm = min(1024, t)
    cos, sin = pl.pallas_call(
        _rope_table_kernel,
        grid=(t // tm,),
        in_specs=[pl.BlockSpec((tm, 1), lambda i: (i, 0)),
                  pl.BlockSpec((8, LANES), lambda i: (0, 0))],
        out_specs=[pl.BlockSpec((3, tm, LANES), lambda i: (0, i, 0)),
                   pl.BlockSpec((3, tm, LANES), lambda i: (0, i, 0))],
        out_shape=[jax.ShapeDtypeStruct((3, t, LANES), F32)] * 2,
        compiler_params=_cparams(("parallel",)),
        name="rope_tables",
    )(positions.reshape(t, 1), consts)
    return cos, sin


def _proj_kernel(mode_ref, x_ref, w_ref, cos_ref, sin_ref, g_ref, o_ref):
    j = pl.program_id(1)
    mode = mode_ref[j]
    y = jnp.dot(x_ref[...], w_ref[...], preferred_element_type=F32)
    tm, tn = y.shape
    lane = lax.broadcasted_iota(I32, (tm, LANES), 1)

    def rope_chunk(yc, table, half):
        first = (lane % _ROPE_PERIOD[table]) < half
        swapped = jnp.where(first, pltpu.roll(yc, LANES - half, 1), pltpu.roll(yc, half, 1))
        return yc * cos_ref[table] + swapped * sin_ref[table]

    @pl.when(mode == MODE_PLAIN)
    def _():
        o_ref[...] = y.astype(o_ref.dtype)

    for rope_mode, (table, half) in _ROPE_OF_MODE.items():
        @pl.when(mode == rope_mode)
        def _(table=table, half=half):
            for c in range(tn // LANES):
                sl = slice(c * LANES, (c + 1) * LANES)
                o_ref[:, sl] = rope_chunk(y[:, sl], table, half).astype(o_ref.dtype)

    @pl.when(mode == MODE_RMS)
    def _():
        r = lax.rsqrt(jnp.mean(y * y, axis=-1, keepdims=True) + NORM_EPS)
        o_ref[...] = (y * r * g_ref[0:1, :]).astype(o_ref.dtype)

    @pl.when(mode == MODE_CKV)
    def _():
        ckv = y[:, :MLA_KV_RANK]
        r = lax.rsqrt(jnp.mean(ckv * ckv, axis=-1, keepdims=True) + NORM_EPS)
        o_ref[:, :MLA_KV_RANK] = (ckv * r * g_ref[1:2, :MLA_KV_RANK]).astype(o_ref.dtype)
        sl = slice(MLA_KV_RANK, MLA_KV_RANK + LANES)
        table, half = _ROPE_OF_MODE[MODE_ROPE_MLA]
        o_ref[:, sl] = rope_chunk(y[:, sl], table, half).astype(o_ref.dtype)
        o_ref[:, MLA_KV_RANK + LANES:] = jnp.zeros((tm, tn - MLA_KV_RANK - LANES), o_ref.dtype)

    @pl.when(mode == MODE_SIGMOID)
    def _():
        o_ref[...] = (1.0 / (1.0 + jnp.exp(-y))).astype(o_ref.dtype)


def _proj(x, x_col_block, k_dim, w, modes, cos, sin, gains, tm):
    t = x.shape[0]
    n = w.shape[1]
    assert n % PROJ_TN == 0 and len(modes) == n // PROJ_TN and w.shape[0] == k_dim
    grid_spec = pltpu.PrefetchScalarGridSpec(
        num_scalar_prefetch=1,
        grid=(t // tm, n // PROJ_TN),
        in_specs=[pl.BlockSpec((tm, k_dim), lambda i, j, m: (i, x_col_block)),
                  pl.BlockSpec((k_dim, PROJ_TN), lambda i, j, m: (0, j)),
                  pl.BlockSpec((3, tm, LANES), lambda i, j, m: (0, i, 0)),
                  pl.BlockSpec((3, tm, LANES), lambda i, j, m: (0, i, 0)),
                  pl.BlockSpec((8, PROJ_TN), lambda i, j, m: (0, 0))],
        out_specs=pl.BlockSpec((tm, PROJ_TN), lambda i, j, m: (i, j)),
    )
    return pl.pallas_call(
        _proj_kernel,
        grid_spec=grid_spec,
        out_shape=jax.ShapeDtypeStruct((t, n), BF16),
        compiler_params=_cparams(("parallel", "arbitrary")),
        name="proj",
    )(jnp.asarray(modes, I32), x, w, cos, sin, gains)


_NT_DIMS = (((1,), (1,)), ((), ()))


def _online_softmax_step(s, v, m_ref, l_ref, acc_ref):
    m_prev = m_ref[...]
    m_new = jnp.maximum(m_prev, jnp.max(s, axis=-1, keepdims=True))
    alpha = jnp.exp(m_prev - m_new)
    p = jnp.exp(s - m_new)
    l_ref[...] = alpha * l_ref[...] + jnp.sum(p, axis=-1, keepdims=True)
    acc_ref[...] = alpha * acc_ref[...] + jnp.dot(p.astype(BF16), v, preferred_element_type=F32)
    m_ref[...] = m_new


def _init_softmax_state(m_ref, l_ref, acc_ref):
    m_ref[...] = jnp.full(m_ref.shape, NEG_BIG, F32)
    l_ref[...] = jnp.zeros(l_ref.shape, F32)
    acc_ref[...] = jnp.zeros(acc_ref.shape, F32)


def _causal_mask(tq):
    row = lax.broadcasted_iota(I32, (tq, tq), 0)
    col = lax.broadcasted_iota(I32, (tq, tq), 1)
    return col <= row


def _diff_attn_kernel(lam_ref, g_ref, q_ref, k_ref, v_ref, o_ref,
                      m1_ref, l1_ref, a1_ref, m2_ref, l2_ref, a2_ref, *, tq, lambda_init):
    qi = pl.program_id(2)
    scale = DIFF_HALF_DIM ** -0.5
    _init_softmax_state(m1_ref, l1_ref, a1_ref)
    _init_softmax_state(m2_ref, l2_ref, a2_ref)
    q = q_ref[...]
    lane = lax.broadcasted_iota(I32, q.shape, 1)
    q1 = jnp.where(lane < DIFF_HALF_DIM, q, jnp.zeros_like(q))
    q2 = jnp.where(lane >= DIFF_HALF_DIM, q, jnp.zeros_like(q))

    def step(j, mask):
        start = pl.multiple_of(j * tq, tq)
        k = k_ref[pl.ds(start, tq), :]
        v = v_ref[pl.ds(start, tq), :]
        s1 = lax.dot_general(q1, k, _NT_DIMS, preferred_element_type=F32) * scale
        s2 = lax.dot_general(q2, k, _NT_DIMS, preferred_element_type=F32) * scale
        if mask is not None:
            s1 = jnp.where(mask, s1, NEG_BIG)
            s2 = jnp.where(mask, s2, NEG_BIG)
        _online_softmax_step(s1, v, m1_ref, l1_ref, a1_ref)
        _online_softmax_step(s2, v, m2_ref, l2_ref, a2_ref)

    step(qi, _causal_mask(tq))

    def body(j, c):
        step(j, None)
        return c
    lax.fori_loop(0, qi, body, 0)

    lp = lam_ref[...]
    lam = (jnp.exp(jnp.sum(lp[0:1] * lp[1:2], axis=-1, keepdims=True))
           - jnp.exp(jnp.sum(lp[2:3] * lp[3:4], axis=-1, keepdims=True)) + lambda_init)
    o = a1_ref[...] / l1_ref[...] - lam * (a2_ref[...] / l2_ref[...])
    r = lax.rsqrt(jnp.mean(o * o, axis=-1, keepdims=True) + NORM_EPS)
    o_ref[...] = (o * r * g_ref[...] * (1.0 - lambda_init)).astype(o_ref.dtype)


def _diff_attention(proj, lam_parts, subln_g, b, s, lambda_init):
    t = b * s
    tq = min(ATT_TQ, s)
    nq = s // tq
    kern = functools.partial(_diff_attn_kernel, tq=tq, lambda_init=lambda_init)
    stat = pltpu.VMEM((tq, 1), F32)
    acc = pltpu.VMEM((tq, LANES), F32)
    return pl.pallas_call(
        kern,
        grid=(b, DIFF_HEADS, nq),
        in_specs=[pl.BlockSpec((4, DIFF_HALF_DIM), lambda bi, h, qi: (0, 0)),
                  pl.BlockSpec((1, LANES), lambda bi, h, qi: (0, 0)),
                  pl.BlockSpec((tq, LANES), lambda bi, h, qi: (bi * nq + qi, h)),
                  pl.BlockSpec((s, LANES), lambda bi, h, qi: (bi, DIFF_HEADS + h)),
                  pl.BlockSpec((s, LANES), lambda bi, h, qi: (bi, 2 * DIFF_HEADS + h))],
        out_specs=pl.BlockSpec((tq, LANES), lambda bi, h, qi: (bi * nq + qi, h)),
        out_shape=jax.ShapeDtypeStruct((t, A_WIDTH), BF16),
        scratch_shapes=[stat, stat, acc, stat, stat, acc],
        compiler_params=_cparams(("parallel", "parallel", "arbitrary")),
        name="diff_attention",
    )(lam_parts, subln_g.reshape(1, LANES), proj, proj, proj)


def _mla_attn_kernel(qn_ref, qr_ref, kn_ref, kpe_ref, v_ref, o_ref, m_ref, l_ref, acc_ref, *, tq):
    h = pl.program_id(1)
    qi = pl.program_id(2)
    scale = (MLA_NOPE + MLA_ROPE) ** -0.5
    _init_softmax_state(m_ref, l_ref, acc_ref)
    qn = qn_ref[...]
    qr = qr_ref[...]
    lane = lax.broadcasted_iota(I32, qr.shape, 1)
    mine = (lane >= MLA_ROPE) == (h % 2 == 1)
    qr = jnp.where(mine, qr, jnp.zeros_like(qr))

    def step(j, mask):
        start = pl.multiple_of(j * tq, tq)
        kn = kn_ref[pl.ds(start, tq), :]
        kp = kpe_ref[pl.ds(start, tq), :]
        v = v_ref[pl.ds(start, tq), :]
        sc = (lax.dot_general(qn, kn, _NT_DIMS, preferred_element_type=F32)
              + lax.dot_general(qr, kp, _NT_DIMS, preferred_element_type=F32)) * scale
        if mask is not None:
            sc = jnp.where(mask, sc, NEG_BIG)
        _online_softmax_step(sc, v, m_ref, l_ref, acc_ref)

    step(qi, _causal_mask(tq))

    def body(j, c):
        step(j, None)
        return c
    lax.fori_loop(0, qi, body, 0)
    o_ref[...] = (acc_ref[...] / l_ref[...]).astype(o_ref.dtype)


def _mla_attention(qbuf, kvbuf, proj, b, s):
    t = b * s
    tq = min(ATT_TQ, s)
    nq = s // tq
    kpe_block = (4 * PROJ_TN + MLA_KV_RANK) // LANES
    return pl.pallas_call(
        functools.partial(_mla_attn_kernel, tq=tq),
        grid=(b, MLA_HEADS, nq),
        in_specs=[pl.BlockSpec((tq, LANES), lambda bi, h, qi: (bi * nq + qi, h)),
                  pl.BlockSpec((tq, LANES), lambda bi, h, qi: (bi * nq + qi, MLA_HEADS + h // 2)),
                  pl.BlockSpec((s, LANES), lambda bi, h, qi: (bi, h)),
                  pl.BlockSpec((s, LANES), lambda bi, h, qi: (bi, kpe_block)),
                  pl.BlockSpec((s, LANES), lambda bi, h, qi: (bi, MLA_HEADS + h))],
        out_specs=pl.BlockSpec((tq, LANES), lambda bi, h, qi: (bi * nq + qi, h)),
        out_shape=jax.ShapeDtypeStruct((t, B_WIDTH), BF16),
        scratch_shapes=[pltpu.VMEM((tq, 1), F32), pltpu.VMEM((tq, 1), F32), pltpu.VMEM((tq, LANES), F32)],
        compiler_params=_cparams(("parallel", "parallel", "arbitrary")),
        name="mla_attention",
    )(qbuf, qbuf, kvbuf, proj, kvbuf)


def _moba_attn_kernel(q_ref, k_ref, v_ref, o_ref, m_ref, l_ref, acc_ref, kmean_ref, *, nb):
    qi = pl.program_id(2)
    tq = MOBA_BLOCK
    scale = MOBA_HEAD_DIM ** -0.5
    _init_softmax_state(m_ref, l_ref, acc_ref)

    @pl.when(qi == 0)
    def _():
        kmean_ref[...] = jnp.zeros(kmean_ref.shape, F32)
        for j in range(nb):
            kb = k_ref[j * tq:(j + 1) * tq, :].astype(F32)
            kmean_ref[j:j + 1, :] = jnp.sum(kb, axis=0, keepdims=True) * (1.0 / tq)

    q = q_ref[...]
    km = kmean_ref[...]
    km_hi = km.astype(BF16)
    km_lo = (km - km_hi.astype(F32)).astype(BF16)
    gate = (lax.dot_general(q, km_hi, _NT_DIMS, preferred_element_type=F32)
            + lax.dot_general(q, km_lo, _NT_DIMS, preferred_element_type=F32))
    lane = lax.broadcasted_iota(I32, gate.shape, 1)
    fully_past = lane < qi
    g = jnp.where(fully_past, gate, -jnp.inf)
    cnt = jnp.zeros(gate.shape, F32)
    for jp in range(nb):
        col = g[:, jp:jp + 1]
        ahead = (col > g) | ((col == g) & (jp < lane))
        cnt = cnt + jnp.where(ahead, 1.0, 0.0)
    sel = jnp.where((cnt < MOBA_TOPK) & fully_past, 1.0, 0.0)

    def step(j, mask):
        start = pl.multiple_of(j * tq, tq)
        k = k_ref[pl.ds(start, tq), :]
        v = v_ref[pl.ds(start, tq), :]
        sc = lax.dot_general(q, k, _NT_DIMS, preferred_element_type=F32) * scale
        sc = jnp.where(mask, sc, NEG_BIG)
        _online_softmax_step(sc, v, m_ref, l_ref, acc_ref)

    step(qi, _causal_mask(tq))

    def body(j, c):
        kept = jnp.sum(jnp.where(lane == j, sel, 0.0), axis=-1, keepdims=True) > 0.0
        step(j, kept)
        return c
    lax.fori_loop(0, qi, body, 0)
    o_ref[...] = (acc_ref[...] / l_ref[...]).astype(o_ref.dtype)


def _moba_attention(proj, b, s):
    assert s % MOBA_BLOCK == 0
    t = b * s
    tq = MOBA_BLOCK
    nq = s // tq
    base = 5 * PROJ_TN // LANES
    return pl.pallas_call(
        functools.partial(_moba_attn_kernel, nb=nq),
        grid=(b, MOBA_HEADS, nq),
        in_specs=[pl.BlockSpec((tq, LANES), lambda bi, h, qi: (bi * nq + qi, base + h)),
                  pl.BlockSpec((s, LANES), lambda bi, h, qi: (bi, base + MOBA_HEADS + h)),
                  pl.BlockSpec((s, LANES), lambda bi, h, qi: (bi, base + 2 * MOBA_HEADS + h))],
        out_specs=pl.BlockSpec((tq, LANES), lambda bi, h, qi: (bi * nq + qi, h)),
        out_shape=jax.ShapeDtypeStruct((t, C_WIDTH), BF16),
        scratch_shapes=[pltpu.VMEM((tq, 1), F32), pltpu.VMEM((tq, 1), F32), pltpu.VMEM((tq, LANES), F32),
                        pltpu.VMEM((LANES, LANES), F32)],
        compiler_params=_cparams(("parallel", "parallel", "arbitrary")),
        name="moba_attention",
    )(proj, proj, proj)


def _layer_norm_rows(z, g, b):
    mu = jnp.mean(z, axis=-1, keepdims=True)
    zc = z - mu
    var = jnp.mean(zc * zc, axis=-1, keepdims=True)
    return zc * lax.rsqrt(var + NORM_EPS) * g + b


def _pack_halves(z):
    n = z.shape[1] // 2
    lo = lax.bitcast_convert_type(z[:, :n].astype(BF16).astype(F32), U32)
    hi = lax.bitcast_convert_type(z[:, n:].astype(BF16).astype(F32), U32)
    return (lo >> 16) | hi


def _unpack_halves(u):
    lo = lax.bitcast_convert_type(u << 16, F32)
    hi = lax.bitcast_convert_type(u & jnp.uint32(0xFFFF0000), F32)
    return lo, hi


def _merge_kernel(oa_ref, ob_ref, oc_ref, g0_ref, g1_ref, g2_ref, x_ref, wb_ref, wo_ref,
                  lng_ref, lnb_ref, xo_ref, xb_ref, xp_ref):
    ya = jnp.dot(oa_ref[...], wb_ref[:A_WIDTH, :], preferred_element_type=F32)
    yb = jnp.dot(ob_ref[...], wb_ref[A_WIDTH:A_WIDTH + B_WIDTH, :], preferred_element_type=F32)
    yc = jnp.dot(oc_ref[...], wb_ref[A_WIDTH + B_WIDTH:, :], preferred_element_type=F32)
    y = g0_ref[...].astype(F32) * ya + g1_ref[...].astype(F32) * yb + g2_ref[...].astype(F32) * yc
    mix = jnp.dot(y.astype(BF16), wo_ref[...], preferred_element_type=F32)
    z = _layer_norm_rows(DEEPNORM_ALPHA * x_ref[...] + mix, lng_ref[...], lnb_ref[...])
    xo_ref[...] = z
    xb_ref[...] = z.astype(BF16)
    xp_ref[...] = _pack_halves(z)


def _merge(oa, ob, oc, proj, x, w_branch, w_out, ln_g, ln_b):
    t = x.shape[0]
    tm = min(256, t)
    gate_base = 8 * PROJ_TN // D_MODEL
    row = lambda i: (i, 0)
    whole = lambda i: (0, 0)
    return pl.pallas_call(
        _merge_kernel,
        grid=(t // tm,),
        in_specs=[pl.BlockSpec((tm, A_WIDTH), row),
                  pl.BlockSpec((tm, B_WIDTH), row),
                  pl.BlockSpec((tm, C_WIDTH), row),
                  pl.BlockSpec((tm, D_MODEL), lambda i: (i, gate_base)),
                  pl.BlockSpec((tm, D_MODEL), lambda i: (i, gate_base + 1)),
                  pl.BlockSpec((tm, D_MODEL), lambda i: (i, gate_base + 2)),
                  pl.BlockSpec((tm, D_MODEL), row),
                  pl.BlockSpec((D_MODEL, D_MODEL), whole),
                  pl.BlockSpec((D_MODEL, D_MODEL), whole),
                  pl.BlockSpec((1, D_MODEL), whole),
                  pl.BlockSpec((1, D_MODEL), whole)],
        out_specs=[pl.BlockSpec((tm, D_MODEL), row),
                   pl.BlockSpec((tm, D_MODEL), row),
                   pl.BlockSpec((tm, D_MODEL // 2), row)],
        out_shape=[jax.ShapeDtypeStruct((t, D_MODEL), F32),
                   jax.ShapeDtypeStruct((t, D_MODEL), BF16),
                   jax.ShapeDtypeStruct((t, D_MODEL // 2), U32)],
        compiler_params=_cparams(("parallel",)),
        name="merge_outproj_ln",
    )(oa, ob, oc, proj, proj, proj, x, w_branch, w_out, ln_g.reshape(1, -1), ln_b.reshape(1, -1))


def _split_bf16(a):
    hi = a.astype(BF16)
    lo = (a - hi.astype(F32)).astype(BF16)
    return hi, lo


def _router_kernel(x_ref, wt_ref, bias_ref, upper_ref, lower_ref,
                   e8_ref, r8_ref, w8_ref, cnt_ref, carry_ref):
    i = pl.program_id(0)

    @pl.when(i == 0)
    def _():
        carry_ref[...] = jnp.zeros(carry_ref.shape, F32)

    xh, xl = _split_bf16(x_ref[...])
    wh, wl = _split_bf16(wt_ref[...])
    logits = (lax.dot_general(wh, xh, _NT_DIMS, preferred_element_type=F32)
              + lax.dot_general(wh, xl, _NT_DIMS, preferred_element_type=F32)
              + lax.dot_general(wl, xh, _NT_DIMS, preferred_element_type=F32))
    scores = 1.0 / (1.0 + jnp.exp(-logits))
    choice = scores + bias_ref[:, 0:1]
    tm = choice.shape[1]
    sub = lax.broadcasted_iota(I32, (GROUP_SIZE, tm), 0)

    group_rows = []
    for g in range(N_GROUPS):
        cg = choice[g * GROUP_SIZE:(g + 1) * GROUP_SIZE, :]
        m1 = jnp.max(cg, axis=0, keepdims=True)
        first = jnp.min(jnp.where(cg == m1, sub, GROUP_SIZE), axis=0, keepdims=True)
        m2 = jnp.max(jnp.where(sub == first, -jnp.inf, cg), axis=0, keepdims=True)
        group_rows.append(m1 + m2)
    gs = jnp.concatenate(group_rows, axis=0)
    ahead = jnp.zeros(gs.shape, F32)
    for gp in range(N_GROUPS):
        rowv = gs[gp:gp + 1, :]
        ahead = ahead + jnp.where((rowv > gs) | ((rowv == gs) & (gp < sub)), 1.0, 0.0)
    keep_group = ahead < TOPK_GROUPS
    masked = jnp.concatenate(
        [jnp.where(keep_group[g:g + 1, :], choice[g * GROUP_SIZE:(g + 1) * GROUP_SIZE, :], -jnp.inf)
         for g in range(N_GROUPS)], axis=0)

    eidx = lax.broadcasted_iota(I32, masked.shape, 0)
    ahead = jnp.zeros(masked.shape, F32)
    for ep in range(N_EXPERTS):
        rowv = masked[ep:ep + 1, :]
        ahead = ahead + jnp.where((rowv > masked) | ((rowv == masked) & (ep < eidx)), 1.0, 0.0)
    sel = ahead < TOPK
    picked = jnp.where(sel, scores, 0.0)
    gates = picked / jnp.sum(picked, axis=0, keepdims=True) * ROUTED_SCALE
    self32 = jnp.where(sel, 1.0, 0.0)
    selb = self32.astype(BF16)

    rank = jnp.dot(selb, upper_ref[...], preferred_element_type=F32) + carry_ref[:, 0:1]
    slot = jnp.dot(lower_ref[...], selb, preferred_element_type=F32)
    carry_ref[...] = carry_ref[...] + jnp.sum(self32, axis=1, keepdims=True)
    cnt_ref[...] = carry_ref[...].astype(I32)

    eidf = eidx.astype(F32)
    e_rows, r_rows, w_rows = [], [], []
    for k in range(TOPK):
        hit = sel & (slot == float(k))
        e_rows.append(jnp.sum(jnp.where(hit, eidf, 0.0), axis=0, keepdims=True))
        r_rows.append(jnp.sum(jnp.where(hit, rank, 0.0), axis=0, keepdims=True))
        w_rows.append(jnp.sum(jnp.where(hit, gates, 0.0), axis=0, keepdims=True))
    e8_ref[...] = jnp.concatenate(e_rows, axis=0).astype(I32)
    r8_ref[...] = jnp.concatenate(r_rows, axis=0).astype(I32)
    w8_ref[...] = jnp.concatenate(w_rows, axis=0)


def _router(x, router_w, router_bias):
    t = x.shape[0]
    tm = min(ROUTER_TM, t)
    upper = (jnp.arange(tm)[:, None] < jnp.arange(tm)[None, :]).astype(BF16)
    lower = (jnp.arange(N_EXPERTS)[None, :] < jnp.arange(N_EXPERTS)[:, None]).astype(BF16)
    bias = jnp.broadcast_to(router_bias.astype(F32)[:, None], (N_EXPERTS, LANES))
    whole = lambda i: (0, 0)
    col = lambda i: (0, i)
    return pl.pallas_call(
        _router_kernel,
        grid=(t // tm,),
        in_specs=[pl.BlockSpec((tm, D_MODEL), lambda i: (i, 0)),
                  pl.BlockSpec((N_EXPERTS, D_MODEL), whole),
                  pl.BlockSpec((N_EXPERTS, LANES), whole),
                  pl.BlockSpec((tm, tm), whole),
                  pl.BlockSpec((N_EXPERTS, N_EXPERTS), whole)],
        out_specs=[pl.BlockSpec((TOPK, tm), col),
                   pl.BlockSpec((TOPK, tm), col),
                   pl.BlockSpec((TOPK, tm), col),
                   pl.BlockSpec((N_EXPERTS, LANES), whole)],
        out_shape=[jax.ShapeDtypeStruct((TOPK, t), I32),
                   jax.ShapeDtypeStruct((TOPK, t), I32),
                   jax.ShapeDtypeStruct((TOPK, t), F32),
                   jax.ShapeDtypeStruct((N_EXPERTS, LANES), I32)],
        scratch_shapes=[pltpu.VMEM((N_EXPERTS, LANES), F32)],
        compiler_params=_cparams(("arbitrary",)),
        name="router",
    )(x, router_w.T, bias, upper, lower)


def _dispatch_kernel(off_ref, e8_ref, r8_ref, x_ref, xs_in_ref, xs_ref, sem):
    del xs_in_ref
    tm = x_ref.shape[0]

    def row_copy(t, k):
        pos = off_ref[e8_ref[k, t]] + r8_ref[k, t]
        return pltpu.make_async_copy(x_ref.at[pl.ds(t, 1)], xs_ref.at[pl.ds(pos, 1)], sem)

    def start(t, c):
        for k in range(TOPK):
            row_copy(t, k).start()
        return c
    lax.fori_loop(0, tm, start, 0)

    def wait(t, c):
        for k in range(TOPK):
            row_copy(t, k).wait()
        return c
    lax.fori_loop(0, tm, wait, 0)


def _dispatch(xp, e8, r8, offsets, n_rows):
    t, half = xp.shape
    tm = min(ROW_TM, t)
    grid_spec = pltpu.PrefetchScalarGridSpec(
        num_scalar_prefetch=1,
        grid=(t // tm,),
        in_specs=[pl.BlockSpec((TOPK, tm), lambda i, o: (0, i), memory_space=pltpu.SMEM),
                  pl.BlockSpec((TOPK, tm), lambda i, o: (0, i), memory_space=pltpu.SMEM),
                  pl.BlockSpec((tm, half), lambda i, o: (i, 0)),
                  pl.BlockSpec(memory_space=pl.ANY)],
        out_specs=pl.BlockSpec(memory_space=pl.ANY),
        scratch_shapes=[pltpu.SemaphoreType.DMA(())],
    )
    return pl.pallas_call(
        _dispatch_kernel,
        grid_spec=grid_spec,
        out_shape=jax.ShapeDtypeStruct((n_rows, half), U32),
        input_output_aliases={4: 0},
        compiler_params=_cparams(("arbitrary",)),
        name="dispatch",
    )(offsets, e8, r8, xp, jnp.zeros((n_rows, half), U32))


def _expert_kernel(te_ref, nv_ref, xs_ref, wg_ref, wu_ref, wd_ref, ys_ref):
    i = pl.program_id(0)

    @pl.when(i < nv_ref[0])
    def _():
        half = D_MODEL // 2
        lo, hi = _unpack_halves(xs_ref[...])
        xl = lo.astype(BF16)
        xh = hi.astype(BF16)
        gate = (jnp.dot(xl, wg_ref[0, :half, :], preferred_element_type=F32)
                + jnp.dot(xh, wg_ref[0, half:, :], preferred_element_type=F32))
        up = (jnp.dot(xl, wu_ref[0, :half, :], preferred_element_type=F32)
              + jnp.dot(xh, wu_ref[0, half:, :], preferred_element_type=F32))
        hid = (gate / (1.0 + jnp.exp(-gate))) * up
        y = jnp.dot(hid.astype(BF16), wd_ref[0], preferred_element_type=F32)
        ys_ref[...] = _pack_halves(y)


def _experts(xs, tile_expert, n_valid, w_gate, w_up, w_down):
    n_rows, half = xs.shape
    n_tiles = n_rows // EXPERT_TM
    row = lambda i, te, nv: (jnp.minimum(i, nv[0] - 1), 0)
    wsel = lambda i, te, nv: (te[i], 0, 0)
    grid_spec = pltpu.PrefetchScalarGridSpec(
        num_scalar_prefetch=2,
        grid=(n_tiles,),
        in_specs=[pl.BlockSpec((EXPERT_TM, half), row),
                  pl.BlockSpec((1, D_MODEL, EXPERT_FF), wsel),
                  pl.BlockSpec((1, D_MODEL, EXPERT_FF), wsel),
                  pl.BlockSpec((1, EXPERT_FF, D_MODEL), wsel)],
        out_specs=pl.BlockSpec((EXPERT_TM, half), row),
    )
    return pl.pallas_call(
        _expert_kernel,
        grid_spec=grid_spec,
        out_shape=jax.ShapeDtypeStruct((n_rows, half), U32),
        compiler_params=_cparams(("arbitrary",)),
        name="experts",
    )(tile_expert, n_valid, xs, w_gate, w_up, w_down)


def _combine_kernel(off_ref, e8_ref, r8_ref, w8_ref, ys_ref, x_ref, xb_ref, wsg_ref, wsu_ref, wsd_ref,
                    lng_ref, lnb_ref, xo_ref, xob_ref, buf_ref, sem):
    tm = x_ref.shape[0]

    def row_copy(t, k):
        pos = off_ref[e8_ref[k, t]] + r8_ref[k, t]
        return pltpu.make_async_copy(ys_ref.at[pl.ds(pos, 1)], buf_ref.at[k, pl.ds(t, 1)], sem)

    def start(t, c):
        for k in range(TOPK):
            row_copy(t, k).start()
        return c
    lax.fori_loop(0, tm, start, 0)

    xb = xb_ref[...]
    sg = jnp.dot(xb, wsg_ref[...], preferred_element_type=F32)
    su = jnp.dot(xb, wsu_ref[...], preferred_element_type=F32)
    hid = (sg / (1.0 + jnp.exp(-sg))) * su
    shared = jnp.dot(hid.astype(BF16), wsd_ref[...], preferred_element_type=F32)

    def wait(t, c):
        for k in range(TOPK):
            row_copy(t, k).wait()
        return c
    lax.fori_loop(0, tm, wait, 0)

    half = D_MODEL // 2
    acc_lo = jnp.zeros((tm, half), F32)
    acc_hi = jnp.zeros((tm, half), F32)
    w8 = w8_ref[...]
    for k in range(TOPK):
        lo, hi = _unpack_halves(buf_ref[k])
        wk = w8[:, k:k + 1]
        acc_lo = acc_lo + wk * lo
        acc_hi = acc_hi + wk * hi
    routed = jnp.concatenate([acc_lo, acc_hi], axis=1)
    z = _layer_norm_rows(DEEPNORM_ALPHA * x_ref[...] + (routed + shared), lng_ref[...], lnb_ref[...])
    xo_ref[...] = z
    xob_ref[...] = z.astype(BF16)


def _combine(ys, e8, r8, w8t, offsets, x, xb, ws_gate, ws_up, ws_down, ln_g, ln_b):
    t = x.shape[0]
    half = D_MODEL // 2
    tm = min(ROW_TM, t)
    row = lambda i, o: (i, 0)
    whole = lambda i, o: (0, 0)
    grid_spec = pltpu.PrefetchScalarGridSpec(
        num_scalar_prefetch=1,
        grid=(t // tm,),
        in_specs=[pl.BlockSpec((TOPK, tm), lambda i, o: (0, i), memory_space=pltpu.SMEM),
                  pl.BlockSpec((TOPK, tm), lambda i, o: (0, i), memory_space=pltpu.SMEM),
                  pl.BlockSpec((tm, TOPK), row),
                  pl.BlockSpec(memory_space=pl.ANY),
                  pl.BlockSpec((tm, D_MODEL), row),
                  pl.BlockSpec((tm, D_MODEL), row),
                  pl.BlockSpec((D_MODEL, EXPERT_FF), whole),
                  pl.BlockSpec((D_MODEL, EXPERT_FF), whole),
                  pl.BlockSpec((EXPERT_FF, D_MODEL), whole),
                  pl.BlockSpec((1, D_MODEL), whole),
                  pl.BlockSpec((1, D_MODEL), whole)],
        out_specs=[pl.BlockSpec((tm, D_MODEL), row),
                   pl.BlockSpec((tm, D_MODEL), row)],
        scratch_shapes=[pltpu.VMEM((TOPK, tm, half), U32), pltpu.SemaphoreType.DMA(())],
    )
    return pl.pallas_call(
        _combine_kernel,
        grid_spec=grid_spec,
        out_shape=[jax.ShapeDtypeStruct((t, D_MODEL), F32),
                   jax.ShapeDtypeStruct((t, D_MODEL), BF16)],
        compiler_params=_cparams(("arbitrary",)),
        name="combine_shared_ln",
    )(offsets, e8, r8, w8t, ys, x, xb, ws_gate, ws_up, ws_down, ln_g.reshape(1, -1), ln_b.reshape(1, -1))


def _in_proj_weight(w_in):
    sizes = (512, 512, 512, MLA_Q_RANK, MLA_KV_RANK, MLA_ROPE, 512, 512, 512, 3 * D_MODEL)
    offs = np.concatenate([[0], np.cumsum(sizes)])
    seg = [w_in[:, offs[n]:offs[n + 1]] for n in range(len(sizes))]
    dq, dk, dv, cq, ckv, kpe, mq, mk, mv, gates = seg
    pad = jnp.zeros((w_in.shape[0], PROJ_TN - MLA_KV_RANK - 2 * MLA_ROPE), w_in.dtype)
    return jnp.concatenate([dq, dk, dv, cq, ckv, kpe, kpe, pad, mq, mk, mv, gates], axis=1).astype(BF16)


def _uq_weight(w_uq):
    w = w_uq.reshape(MLA_Q_RANK, MLA_HEADS, MLA_NOPE + MLA_ROPE)
    return jnp.concatenate([w[:, :, :MLA_NOPE].reshape(MLA_Q_RANK, -1),
                            w[:, :, MLA_NOPE:].reshape(MLA_Q_RANK, -1)], axis=1).astype(BF16)


def _ukv_weight(w_ukv):
    w = w_ukv.reshape(MLA_KV_RANK, MLA_HEADS, MLA_NOPE + MLA_V)
    return jnp.concatenate([w[:, :, :MLA_NOPE].reshape(MLA_KV_RANK, -1),
                            w[:, :, MLA_NOPE:].reshape(MLA_KV_RANK, -1)], axis=1).astype(BF16)


def _expert_tiling(counts, n_tiles):
    tiles = (counts + EXPERT_TM - 1) // EXPERT_TM
    ends = jnp.cumsum(tiles)
    offsets = ((ends - tiles) * EXPERT_TM).astype(I32)
    tile_expert = jnp.minimum(
        jnp.searchsorted(ends, jnp.arange(n_tiles, dtype=I32), side="right"), N_EXPERTS - 1).astype(I32)
    return offsets, tile_expert, ends[-1:].astype(I32)


def kernel(x, positions, w_in, diff_lambda_q1, diff_lambda_k1, diff_lambda_q2, diff_lambda_k2, diff_subln_g, mla_q_norm_g, mla_w_uq, mla_kv_norm_g, mla_w_ukv, w_branch, w_out, ln_mix_g, ln_mix_b, router_w, router_bias, expert_w_gate, expert_w_up, expert_w_down, shared_w_gate, shared_w_up, shared_w_down, ln_ffn_g, ln_ffn_b):
    b, s, d = x.shape
    t = b * s
    depth = w_in.shape[0]
    cos, sin = _rope_tables(positions)
    xf = x.reshape(t, d)
    xb = xf.astype(BF16)
    n_tiles = t * TOPK // EXPERT_TM + N_EXPERTS
    proj_tm = min(1024, t)

    for l in range(depth):
        gains = jnp.zeros((8, PROJ_TN), F32)
        gains = gains.at[0, :].set(mla_q_norm_g[l]).at[1, :MLA_KV_RANK].set(mla_kv_norm_g[l])
        proj = _proj(xb, 0, d, _in_proj_weight(w_in[l]), IN_TILE_MODES, cos, sin, gains, proj_tm)
        qbuf = _proj(proj, 3, MLA_Q_RANK, _uq_weight(mla_w_uq[l]),
                     [MODE_PLAIN, MODE_PLAIN, MODE_ROPE_MLA], cos, sin, gains, proj_tm)
        kvbuf = _proj(proj, 4 * PROJ_TN // MLA_KV_RANK, MLA_KV_RANK, _ukv_weight(mla_w_ukv[l]),
                      [MODE_PLAIN] * 4, cos, sin, gains, proj_tm)

        lambda_init = 0.8 - 0.6 * math.exp(-0.3 * l)
        lam_parts = jnp.stack([diff_lambda_q1[l], diff_lambda_k1[l],
                               diff_lambda_q2[l], diff_lambda_k2[l]]).astype(F32)
        oa = _diff_attention(proj, lam_parts, diff_subln_g[l], b, s, lambda_init)
        ob = _mla_attention(qbuf, kvbuf, proj, b, s)
        oc = _moba_attention(proj, b, s)

        x1, x1b, x1p = _merge(oa, ob, oc, proj, xf, w_branch[l].astype(BF16), w_out[l].astype(BF16),
                              ln_mix_g[l], ln_mix_b[l])

        e8, r8, w8, counts = _router(x1, router_w[l], router_bias[l])
        offsets, tile_expert, n_valid = _expert_tiling(counts[:, 0], n_tiles)
        xs = _dispatch(x1p, e8, r8, offsets, n_tiles * EXPERT_TM)
        ys = _experts(xs, tile_expert, n_valid, expert_w_gate[l].astype(BF16),
                      expert_w_up[l].astype(BF16), expert_w_down[l].astype(BF16))
        xf, xb = _combine(ys, e8, r8, w8.T, offsets, x1, x1b, shared_w_gate[l].astype(BF16),
                          shared_w_up[l].astype(BF16), shared_w_down[l].astype(BF16),
                          ln_ffn_g[l], ln_ffn_b[l])
    return xf.reshape(b, s, d)
```

```python
import functools
import math

import numpy as np
import jax
import jax.numpy as jnp
from jax import lax
from jax.experimental import pallas as pl
from jax.experimental.pallas import tpu as pltpu

F32 = jnp.float32
BF16 = jnp.bfloat16
I32 = jnp.int32
U32 = jnp.uint32

D_MODEL = 2048
DEPTH = 2
ROPE_THETA = 500000.0
NORM_EPS = 1e-5

DIFF_HEADS = 4
DIFF_HALF_DIM = 64
DIFF_ROT = 16
MLA_HEADS = 8
MLA_Q_RANK = 512
MLA_KV_RANK = 256
MLA_NOPE = 128
MLA_ROPE = 64
MLA_V = 128
MOBA_HEADS = 4
MOBA_HEAD_DIM = 128
MOBA_BLOCK = 256
MOBA_TOPK = 3
MOBA_ROT = 32
A_WIDTH = 512
B_WIDTH = 1024
C_WIDTH = 512

N_EXPERTS = 64
N_GROUPS = 8
GROUP_SIZE = N_EXPERTS // N_GROUPS
TOPK_GROUPS = 4
TOPK = 8
EXPERT_FF = 512
ROUTED_SCALE = 2.5

DEEPNORM_ALPHA = (2 * DEPTH) ** 0.25

LANES = 128
NEG_BIG = -1e30

PROJ_TN = 512
MODE_PLAIN, MODE_ROPE_DIFF, MODE_ROPE_MLA, MODE_ROPE_MOBA, MODE_RMS, MODE_CKV, MODE_SIGMOID = range(7)
_ROPE_OF_MODE = {MODE_ROPE_DIFF: (0, DIFF_ROT // 2), MODE_ROPE_MLA: (1, MLA_ROPE // 2),
                 MODE_ROPE_MOBA: (2, MOBA_ROT // 2)}
_ROPE_PERIOD = (DIFF_HALF_DIM, MLA_ROPE, MOBA_HEAD_DIM)
_ROPE_ROT = (DIFF_ROT, MLA_ROPE, MOBA_ROT)

IN_TILE_MODES = ([MODE_ROPE_DIFF, MODE_ROPE_DIFF, MODE_PLAIN, MODE_RMS, MODE_CKV,
                  MODE_ROPE_MOBA, MODE_ROPE_MOBA, MODE_PLAIN] + [MODE_SIGMOID] * 12)
IN_TILE_SCALES = [DIFF_HALF_DIM ** -0.5] + [1.0] * 4 + [MOBA_HEAD_DIM ** -0.5] + [1.0] * 14
IN_COLS_PADDED = PROJ_TN * len(IN_TILE_MODES)

VMEM_LIMIT = 56 * 1024 * 1024

ATT_TQ = 256
EXPERT_TM = 256
ROW_TM = 256
ROUTER_TM = 1024


def _cparams(sem):
    return pltpu.CompilerParams(dimension_semantics=sem, vmem_limit_bytes=VMEM_LIMIT)


def _rope_table_kernel(pos_ref, c_ref, cos_ref, sin_ref):
    pos = pos_ref[...].astype(F32)
    for p in range(3):
        ang = pos * c_ref[p:p + 1, :]
        cos_ref[p] = jnp.cos(ang)
        sin_ref[p] = jnp.sin(ang) * c_ref[3 + p:4 + p, :]


def _rope_tables(positions):
    t = positions.size
    lane = jnp.arange(LANES)
    rows = []
    signs = []
    for period, rot in zip(_ROPE_PERIOD, _ROPE_ROT):
        half = rot // 2
        inv_freq = ROPE_THETA ** (-jnp.arange(0, rot, 2, dtype=F32) / rot)
        cp = lane % period
        active = cp < rot
        rows.append(jnp.where(active, inv_freq[cp % half], 0.0))
        signs.append(jnp.where(active, jnp.where(cp < half, -1.0, 1.0), 0.0))
    consts = jnp.stack(rows + signs + [jnp.zeros((LANES,), F32)] * 2).astype(F32)
    tm = min(1024, t)
    cos, sin = pl.pallas_call(
        _rope_table_kernel,
        grid=(t // tm,),
        in_specs=[pl.BlockSpec((tm, 1), lambda i: (i, 0)),
                  pl.BlockSpec((8, LANES), lambda i: (0, 0))],
        out_specs=[pl.BlockSpec((3, tm, LANES), lambda i: (0, i, 0)),
                   pl.BlockSpec((3, tm, LANES), lambda i: (0, i, 0))],
        out_shape=[jax.ShapeDtypeStruct((3, t, LANES), F32)] * 2,
        compiler_params=_cparams(("parallel",)),
        name="rope_tables",
    )(positions.reshape(t, 1), consts)
    return cos, sin


def _proj_kernel(mode_ref, scale_ref, x_ref, w_ref, cos_ref, sin_ref, g_ref, o_ref):
    j = pl.program_id(1)
    mode = mode_ref[j]
    y = jnp.dot(x_ref[...], w_ref[...], preferred_element_type=F32) * scale_ref[j]
    tm, tn = y.shape
    lane = lax.broadcasted_iota(I32, (tm, LANES), 1)

    def rope_chunk(yc, table, half):
        first = (lane % _ROPE_PERIOD[table]) < half
        swapped = jnp.where(first, pltpu.roll(yc, LANES - half, 1), pltpu.roll(yc, half, 1))
        return yc * cos_ref[table] + swapped * sin_ref[table]

    @pl.when(mode == MODE_PLAIN)
    def _():
        o_ref[...] = y.astype(o_ref.dtype)

    for rope_mode, (table, half) in _ROPE_OF_MODE.items():
        @pl.when(mode == rope_mode)
        def _(table=table, half=half):
            for c in range(tn // LANES):
                sl = slice(c * LANES, (c + 1) * LANES)
                o_ref[:, sl] = rope_chunk(y[:, sl], table, half).astype(o_ref.dtype)

    @pl.when(mode == MODE_RMS)
    def _():
        r = lax.rsqrt(jnp.mean(y * y, axis=-1, keepdims=True) + NORM_EPS)
        o_ref[...] = (y * r * g_ref[0:1, :]).astype(o_ref.dtype)

    @pl.when(mode == MODE_CKV)
    def _():
        ckv = y[:, :MLA_KV_RANK]
        r = lax.rsqrt(jnp.mean(ckv * ckv, axis=-1, keepdims=True) + NORM_EPS)
        o_ref[:, :MLA_KV_RANK] = (ckv * r * g_ref[1:2, :MLA_KV_RANK]).astype(o_ref.dtype)
        sl = slice(MLA_KV_RANK, MLA_KV_RANK + LANES)
        table, half = _ROPE_OF_MODE[MODE_ROPE_MLA]
        o_ref[:, sl] = rope_chunk(y[:, sl], table, half).astype(o_ref.dtype)
        o_ref[:, MLA_KV_RANK + LANES:] = jnp.zeros((tm, tn - MLA_KV_RANK - LANES), o_ref.dtype)

    @pl.when(mode == MODE_SIGMOID)
    def _():
        o_ref[...] = (1.0 / (1.0 + jnp.exp(-y))).astype(o_ref.dtype)


def _proj(x, x_col_block, k_dim, w, modes, scales, cos, sin, gains, tm):
    t = x.shape[0]
    n = w.shape[1]
    assert n % PROJ_TN == 0 and len(modes) == len(scales) == n // PROJ_TN and w.shape[0] == k_dim
    grid_spec = pltpu.PrefetchScalarGridSpec(
        num_scalar_prefetch=1,
        grid=(t // tm, n // PROJ_TN),
        in_specs=[pl.BlockSpec(memory_space=pltpu.SMEM),
                  pl.BlockSpec((tm, k_dim), lambda i, j, m: (i, x_col_block)),
                  pl.BlockSpec((k_dim, PROJ_TN), lambda i, j, m: (0, j)),
                  pl.BlockSpec((3, tm, LANES), lambda i, j, m: (0, i, 0)),
                  pl.BlockSpec((3, tm, LANES), lambda i, j, m: (0, i, 0)),
                  pl.BlockSpec((8, PROJ_TN), lambda i, j, m: (0, 0))],
        out_specs=pl.BlockSpec((tm, PROJ_TN), lambda i, j, m: (i, j)),
    )
    return pl.pallas_call(
        _proj_kernel,
        grid_spec=grid_spec,
        out_shape=jax.ShapeDtypeStruct((t, n), BF16),
        compiler_params=_cparams(("parallel", "arbitrary")),
        name="proj",
    )(jnp.asarray(modes, I32), jnp.asarray(scales, F32), x, w, cos, sin, gains)


_NT_DIMS = (((1,), (1,)), ((), ()))


def _causal_mask(tq):
    row = lax.broadcasted_iota(I32, (tq, tq), 0)
    col = lax.broadcasted_iota(I32, (tq, tq), 1)
    return col <= row


def _score_strip(s_ref, i, tq, block_scores, block_mask):
    causal = _causal_mask(tq)
    for j in range(i + 1):
        sc = block_scores(j)
        if j == i:
            sc = jnp.where(causal, sc, NEG_BIG)
        elif block_mask is not None:
            sc = jnp.where(block_mask(j), sc, NEG_BIG)
        s_ref[:, j * tq:(j + 1) * tq] = sc


def _softmax_times_v(s_ref, p_ref, v_ref, n):
    tq = s_ref.shape[0]
    nch = n // LANES
    mrun = s_ref[:, 0:LANES]
    for c in range(1, nch):
        mrun = jnp.maximum(mrun, s_ref[:, c * LANES:(c + 1) * LANES])
    m = jnp.broadcast_to(jnp.max(mrun, axis=-1, keepdims=True), (tq, LANES))
    lrun = jnp.zeros((tq, LANES), F32)
    for c in range(nch):
        sl = slice(c * LANES, (c + 1) * LANES)
        p = jnp.exp(s_ref[:, sl] - m)
        lrun = lrun + p
        p_ref[:, sl] = p.astype(BF16)
    l = jnp.sum(lrun, axis=-1, keepdims=True)
    o = jnp.dot(p_ref[:, :n], v_ref[0:n, :], preferred_element_type=F32)
    return o / l


def _diff_attn_kernel(lam_ref, g_ref, q_ref, k_ref, v_ref, o_ref, s1_ref, p1_ref, s2_ref, p2_ref,
                      *, tq, lambda_init):
    nq = q_ref.shape[0] // tq
    lp = lam_ref[...]
    lam = (jnp.exp(jnp.sum(lp[0:1] * lp[1:2], axis=-1, keepdims=True))
           - jnp.exp(jnp.sum(lp[2:3] * lp[3:4], axis=-1, keepdims=True)) + lambda_init)
    lane = lax.broadcasted_iota(I32, (tq, LANES), 1)
    for i in range(nq):
        q = q_ref[i * tq:(i + 1) * tq, :]
        q1 = jnp.where(lane < DIFF_HALF_DIM, q, jnp.zeros_like(q))
        q2 = jnp.where(lane >= DIFF_HALF_DIM, q, jnp.zeros_like(q))
        kblk = lambda j: k_ref[j * tq:(j + 1) * tq, :]
        _score_strip(s1_ref, i, tq,
                     lambda j: lax.dot_general(q1, kblk(j), _NT_DIMS, preferred_element_type=F32), None)
        _score_strip(s2_ref, i, tq,
                     lambda j: lax.dot_general(q2, kblk(j), _NT_DIMS, preferred_element_type=F32), None)
        n = (i + 1) * tq
        o = _softmax_times_v(s1_ref, p1_ref, v_ref, n) - lam * _softmax_times_v(s2_ref, p2_ref, v_ref, n)
        r = lax.rsqrt(jnp.mean(o * o, axis=-1, keepdims=True) + NORM_EPS)
        o_ref[i * tq:(i + 1) * tq, :] = (o * r * g_ref[...] * (1.0 - lambda_init)).astype(o_ref.dtype)


def _diff_attention(proj, lam_parts, subln_g, b, s, lambda_init):
    t = b * s
    tq = min(ATT_TQ, s)
    kern = functools.partial(_diff_attn_kernel, tq=tq, lambda_init=lambda_init)
    return pl.pallas_call(
        kern,
        grid=(b, DIFF_HEADS),
        in_specs=[pl.BlockSpec((4, DIFF_HALF_DIM), lambda bi, h: (0, 0)),
                  pl.BlockSpec((1, LANES), lambda bi, h: (0, 0)),
                  pl.BlockSpec((s, LANES), lambda bi, h: (bi, h)),
                  pl.BlockSpec((s, LANES), lambda bi, h: (bi, DIFF_HEADS + h)),
                  pl.BlockSpec((s, LANES), lambda bi, h: (bi, 2 * DIFF_HEADS + h))],
        out_specs=pl.BlockSpec((s, LANES), lambda bi, h: (bi, h)),
        out_shape=jax.ShapeDtypeStruct((t, A_WIDTH), BF16),
        scratch_shapes=[pltpu.VMEM((tq, s), F32), pltpu.VMEM((tq, s), BF16),
                        pltpu.VMEM((tq, s), F32), pltpu.VMEM((tq, s), BF16)],
        compiler_params=_cparams(("parallel", "parallel")),
        name="diff_attention",
    )(lam_parts, subln_g.reshape(1, LANES), proj, proj, proj)


def _mla_attn_kernel(qn_ref, qr_ref, kn_ref, kpe_ref, v_ref, o_ref, s_ref, p_ref, *, tq):
    h = pl.program_id(1)
    nq = qn_ref.shape[0] // tq
    lane = lax.broadcasted_iota(I32, (tq, LANES), 1)
    mine = (lane >= MLA_ROPE) == (h % 2 == 1)
    for i in range(nq):
        qn = qn_ref[i * tq:(i + 1) * tq, :]
        qr = qr_ref[i * tq:(i + 1) * tq, :]
        qr = jnp.where(mine, qr, jnp.zeros_like(qr))

        def block_scores(j, qn=qn, qr=qr):
            rows = slice(j * tq, (j + 1) * tq)
            return (lax.dot_general(qn, kn_ref[rows, :], _NT_DIMS, preferred_element_type=F32)
                    + lax.dot_general(qr, kpe_ref[rows, :], _NT_DIMS, preferred_element_type=F32))

        _score_strip(s_ref, i, tq, block_scores, None)
        o = _softmax_times_v(s_ref, p_ref, v_ref, (i + 1) * tq)
        o_ref[i * tq:(i + 1) * tq, :] = o.astype(o_ref.dtype)


def _mla_attention(qbuf, kvbuf, proj, b, s):
    t = b * s
    tq = min(ATT_TQ, s)
    kpe_block = (4 * PROJ_TN + MLA_KV_RANK) // LANES
    return pl.pallas_call(
        functools.partial(_mla_attn_kernel, tq=tq),
        grid=(b, MLA_HEADS),
        in_specs=[pl.BlockSpec((s, LANES), lambda bi, h: (bi, h)),
                  pl.BlockSpec((s, LANES), lambda bi, h: (bi, MLA_HEADS + h // 2)),
                  pl.BlockSpec((s, LANES), lambda bi, h: (bi, h)),
                  pl.BlockSpec((s, LANES), lambda bi, h: (bi, kpe_block)),
                  pl.BlockSpec((s, LANES), lambda bi, h: (bi, MLA_HEADS + h))],
        out_specs=pl.BlockSpec((s, LANES), lambda bi, h: (bi, h)),
        out_shape=jax.ShapeDtypeStruct((t, B_WIDTH), BF16),
        scratch_shapes=[pltpu.VMEM((tq, s), F32), pltpu.VMEM((tq, s), BF16)],
        compiler_params=_cparams(("parallel", "parallel")),
        name="mla_attention",
    )(qbuf, qbuf, kvbuf, proj, kvbuf)


def _moba_attn_kernel(q_ref, k_ref, v_ref, o_ref, s_ref, p_ref, kmean_ref):
    tq = MOBA_BLOCK
    nb = q_ref.shape[0] // tq
    kmean_ref[...] = jnp.zeros(kmean_ref.shape, F32)
    for j in range(nb):
        kb = k_ref[j * tq:(j + 1) * tq, :].astype(F32)
        kmean_ref[j:j + 1, :] = jnp.sum(kb, axis=0, keepdims=True) * (1.0 / tq)
    km = kmean_ref[...]
    km_hi = km.astype(BF16)
    km_lo = (km - km_hi.astype(F32)).astype(BF16)
    lane = lax.broadcasted_iota(I32, (tq, LANES), 1)
    for i in range(nb):
        q = q_ref[i * tq:(i + 1) * tq, :]
        gate = (lax.dot_general(q, km_hi, _NT_DIMS, preferred_element_type=F32)
                + lax.dot_general(q, km_lo, _NT_DIMS, preferred_element_type=F32))
        fully_past = lane < i
        g = jnp.where(fully_past, gate, -jnp.inf)
        cnt = jnp.zeros(gate.shape, F32)
        for jp in range(i):
            col = g[:, jp:jp + 1]
            ahead = (col > g) | ((col == g) & (jp < lane))
            cnt = cnt + jnp.where(ahead, 1.0, 0.0)
        kept = (cnt < MOBA_TOPK) & fully_past
        _score_strip(s_ref, i, tq,
                     lambda j, q=q: lax.dot_general(q, k_ref[j * tq:(j + 1) * tq, :], _NT_DIMS,
                                                    preferred_element_type=F32),
                     lambda j, kept=kept: kept[:, j:j + 1])
        o = _softmax_times_v(s_ref, p_ref, v_ref, (i + 1) * tq)
        o_ref[i * tq:(i + 1) * tq, :] = o.astype(o_ref.dtype)


def _moba_attention(proj, b, s):
    assert s % MOBA_BLOCK == 0 and s // MOBA_BLOCK <= LANES
    t = b * s
    base = 5 * PROJ_TN // LANES
    return pl.pallas_call(
        _moba_attn_kernel,
        grid=(b, MOBA_HEADS),
        in_specs=[pl.BlockSpec((s, LANES), lambda bi, h: (bi, base + h)),
                  pl.BlockSpec((s, LANES), lambda bi, h: (bi, base + MOBA_HEADS + h)),
                  pl.BlockSpec((s, LANES), lambda bi, h: (bi, base + 2 * MOBA_HEADS + h))],
        out_specs=pl.BlockSpec((s, LANES), lambda bi, h: (bi, h)),
        out_shape=jax.ShapeDtypeStruct((t, C_WIDTH), BF16),
        scratch_shapes=[pltpu.VMEM((MOBA_BLOCK, s), F32), pltpu.VMEM((MOBA_BLOCK, s), BF16),
                        pltpu.VMEM((LANES, LANES), F32)],
        compiler_params=_cparams(("parallel", "parallel")),
        name="moba_attention",
    )(proj, proj, proj)


def _layer_norm_rows(z, g, b):
    mu = jnp.mean(z, axis=-1, keepdims=True)
    zc = z - mu
    var = jnp.mean(zc * zc, axis=-1, keepdims=True)
    return zc * lax.rsqrt(var + NORM_EPS) * g + b


def _pack_halves(z):
    n = z.shape[1] // 2
    lo = lax.bitcast_convert_type(z[:, :n].astype(BF16).astype(F32), U32)
    hi = lax.bitcast_convert_type(z[:, n:].astype(BF16).astype(F32), U32)
    return (lo >> 16) | hi


def _unpack_halves(u):
    lo = lax.bitcast_convert_type(u << 16, F32)
    hi = lax.bitcast_convert_type(u & jnp.uint32(0xFFFF0000), F32)
    return lo, hi


def _merge_kernel(oa_ref, ob_ref, oc_ref, g0_ref, g1_ref, g2_ref, x_ref, wb_ref, wo_ref,
                  lng_ref, lnb_ref, xo_ref, xb_ref, xp_ref):
    ya = jnp.dot(oa_ref[...], wb_ref[:A_WIDTH, :], preferred_element_type=F32)
    yb = jnp.dot(ob_ref[...], wb_ref[A_WIDTH:A_WIDTH + B_WIDTH, :], preferred_element_type=F32)
    yc = jnp.dot(oc_ref[...], wb_ref[A_WIDTH + B_WIDTH:, :], preferred_element_type=F32)
    y = g0_ref[...].astype(F32) * ya + g1_ref[...].astype(F32) * yb + g2_ref[...].astype(F32) * yc
    mix = jnp.dot(y.astype(BF16), wo_ref[...], preferred_element_type=F32)
    z = _layer_norm_rows(DEEPNORM_ALPHA * x_ref[...] + mix, lng_ref[...], lnb_ref[...])
    xo_ref[...] = z
    xb_ref[...] = z.astype(BF16)
    xp_ref[...] = _pack_halves(z)


def _merge(oa, ob, oc, proj, x, w_branch, w_out, ln_g, ln_b):
    t = x.shape[0]
    tm = min(256, t)
    gate_base = 8 * PROJ_TN // D_MODEL
    row = lambda i: (i, 0)
    whole = lambda i: (0, 0)
    return pl.pallas_call(
        _merge_kernel,
        grid=(t // tm,),
        in_specs=[pl.BlockSpec((tm, A_WIDTH), row),
                  pl.BlockSpec((tm, B_WIDTH), row),
                  pl.BlockSpec((tm, C_WIDTH), row),
                  pl.BlockSpec((tm, D_MODEL), lambda i: (i, gate_base)),
                  pl.BlockSpec((tm, D_MODEL), lambda i: (i, gate_base + 1)),
                  pl.BlockSpec((tm, D_MODEL), lambda i: (i, gate_base + 2)),
                  pl.BlockSpec((tm, D_MODEL), row),
                  pl.BlockSpec((D_MODEL, D_MODEL), whole),
                  pl.BlockSpec((D_MODEL, D_MODEL), whole),
                  pl.BlockSpec((1, D_MODEL), whole),
                  pl.BlockSpec((1, D_MODEL), whole)],
        out_specs=[pl.BlockSpec((tm, D_MODEL), row),
                   pl.BlockSpec((tm, D_MODEL), row),
                   pl.BlockSpec((tm, D_MODEL // 2), row)],
        out_shape=[jax.ShapeDtypeStruct((t, D_MODEL), F32),
                   jax.ShapeDtypeStruct((t, D_MODEL), BF16),
                   jax.ShapeDtypeStruct((t, D_MODEL // 2), U32)],
        compiler_params=_cparams(("parallel",)),
        name="merge_outproj_ln",
    )(oa, ob, oc, proj, proj, proj, x, w_branch, w_out, ln_g.reshape(1, -1), ln_b.reshape(1, -1))


def _split_bf16(a):
    hi = a.astype(BF16)
    lo = (a - hi.astype(F32)).astype(BF16)
    return hi, lo


def _router_kernel(x_ref, wt_ref, bias_ref, upper_ref, lower_ref,
                   e8_ref, r8_ref, w8_ref, cnt_ref, carry_ref):
    i = pl.program_id(0)

    @pl.when(i == 0)
    def _():
        carry_ref[...] = jnp.zeros(carry_ref.shape, F32)

    xh, xl = _split_bf16(x_ref[...])
    wh, wl = _split_bf16(wt_ref[...])
    logits = (lax.dot_general(wh, xh, _NT_DIMS, preferred_element_type=F32)
              + lax.dot_general(wh, xl, _NT_DIMS, preferred_element_type=F32)
              + lax.dot_general(wl, xh, _NT_DIMS, preferred_element_type=F32))
    scores = 1.0 / (1.0 + jnp.exp(-logits))
    choice = scores + bias_ref[:, 0:1]
    tm = choice.shape[1]
    sub = lax.broadcasted_iota(I32, (GROUP_SIZE, tm), 0)

    group_rows = []
    for g in range(N_GROUPS):
        cg = choice[g * GROUP_SIZE:(g + 1) * GROUP_SIZE, :]
        m1 = jnp.max(cg, axis=0, keepdims=True)
        first = jnp.min(jnp.where(cg == m1, sub, GROUP_SIZE), axis=0, keepdims=True)
        m2 = jnp.max(jnp.where(sub == first, -jnp.inf, cg), axis=0, keepdims=True)
        group_rows.append(m1 + m2)
    gs = jnp.concatenate(group_rows, axis=0)
    ahead = jnp.zeros(gs.shape, F32)
    for gp in range(N_GROUPS):
        rowv = gs[gp:gp + 1, :]
        ahead = ahead + jnp.where((rowv > gs) | ((rowv == gs) & (gp < sub)), 1.0, 0.0)
    keep_group = ahead < TOPK_GROUPS
    masked = jnp.concatenate(
        [jnp.where(keep_group[g:g + 1, :], choice[g * GROUP_SIZE:(g + 1) * GROUP_SIZE, :], -jnp.inf)
         for g in range(N_GROUPS)], axis=0)

    eidx = lax.broadcasted_iota(I32, masked.shape, 0)
    ahead = jnp.zeros(masked.shape, F32)
    for ep in range(N_EXPERTS):
        rowv = masked[ep:ep + 1, :]
        ahead = ahead + jnp.where((rowv > masked) | ((rowv == masked) & (ep < eidx)), 1.0, 0.0)
    sel = ahead < TOPK
    picked = jnp.where(sel, scores, 0.0)
    gates = picked / jnp.sum(picked, axis=0, keepdims=True) * ROUTED_SCALE
    self32 = jnp.where(sel, 1.0, 0.0)
    selb = self32.astype(BF16)

    rank = jnp.dot(selb, upper_ref[...], preferred_element_type=F32) + carry_ref[:, 0:1]
    slot = jnp.dot(lower_ref[...], selb, preferred_element_type=F32)
    carry_ref[...] = carry_ref[...] + jnp.sum(self32, axis=1, keepdims=True)
    cnt_ref[...] = carry_ref[...].astype(I32)

    eidf = eidx.astype(F32)
    e_rows, r_rows, w_rows = [], [], []
    for k in range(TOPK):
        hit = sel & (slot == float(k))
        e_rows.append(jnp.sum(jnp.where(hit, eidf, 0.0), axis=0, keepdims=True))
        r_rows.append(jnp.sum(jnp.where(hit, rank, 0.0), axis=0, keepdims=True))
        w_rows.append(jnp.sum(jnp.where(hit, gates, 0.0), axis=0, keepdims=True))
    e8_ref[...] = jnp.concatenate(e_rows, axis=0).astype(I32)
    r8_ref[...] = jnp.concatenate(r_rows, axis=0).astype(I32)
    w8_ref[...] = jnp.concatenate(w_rows, axis=0)


def _router(x, router_w, router_bias):
    t = x.shape[0]
    tm = min(ROUTER_TM, t)
    upper = (jnp.arange(tm)[:, None] < jnp.arange(tm)[None, :]).astype(BF16)
    lower = (jnp.arange(N_EXPERTS)[None, :] < jnp.arange(N_EXPERTS)[:, None]).astype(BF16)
    bias = jnp.broadcast_to(router_bias.astype(F32)[:, None], (N_EXPERTS, LANES))
    whole = lambda i: (0, 0)
    col = lambda i: (0, i)
    return pl.pallas_call(
        _router_kernel,
        grid=(t // tm,),
        in_specs=[pl.BlockSpec((tm, D_MODEL), lambda i: (i, 0)),
                  pl.BlockSpec((N_EXPERTS, D_MODEL), whole),
                  pl.BlockSpec((N_EXPERTS, LANES), whole),
                  pl.BlockSpec((tm, tm), whole),
                  pl.BlockSpec((N_EXPERTS, N_EXPERTS), whole)],
        out_specs=[pl.BlockSpec((TOPK, tm), col),
                   pl.BlockSpec((TOPK, tm), col),
                   pl.BlockSpec((TOPK, tm), col),
                   pl.BlockSpec((N_EXPERTS, LANES), whole)],
        out_shape=[jax.ShapeDtypeStruct((TOPK, t), I32),
                   jax.ShapeDtypeStruct((TOPK, t), I32),
                   jax.ShapeDtypeStruct((TOPK, t), F32),
                   jax.ShapeDtypeStruct((N_EXPERTS, LANES), I32)],
        scratch_shapes=[pltpu.VMEM((N_EXPERTS, LANES), F32)],
        compiler_params=_cparams(("arbitrary",)),
        name="router",
    )(x, router_w.T, bias, upper, lower)


def _dispatch_kernel(off_ref, e8_ref, r8_ref, x_ref, xs_ref, sem):
    tm = x_ref.shape[0]

    def row_copy(t, k):
        pos = off_ref[e8_ref[k, t]] + r8_ref[k, t]
        return pltpu.make_async_copy(x_ref.at[pl.ds(t, 1)], xs_ref.at[pl.ds(pos, 1)], sem)

    def start(t, c):
        for k in range(TOPK):
            row_copy(t, k).start(priority=k % 2)
        return c
    lax.fori_loop(0, tm, start, 0)

    def wait(t, c):
        for k in range(TOPK):
            row_copy(t, k).wait()
        return c
    lax.fori_loop(0, tm, wait, 0)


def _dispatch(xp, e8, r8, offsets, n_rows):
    t, half = xp.shape
    tm = min(ROW_TM, t)
    grid_spec = pltpu.PrefetchScalarGridSpec(
        num_scalar_prefetch=1,
        grid=(t // tm,),
        in_specs=[pl.BlockSpec((TOPK, tm), lambda i, o: (0, i), memory_space=pltpu.SMEM),
                  pl.BlockSpec((TOPK, tm), lambda i, o: (0, i), memory_space=pltpu.SMEM),
                  pl.BlockSpec((tm, half), lambda i, o: (i, 0))],
        out_specs=pl.BlockSpec(memory_space=pl.ANY),
        scratch_shapes=[pltpu.SemaphoreType.DMA(())],
    )
    return pl.pallas_call(
        _dispatch_kernel,
        grid_spec=grid_spec,
        out_shape=jax.ShapeDtypeStruct((n_rows, half), U32),
        compiler_params=_cparams(("arbitrary",)),
        name="dispatch",
    )(offsets, e8, r8, xp)


def _expert_kernel(te_ref, rows_ref, nv_ref, xs_ref, wg_ref, wu_ref, wd_ref, ys_ref,
                   wgb_ref, wub_ref, wdb_ref):
    i = pl.program_id(0)

    @pl.when(i < nv_ref[0])
    def _():
        @pl.when((i == 0) | (te_ref[i] != te_ref[jnp.maximum(i - 1, 0)]))
        def _():
            wgb_ref[...] = wg_ref[0].astype(BF16)
            wub_ref[...] = wu_ref[0].astype(BF16)
            wdb_ref[...] = wd_ref[0].astype(BF16)

        half = D_MODEL // 2
        u = xs_ref[...]
        live = lax.broadcasted_iota(I32, (u.shape[0], 1), 0) < rows_ref[i]
        lo, hi = _unpack_halves(jnp.where(live, u, jnp.zeros_like(u)))
        xl = lo.astype(BF16)
        xh = hi.astype(BF16)
        gate = (jnp.dot(xl, wgb_ref[:half, :], preferred_element_type=F32)
                + jnp.dot(xh, wgb_ref[half:, :], preferred_element_type=F32))
        up = (jnp.dot(xl, wub_ref[:half, :], preferred_element_type=F32)
              + jnp.dot(xh, wub_ref[half:, :], preferred_element_type=F32))
        hid = (gate / (1.0 + jnp.exp(-gate))) * up
        y = jnp.dot(hid.astype(BF16), wdb_ref[...], preferred_element_type=F32)
        ys_ref[...] = _pack_halves(y)


def _experts(xs, tile_expert, tile_rows, n_valid, w_gate, w_up, w_down, layer):
    n_rows, half = xs.shape
    n_tiles = n_rows // EXPERT_TM
    row = lambda i, te, tr, nv: (jnp.minimum(i, nv[0] - 1), 0)
    wsel = lambda i, te, tr, nv: (layer * N_EXPERTS + te[i], 0, 0)
    grid_spec = pltpu.PrefetchScalarGridSpec(
        num_scalar_prefetch=3,
        grid=(n_tiles,),
        in_specs=[pl.BlockSpec((EXPERT_TM, half), row),
                  pl.BlockSpec((1, D_MODEL, EXPERT_FF), wsel),
                  pl.BlockSpec((1, D_MODEL, EXPERT_FF), wsel),
                  pl.BlockSpec((1, EXPERT_FF, D_MODEL), wsel)],
        out_specs=pl.BlockSpec((EXPERT_TM, half), row),
        scratch_shapes=[pltpu.VMEM((D_MODEL, EXPERT_FF), BF16), pltpu.VMEM((D_MODEL, EXPERT_FF), BF16),
                        pltpu.VMEM((EXPERT_FF, D_MODEL), BF16)],
    )
    return pl.pallas_call(
        _expert_kernel,
        grid_spec=grid_spec,
        out_shape=jax.ShapeDtypeStruct((n_rows, half), U32),
        compiler_params=_cparams(("arbitrary",)),
        name="experts",
    )(tile_expert, tile_rows, n_valid, xs, w_gate, w_up, w_down)


def _combine_kernel(off_ref, e8_ref, r8_ref, w8_ref, ys_ref, x_ref, xb_ref, wsg_ref, wsu_ref, wsd_ref,
                    lng_ref, lnb_ref, xo_ref, xob_ref, buf_ref, sem):
    tm = x_ref.shape[0]

    def row_copy(t, k):
        pos = off_ref[e8_ref[k, t]] + r8_ref[k, t]
        return pltpu.make_async_copy(ys_ref.at[pl.ds(pos, 1)], buf_ref.at[k, pl.ds(t, 1)], sem)

    def start(t, c):
        for k in range(TOPK):
            row_copy(t, k).start(priority=k % 2)
        return c
    lax.fori_loop(0, tm, start, 0)

    xb = xb_ref[...]
    sg = jnp.dot(xb, wsg_ref[...], preferred_element_type=F32)
    su = jnp.dot(xb, wsu_ref[...], preferred_element_type=F32)
    hid = (sg / (1.0 + jnp.exp(-sg))) * su
    shared = jnp.dot(hid.astype(BF16), wsd_ref[...], preferred_element_type=F32)

    def wait(t, c):
        for k in range(TOPK):
            row_copy(t, k).wait()
        return c
    lax.fori_loop(0, tm, wait, 0)

    half = D_MODEL // 2
    acc_lo = jnp.zeros((tm, half), F32)
    acc_hi = jnp.zeros((tm, half), F32)
    w8 = w8_ref[...]
    for k in range(TOPK):
        lo, hi = _unpack_halves(buf_ref[k])
        wk = w8[:, k:k + 1]
        acc_lo = acc_lo + wk * lo
        acc_hi = acc_hi + wk * hi
    routed = jnp.concatenate([acc_lo, acc_hi], axis=1)
    z = _layer_norm_rows(DEEPNORM_ALPHA * x_ref[...] + (routed + shared), lng_ref[...], lnb_ref[...])
    xo_ref[...] = z
    xob_ref[...] = z.astype(BF16)


def _combine(ys, e8, r8, w8t, offsets, x, xb, ws_gate, ws_up, ws_down, ln_g, ln_b):
    t = x.shape[0]
    half = D_MODEL // 2
    tm = min(ROW_TM, t)
    row = lambda i, o: (i, 0)
    whole = lambda i, o: (0, 0)
    grid_spec = pltpu.PrefetchScalarGridSpec(
        num_scalar_prefetch=1,
        grid=(t // tm,),
        in_specs=[pl.BlockSpec((TOPK, tm), lambda i, o: (0, i), memory_space=pltpu.SMEM),
                  pl.BlockSpec((TOPK, tm), lambda i, o: (0, i), memory_space=pltpu.SMEM),
                  pl.BlockSpec((tm, TOPK), row),
                  pl.BlockSpec(memory_space=pl.ANY),
                  pl.BlockSpec((tm, D_MODEL), row),
                  pl.BlockSpec((tm, D_MODEL), row),
                  pl.BlockSpec((D_MODEL, EXPERT_FF), whole),
                  pl.BlockSpec((D_MODEL, EXPERT_FF), whole),
                  pl.BlockSpec((EXPERT_FF, D_MODEL), whole),
                  pl.BlockSpec((1, D_MODEL), whole),
                  pl.BlockSpec((1, D_MODEL), whole)],
        out_specs=[pl.BlockSpec((tm, D_MODEL), row),
                   pl.BlockSpec((tm, D_MODEL), row)],
        scratch_shapes=[pltpu.VMEM((TOPK, tm, half), U32), pltpu.SemaphoreType.DMA(())],
    )
    return pl.pallas_call(
        _combine_kernel,
        grid_spec=grid_spec,
        out_shape=[jax.ShapeDtypeStruct((t, D_MODEL), F32),
                   jax.ShapeDtypeStruct((t, D_MODEL), BF16)],
        compiler_params=_cparams(("arbitrary",)),
        name="combine_shared_ln",
    )(offsets, e8, r8, w8t, ys, x, xb, ws_gate, ws_up, ws_down, ln_g.reshape(1, -1), ln_b.reshape(1, -1))


def _in_proj_weight(w_in):
    sizes = (512, 512, 512, MLA_Q_RANK, MLA_KV_RANK, MLA_ROPE, 512, 512, 512, 3 * D_MODEL)
    offs = np.concatenate([[0], np.cumsum(sizes)])
    seg = [w_in[:, offs[n]:offs[n + 1]] for n in range(len(sizes))]
    dq, dk, dv, cq, ckv, kpe, mq, mk, mv, gates = seg
    pad = jnp.zeros((w_in.shape[0], PROJ_TN - MLA_KV_RANK - 2 * MLA_ROPE), w_in.dtype)
    return jnp.concatenate([dq, dk, dv, cq, ckv, kpe, kpe, pad, mq, mk, mv, gates], axis=1).astype(BF16)


def _uq_weight(w_uq):
    w = w_uq.reshape(MLA_Q_RANK, MLA_HEADS, MLA_NOPE + MLA_ROPE)
    return jnp.concatenate([w[:, :, :MLA_NOPE].reshape(MLA_Q_RANK, -1),
                            w[:, :, MLA_NOPE:].reshape(MLA_Q_RANK, -1)], axis=1).astype(BF16)


def _ukv_weight(w_ukv):
    w = w_ukv.reshape(MLA_KV_RANK, MLA_HEADS, MLA_NOPE + MLA_V)
    return jnp.concatenate([w[:, :, :MLA_NOPE].reshape(MLA_KV_RANK, -1),
                            w[:, :, MLA_NOPE:].reshape(MLA_KV_RANK, -1)], axis=1).astype(BF16)


def _expert_tiling(counts, n_tiles):
    tiles = (counts + EXPERT_TM - 1) // EXPERT_TM
    ends = jnp.cumsum(tiles)
    offsets = ((ends - tiles) * EXPERT_TM).astype(I32)
    tile_ids = jnp.arange(n_tiles, dtype=I32)
    tile_expert = jnp.minimum(
        jnp.sum((ends[None, :] <= tile_ids[:, None]).astype(I32), axis=1), N_EXPERTS - 1).astype(I32)
    first_tile = (ends - tiles).astype(I32)
    tile_rows = jnp.clip(counts[tile_expert] - (tile_ids - first_tile[tile_expert]) * EXPERT_TM,
                         0, EXPERT_TM).astype(I32)
    return offsets, tile_expert, tile_rows, ends[-1:].astype(I32)


def kernel(x, positions, w_in, diff_lambda_q1, diff_lambda_k1, diff_lambda_q2, diff_lambda_k2, diff_subln_g, mla_q_norm_g, mla_w_uq, mla_kv_norm_g, mla_w_ukv, w_branch, w_out, ln_mix_g, ln_mix_b, router_w, router_bias, expert_w_gate, expert_w_up, expert_w_down, shared_w_gate, shared_w_up, shared_w_down, ln_ffn_g, ln_ffn_b):
    b, s, d = x.shape
    t = b * s
    depth = w_in.shape[0]
    cos, sin = _rope_tables(positions)
    xf = x.reshape(t, d)
    xb = xf.astype(BF16)
    n_tiles = t * TOPK // EXPERT_TM + N_EXPERTS
    proj_tm = min(1024, t)

    for l in range(depth):
        gains = jnp.zeros((8, PROJ_TN), F32)
        gains = gains.at[0, :].set(mla_q_norm_g[l]).at[1, :MLA_KV_RANK].set(mla_kv_norm_g[l])
        proj = _proj(xb, 0, d, _in_proj_weight(w_in[l]), IN_TILE_MODES, IN_TILE_SCALES,
                     cos, sin, gains, proj_tm)
        qbuf = _proj(proj, 3, MLA_Q_RANK, _uq_weight(mla_w_uq[l]),
                     [MODE_PLAIN, MODE_PLAIN, MODE_ROPE_MLA], [(MLA_NOPE + MLA_ROPE) ** -0.5] * 3,
                     cos, sin, gains, proj_tm)
        kvbuf = _proj(proj, 4 * PROJ_TN // MLA_KV_RANK, MLA_KV_RANK, _ukv_weight(mla_w_ukv[l]),
                      [MODE_PLAIN] * 4, [1.0] * 4, cos, sin, gains, proj_tm)

        lambda_init = 0.8 - 0.6 * math.exp(-0.3 * l)
        lam_parts = jnp.stack([diff_lambda_q1[l], diff_lambda_k1[l],
                               diff_lambda_q2[l], diff_lambda_k2[l]]).astype(F32)
        oa = _diff_attention(proj, lam_parts, diff_subln_g[l], b, s, lambda_init)
        ob = _mla_attention(qbuf, kvbuf, proj, b, s)
        oc = _moba_attention(proj, b, s)

        x1, x1b, x1p = _merge(oa, ob, oc, proj, xf, w_branch[l].astype(BF16), w_out[l].astype(BF16),
                              ln_mix_g[l], ln_mix_b[l])

        e8, r8, w8, counts = _router(x1, router_w[l], router_bias[l])
        offsets, tile_expert, tile_rows, n_valid = _expert_tiling(counts[:, 0], n_tiles)
        xs = _dispatch(x1p, e8, r8, offsets, n_tiles * EXPERT_TM)
        ys = _experts(xs, tile_expert, tile_rows, n_valid,
                      expert_w_gate.reshape(depth * N_EXPERTS, d, EXPERT_FF),
                      expert_w_up.reshape(depth * N_EXPERTS, d, EXPERT_FF),
                      expert_w_down.reshape(depth * N_EXPERTS, EXPERT_FF, d), l)
        xf, xb = _combine(ys, e8, r8, w8.T, offsets, x1, x1b, shared_w_gate[l].astype(BF16),
                          shared_w_up[l].astype(BF16), shared_w_down[l].astype(BF16),
                          ln_ffn_g[l], ln_ffn_b[l])
    return xf.reshape(b, s, d)
```

```python
import functools
import math

import numpy as np
import jax
import jax.numpy as jnp
from jax import lax
from jax.experimental import pallas as pl
from jax.experimental.pallas import tpu as pltpu

F32 = jnp.float32
BF16 = jnp.bfloat16
I32 = jnp.int32
U32 = jnp.uint32

D_MODEL = 2048
DEPTH = 2
ROPE_THETA = 500000.0
NORM_EPS = 1e-5

DIFF_HEADS = 4
DIFF_HALF_DIM = 64
DIFF_ROT = 16
MLA_HEADS = 8
MLA_Q_RANK = 512
MLA_KV_RANK = 256
MLA_NOPE = 128
MLA_ROPE = 64
MLA_V = 128
MOBA_HEADS = 4
MOBA_HEAD_DIM = 128
MOBA_BLOCK = 256
MOBA_TOPK = 3
MOBA_ROT = 32
A_WIDTH = 512
B_WIDTH = 1024
C_WIDTH = 512

N_EXPERTS = 64
N_GROUPS = 8
GROUP_SIZE = N_EXPERTS // N_GROUPS
TOPK_GROUPS = 4
TOPK = 8
EXPERT_FF = 512
ROUTED_SCALE = 2.5

DEEPNORM_ALPHA = (2 * DEPTH) ** 0.25

LANES = 128
NEG_BIG = -1e30

PROJ_TN = 512
MODE_PLAIN, MODE_ROPE_DIFF, MODE_ROPE_MLA, MODE_ROPE_MOBA, MODE_RMS, MODE_CKV, MODE_SIGMOID = range(7)
_ROPE_OF_MODE = {MODE_ROPE_DIFF: (0, DIFF_ROT // 2), MODE_ROPE_MLA: (1, MLA_ROPE // 2),
                 MODE_ROPE_MOBA: (2, MOBA_ROT // 2)}
_ROPE_PERIOD = (DIFF_HALF_DIM, MLA_ROPE, MOBA_HEAD_DIM)
_ROPE_ROT = (DIFF_ROT, MLA_ROPE, MOBA_ROT)

IN_TILE_MODES = ([MODE_ROPE_DIFF, MODE_ROPE_DIFF, MODE_PLAIN, MODE_RMS, MODE_CKV,
                  MODE_ROPE_MOBA, MODE_ROPE_MOBA, MODE_PLAIN] + [MODE_SIGMOID] * 12)
IN_TILE_SCALES = [DIFF_HALF_DIM ** -0.5] + [1.0] * 4 + [MOBA_HEAD_DIM ** -0.5] + [1.0] * 14
IN_COLS_PADDED = PROJ_TN * len(IN_TILE_MODES)

VMEM_LIMIT = 56 * 1024 * 1024

ATT_TQ = 256
EXPERT_TM = 256
ROW_TM = 256
ROUTER_TM = 1024


def _cparams(sem):
    return pltpu.CompilerParams(dimension_semantics=sem, vmem_limit_bytes=VMEM_LIMIT)


def _rope_table_kernel(pos_ref, c_ref, cos_ref, sin_ref):
    pos = pos_ref[...].astype(F32)
    for p in range(3):
        ang = pos * c_ref[p:p + 1, :]
        cos_ref[p] = jnp.cos(ang)
        sin_ref[p] = jnp.sin(ang) * c_ref[3 + p:4 + p, :]


def _rope_tables(positions):
    t = positions.size
    lane = jnp.arange(LANES)
    rows = []
    signs = []
    for period, rot in zip(_ROPE_PERIOD, _ROPE_ROT):
        half = rot // 2
        inv_freq = ROPE_THETA ** (-jnp.arange(0, rot, 2, dtype=F32) / rot)
        cp = lane % period
        active = cp < rot
        rows.append(jnp.where(active, inv_freq[cp % half], 0.0))
        signs.append(jnp.where(active, jnp.where(cp < half, -1.0, 1.0), 0.0))
    consts = jnp.stack(rows + signs + [jnp.zeros((LANES,), F32)] * 2).astype(F32)
    tm = min(1024, t)
    cos, sin = pl.pallas_call(
        _rope_table_kernel,
        grid=(t // tm,),
        in_specs=[pl.BlockSpec((tm, 1), lambda i: (i, 0)),
                  pl.BlockSpec((8, LANES), lambda i: (0, 0))],
        out_specs=[pl.BlockSpec((3, tm, LANES), lambda i: (0, i, 0)),
                   pl.BlockSpec((3, tm, LANES), lambda i: (0, i, 0))],
        out_shape=[jax.ShapeDtypeStruct((3, t, LANES), F32)] * 2,
        compiler_params=_cparams(("parallel",)),
        name="rope_tables",
    )(positions.reshape(t, 1), consts)
    return cos, sin


def _proj_kernel(mode_ref, scale_ref, x_ref, w_ref, cos_ref, sin_ref, g_ref, o_ref):
    j = pl.program_id(1)
    mode = mode_ref[j]
    y = jnp.dot(x_ref[...], w_ref[...], preferred_element_type=F32) * scale_ref[j]
    tm, tn = y.shape
    lane = lax.broadcasted_iota(I32, (tm, LANES), 1)

    def rope_chunk(yc, table, half):
        first = (lane % _ROPE_PERIOD[table]) < half
        swapped = jnp.where(first, pltpu.roll(yc, LANES - half, 1), pltpu.roll(yc, half, 1))
        return yc * cos_ref[table] + swapped * sin_ref[table]

    @pl.when(mode == MODE_PLAIN)
    def _():
        o_ref[...] = y.astype(o_ref.dtype)

    for rope_mode, (table, half) in _ROPE_OF_MODE.items():
        @pl.when(mode == rope_mode)
        def _(table=table, half=half):
            for c in range(tn // LANES):
                sl = slice(c * LANES, (c + 1) * LANES)
                o_ref[:, sl] = rope_chunk(y[:, sl], table, half).astype(o_ref.dtype)

    @pl.when(mode == MODE_RMS)
    def _():
        r = lax.rsqrt(jnp.mean(y * y, axis=-1, keepdims=True) + NORM_EPS)
        o_ref[...] = (y * r * g_ref[0:1, :]).astype(o_ref.dtype)

    @pl.when(mode == MODE_CKV)
    def _():
        ckv = y[:, :MLA_KV_RANK]
        r = lax.rsqrt(jnp.mean(ckv * ckv, axis=-1, keepdims=True) + NORM_EPS)
        o_ref[:, :MLA_KV_RANK] = (ckv * r * g_ref[1:2, :MLA_KV_RANK]).astype(o_ref.dtype)
        sl = slice(MLA_KV_RANK, MLA_KV_RANK + LANES)
        table, half = _ROPE_OF_MODE[MODE_ROPE_MLA]
        o_ref[:, sl] = rope_chunk(y[:, sl], table, half).astype(o_ref.dtype)
        o_ref[:, MLA_KV_RANK + LANES:] = jnp.zeros((tm, tn - MLA_KV_RANK - LANES), o_ref.dtype)

    @pl.when(mode == MODE_SIGMOID)
    def _():
        o_ref[...] = (1.0 / (1.0 + jnp.exp(-y))).astype(o_ref.dtype)


def _proj(x, x_col_block, k_dim, w, modes, scales, cos, sin, gains, tm):
    t = x.shape[0]
    n = w.shape[1]
    assert n % PROJ_TN == 0 and len(modes) == len(scales) == n // PROJ_TN and w.shape[0] == k_dim
    grid_spec = pltpu.PrefetchScalarGridSpec(
        num_scalar_prefetch=1,
        grid=(t // tm, n // PROJ_TN),
        in_specs=[pl.BlockSpec(memory_space=pltpu.SMEM),
                  pl.BlockSpec((tm, k_dim), lambda i, j, m: (i, x_col_block)),
                  pl.BlockSpec((k_dim, PROJ_TN), lambda i, j, m: (0, j)),
                  pl.BlockSpec((3, tm, LANES), lambda i, j, m: (0, i, 0)),
                  pl.BlockSpec((3, tm, LANES), lambda i, j, m: (0, i, 0)),
                  pl.BlockSpec((8, PROJ_TN), lambda i, j, m: (0, 0))],
        out_specs=pl.BlockSpec((tm, PROJ_TN), lambda i, j, m: (i, j)),
    )
    return pl.pallas_call(
        _proj_kernel,
        grid_spec=grid_spec,
        out_shape=jax.ShapeDtypeStruct((t, n), BF16),
        compiler_params=_cparams(("parallel", "arbitrary")),
        name="proj",
    )(jnp.asarray(modes, I32), jnp.asarray(scales, F32), x, w, cos, sin, gains)


_NT_DIMS = (((1,), (1,)), ((), ()))


def _causal_mask(tq):
    row = lax.broadcasted_iota(I32, (tq, tq), 0)
    col = lax.broadcasted_iota(I32, (tq, tq), 1)
    return col <= row


def _score_strip(s_ref, i, tq, block_scores, block_mask):
    causal = _causal_mask(tq)
    for j in range(i + 1):
        sc = block_scores(j)
        if j == i:
            sc = jnp.where(causal, sc, NEG_BIG)
        elif block_mask is not None:
            sc = jnp.where(block_mask(j), sc, NEG_BIG)
        s_ref[:, j * tq:(j + 1) * tq] = sc


def _softmax_times_v(s_ref, p_ref, v_ref, n):
    tq = s_ref.shape[0]
    nch = n // LANES
    mrun = s_ref[:, 0:LANES]
    for c in range(1, nch):
        mrun = jnp.maximum(mrun, s_ref[:, c * LANES:(c + 1) * LANES])
    m = jnp.broadcast_to(jnp.max(mrun, axis=-1, keepdims=True), (tq, LANES))
    lrun = jnp.zeros((tq, LANES), F32)
    for c in range(nch):
        sl = slice(c * LANES, (c + 1) * LANES)
        p = jnp.exp(s_ref[:, sl] - m)
        lrun = lrun + p
        p_ref[:, sl] = p.astype(BF16)
    l = jnp.sum(lrun, axis=-1, keepdims=True)
    o = jnp.dot(p_ref[:, :n], v_ref[0:n, :], preferred_element_type=F32)
    return o / l


def _diff_attn_kernel(lam_ref, g_ref, q_ref, k_ref, v_ref, o_ref, s1_ref, p1_ref, s2_ref, p2_ref,
                      *, tq, lambda_init):
    nq = q_ref.shape[0] // tq
    lp = lam_ref[...]
    lam = (jnp.exp(jnp.sum(lp[0:1] * lp[1:2], axis=-1, keepdims=True))
           - jnp.exp(jnp.sum(lp[2:3] * lp[3:4], axis=-1, keepdims=True)) + lambda_init)
    lane = lax.broadcasted_iota(I32, (tq, LANES), 1)
    for i in range(nq):
        q = q_ref[i * tq:(i + 1) * tq, :]
        q1 = jnp.where(lane < DIFF_HALF_DIM, q, jnp.zeros_like(q))
        q2 = jnp.where(lane >= DIFF_HALF_DIM, q, jnp.zeros_like(q))
        kblk = lambda j: k_ref[j * tq:(j + 1) * tq, :]
        _score_strip(s1_ref, i, tq,
                     lambda j: lax.dot_general(q1, kblk(j), _NT_DIMS, preferred_element_type=F32), None)
        _score_strip(s2_ref, i, tq,
                     lambda j: lax.dot_general(q2, kblk(j), _NT_DIMS, preferred_element_type=F32), None)
        n = (i + 1) * tq
        o = _softmax_times_v(s1_ref, p1_ref, v_ref, n) - lam * _softmax_times_v(s2_ref, p2_ref, v_ref, n)
        r = lax.rsqrt(jnp.mean(o * o, axis=-1, keepdims=True) + NORM_EPS)
        o_ref[i * tq:(i + 1) * tq, :] = (o * r * g_ref[...] * (1.0 - lambda_init)).astype(o_ref.dtype)


def _diff_attention(proj, lam_parts, subln_g, b, s, lambda_init):
    t = b * s
    tq = min(ATT_TQ, s)
    kern = functools.partial(_diff_attn_kernel, tq=tq, lambda_init=lambda_init)
    return pl.pallas_call(
        kern,
        grid=(b, DIFF_HEADS),
        in_specs=[pl.BlockSpec((4, DIFF_HALF_DIM), lambda bi, h: (0, 0)),
                  pl.BlockSpec((1, LANES), lambda bi, h: (0, 0)),
                  pl.BlockSpec((s, LANES), lambda bi, h: (bi, h)),
                  pl.BlockSpec((s, LANES), lambda bi, h: (bi, DIFF_HEADS + h)),
                  pl.BlockSpec((s, LANES), lambda bi, h: (bi, 2 * DIFF_HEADS + h))],
        out_specs=pl.BlockSpec((s, LANES), lambda bi, h: (bi, h)),
        out_shape=jax.ShapeDtypeStruct((t, A_WIDTH), BF16),
        scratch_shapes=[pltpu.VMEM((tq, s), F32), pltpu.VMEM((tq, s), BF16),
                        pltpu.VMEM((tq, s), F32), pltpu.VMEM((tq, s), BF16)],
        compiler_params=_cparams(("parallel", "parallel")),
        name="diff_attention",
    )(lam_parts, subln_g.reshape(1, LANES), proj, proj, proj)


def _mla_attn_kernel(qn_ref, qr_ref, kn_ref, kpe_ref, v_ref, o_ref, s_ref, p_ref, *, tq):
    h = pl.program_id(1)
    nq = qn_ref.shape[0] // tq
    lane = lax.broadcasted_iota(I32, (tq, LANES), 1)
    mine = (lane >= MLA_ROPE) == (h % 2 == 1)
    for i in range(nq):
        qn = qn_ref[i * tq:(i + 1) * tq, :]
        qr = qr_ref[i * tq:(i + 1) * tq, :]
        qr = jnp.where(mine, qr, jnp.zeros_like(qr))

        def block_scores(j, qn=qn, qr=qr):
            rows = slice(j * tq, (j + 1) * tq)
            return (lax.dot_general(qn, kn_ref[rows, :], _NT_DIMS, preferred_element_type=F32)
                    + lax.dot_general(qr, kpe_ref[rows, :], _NT_DIMS, preferred_element_type=F32))

        _score_strip(s_ref, i, tq, block_scores, None)
        o = _softmax_times_v(s_ref, p_ref, v_ref, (i + 1) * tq)
        o_ref[i * tq:(i + 1) * tq, :] = o.astype(o_ref.dtype)


def _mla_attention(qbuf, kvbuf, proj, b, s):
    t = b * s
    tq = min(ATT_TQ, s)
    kpe_block = (4 * PROJ_TN + MLA_KV_RANK) // LANES
    return pl.pallas_call(
        functools.partial(_mla_attn_kernel, tq=tq),
        grid=(b, MLA_HEADS),
        in_specs=[pl.BlockSpec((s, LANES), lambda bi, h: (bi, h)),
                  pl.BlockSpec((s, LANES), lambda bi, h: (bi, MLA_HEADS + h // 2)),
                  pl.BlockSpec((s, LANES), lambda bi, h: (bi, h)),
                  pl.BlockSpec((s, LANES), lambda bi, h: (bi, kpe_block)),
                  pl.BlockSpec((s, LANES), lambda bi, h: (bi, MLA_HEADS + h))],
        out_specs=pl.BlockSpec((s, LANES), lambda bi, h: (bi, h)),
        out_shape=jax.ShapeDtypeStruct((t, B_WIDTH), BF16),
        scratch_shapes=[pltpu.VMEM((tq, s), F32), pltpu.VMEM((tq, s), BF16)],
        compiler_params=_cparams(("parallel", "parallel")),
        name="mla_attention",
    )(qbuf, qbuf, kvbuf, proj, kvbuf)


def _moba_attn_kernel(q_ref, k_ref, v_ref, o_ref, s_ref, p_ref, kmean_ref):
    tq = MOBA_BLOCK
    nb = q_ref.shape[0] // tq
    kmean_ref[...] = jnp.zeros(kmean_ref.shape, F32)
    for j in range(nb):
        kb = k_ref[j * tq:(j + 1) * tq, :].astype(F32)
        kmean_ref[j:j + 1, :] = jnp.sum(kb, axis=0, keepdims=True) * (1.0 / tq)
    km = kmean_ref[...]
    km_hi = km.astype(BF16)
    km_lo = (km - km_hi.astype(F32)).astype(BF16)
    lane = lax.broadcasted_iota(I32, (tq, LANES), 1)
    for i in range(nb):
        q = q_ref[i * tq:(i + 1) * tq, :]
        gate = (lax.dot_general(q, km_hi, _NT_DIMS, preferred_element_type=F32)
                + lax.dot_general(q, km_lo, _NT_DIMS, preferred_element_type=F32))
        fully_past = lane < i
        g = jnp.where(fully_past, gate, -jnp.inf)
        cnt = jnp.zeros(gate.shape, F32)
        for jp in range(i):
            col = g[:, jp:jp + 1]
            ahead = (col > g) | ((col == g) & (jp < lane))
            cnt = cnt + jnp.where(ahead, 1.0, 0.0)
        kept = (cnt < MOBA_TOPK) & fully_past
        _score_strip(s_ref, i, tq,
                     lambda j, q=q: lax.dot_general(q, k_ref[j * tq:(j + 1) * tq, :], _NT_DIMS,
                                                    preferred_element_type=F32),
                     lambda j, kept=kept: kept[:, j:j + 1])
        o = _softmax_times_v(s_ref, p_ref, v_ref, (i + 1) * tq)
        o_ref[i * tq:(i + 1) * tq, :] = o.astype(o_ref.dtype)


def _moba_attention(proj, b, s):
    assert s % MOBA_BLOCK == 0 and s // MOBA_BLOCK <= LANES
    t = b * s
    base = 5 * PROJ_TN // LANES
    return pl.pallas_call(
        _moba_attn_kernel,
        grid=(b, MOBA_HEADS),
        in_specs=[pl.BlockSpec((s, LANES), lambda bi, h: (bi, base + h)),
                  pl.BlockSpec((s, LANES), lambda bi, h: (bi, base + MOBA_HEADS + h)),
                  pl.BlockSpec((s, LANES), lambda bi, h: (bi, base + 2 * MOBA_HEADS + h))],
        out_specs=pl.BlockSpec((s, LANES), lambda bi, h: (bi, h)),
        out_shape=jax.ShapeDtypeStruct((t, C_WIDTH), BF16),
        scratch_shapes=[pltpu.VMEM((MOBA_BLOCK, s), F32), pltpu.VMEM((MOBA_BLOCK, s), BF16),
                        pltpu.VMEM((LANES, LANES), F32)],
        compiler_params=_cparams(("parallel", "parallel")),
        name="moba_attention",
    )(proj, proj, proj)


def _layer_norm_rows(z, g, b):
    mu = jnp.mean(z, axis=-1, keepdims=True)
    zc = z - mu
    var = jnp.mean(zc * zc, axis=-1, keepdims=True)
    return zc * lax.rsqrt(var + NORM_EPS) * g + b


def _pack_halves(z):
    n = z.shape[1] // 2
    lo = lax.bitcast_convert_type(z[:, :n].astype(BF16).astype(F32), U32)
    hi = lax.bitcast_convert_type(z[:, n:].astype(BF16).astype(F32), U32)
    return (lo >> 16) | hi


def _unpack_halves(u):
    lo = lax.bitcast_convert_type(u << 16, F32)
    hi = lax.bitcast_convert_type(u & jnp.uint32(0xFFFF0000), F32)
    return lo, hi


ROW_WORDS = D_MODEL // 2
ROW_SUBLANES = ROW_WORDS // LANES


def _store_row_tiles(ref, packed):
    tm = packed.shape[0]
    for a in range(ROW_SUBLANES):
        ref[pl.ds(a, tm, stride=ROW_SUBLANES), :] = packed[:, a * LANES:(a + 1) * LANES]


def _load_row_tile_chunk(ref, a, tm):
    return ref[pl.ds(a, tm, stride=ROW_SUBLANES), :]


def _merge_kernel(oa_ref, ob_ref, oc_ref, g0_ref, g1_ref, g2_ref, x_ref, wb_ref, wo_ref,
                  lng_ref, lnb_ref, xo_ref, xb_ref, xp_ref):
    ya = jnp.dot(oa_ref[...], wb_ref[:A_WIDTH, :], preferred_element_type=F32)
    yb = jnp.dot(ob_ref[...], wb_ref[A_WIDTH:A_WIDTH + B_WIDTH, :], preferred_element_type=F32)
    yc = jnp.dot(oc_ref[...], wb_ref[A_WIDTH + B_WIDTH:, :], preferred_element_type=F32)
    y = g0_ref[...].astype(F32) * ya + g1_ref[...].astype(F32) * yb + g2_ref[...].astype(F32) * yc
    mix = jnp.dot(y.astype(BF16), wo_ref[...], preferred_element_type=F32)
    z = _layer_norm_rows(DEEPNORM_ALPHA * x_ref[...] + mix, lng_ref[...], lnb_ref[...])
    xo_ref[...] = z
    xb_ref[...] = z.astype(BF16)
    _store_row_tiles(xp_ref, _pack_halves(z))


def _merge(oa, ob, oc, proj, x, w_branch, w_out, ln_g, ln_b):
    t = x.shape[0]
    tm = min(256, t)
    gate_base = 8 * PROJ_TN // D_MODEL
    row = lambda i: (i, 0)
    whole = lambda i: (0, 0)
    return pl.pallas_call(
        _merge_kernel,
        grid=(t // tm,),
        in_specs=[pl.BlockSpec((tm, A_WIDTH), row),
                  pl.BlockSpec((tm, B_WIDTH), row),
                  pl.BlockSpec((tm, C_WIDTH), row),
                  pl.BlockSpec((tm, D_MODEL), lambda i: (i, gate_base)),
                  pl.BlockSpec((tm, D_MODEL), lambda i: (i, gate_base + 1)),
                  pl.BlockSpec((tm, D_MODEL), lambda i: (i, gate_base + 2)),
                  pl.BlockSpec((tm, D_MODEL), row),
                  pl.BlockSpec((D_MODEL, D_MODEL), whole),
                  pl.BlockSpec((D_MODEL, D_MODEL), whole),
                  pl.BlockSpec((1, D_MODEL), whole),
                  pl.BlockSpec((1, D_MODEL), whole)],
        out_specs=[pl.BlockSpec((tm, D_MODEL), row),
                   pl.BlockSpec((tm, D_MODEL), row),
                   pl.BlockSpec((tm * ROW_SUBLANES, LANES), row)],
        out_shape=[jax.ShapeDtypeStruct((t, D_MODEL), F32),
                   jax.ShapeDtypeStruct((t, D_MODEL), BF16),
                   jax.ShapeDtypeStruct((t * ROW_SUBLANES, LANES), U32)],
        compiler_params=_cparams(("parallel",)),
        name="merge_outproj_ln",
    )(oa, ob, oc, proj, proj, proj, x, w_branch, w_out, ln_g.reshape(1, -1), ln_b.reshape(1, -1))


def _split_bf16(a):
    hi = a.astype(BF16)
    lo = (a - hi.astype(F32)).astype(BF16)
    return hi, lo


def _router_kernel(x_ref, wt_ref, bias_ref, upper_ref, lower_ref,
                   e8_ref, r8_ref, w8_ref, cnt_ref, carry_ref):
    i = pl.program_id(0)

    @pl.when(i == 0)
    def _():
        carry_ref[...] = jnp.zeros(carry_ref.shape, F32)

    xh, xl = _split_bf16(x_ref[...])
    wh, wl = _split_bf16(wt_ref[...])
    logits = (lax.dot_general(wh, xh, _NT_DIMS, preferred_element_type=F32)
              + lax.dot_general(wh, xl, _NT_DIMS, preferred_element_type=F32)
              + lax.dot_general(wl, xh, _NT_DIMS, preferred_element_type=F32))
    scores = 1.0 / (1.0 + jnp.exp(-logits))
    choice = scores + bias_ref[:, 0:1]
    tm = choice.shape[1]
    sub = lax.broadcasted_iota(I32, (GROUP_SIZE, tm), 0)

    group_rows = []
    for g in range(N_GROUPS):
        cg = choice[g * GROUP_SIZE:(g + 1) * GROUP_SIZE, :]
        m1 = jnp.max(cg, axis=0, keepdims=True)
        first = jnp.min(jnp.where(cg == m1, sub, GROUP_SIZE), axis=0, keepdims=True)
        m2 = jnp.max(jnp.where(sub == first, -jnp.inf, cg), axis=0, keepdims=True)
        group_rows.append(m1 + m2)
    gs = jnp.concatenate(group_rows, axis=0)
    ahead = jnp.zeros(gs.shape, F32)
    for gp in range(N_GROUPS):
        rowv = gs[gp:gp + 1, :]
        ahead = ahead + jnp.where((rowv > gs) | ((rowv == gs) & (gp < sub)), 1.0, 0.0)
    keep_group = ahead < TOPK_GROUPS
    masked = jnp.concatenate(
        [jnp.where(keep_group[g:g + 1, :], choice[g * GROUP_SIZE:(g + 1) * GROUP_SIZE, :], -jnp.inf)
         for g in range(N_GROUPS)], axis=0)

    eidx = lax.broadcasted_iota(I32, masked.shape, 0)
    ahead = jnp.zeros(masked.shape, F32)
    for ep in range(N_EXPERTS):
        rowv = masked[ep:ep + 1, :]
        ahead = ahead + jnp.where((rowv > masked) | ((rowv == masked) & (ep < eidx)), 1.0, 0.0)
    sel = ahead < TOPK
    picked = jnp.where(sel, scores, 0.0)
    gates = picked / jnp.sum(picked, axis=0, keepdims=True) * ROUTED_SCALE
    self32 = jnp.where(sel, 1.0, 0.0)
    selb = self32.astype(BF16)

    rank = jnp.dot(selb, upper_ref[...], preferred_element_type=F32) + carry_ref[:, 0:1]
    slot = jnp.dot(lower_ref[...], selb, preferred_element_type=F32)
    carry_ref[...] = carry_ref[...] + jnp.sum(self32, axis=1, keepdims=True)
    cnt_ref[...] = carry_ref[...].astype(I32)

    eidf = eidx.astype(F32)
    e_rows, r_rows, w_rows = [], [], []
    for k in range(TOPK):
        hit = sel & (slot == float(k))
        e_rows.append(jnp.sum(jnp.where(hit, eidf, 0.0), axis=0, keepdims=True))
        r_rows.append(jnp.sum(jnp.where(hit, rank, 0.0), axis=0, keepdims=True))
        w_rows.append(jnp.sum(jnp.where(hit, gates, 0.0), axis=0, keepdims=True))
    e8_ref[...] = jnp.concatenate(e_rows, axis=0).astype(I32)
    r8_ref[...] = jnp.concatenate(r_rows, axis=0).astype(I32)
    w8_ref[...] = jnp.concatenate(w_rows, axis=0)


def _router(x, router_w, router_bias):
    t = x.shape[0]
    tm = min(ROUTER_TM, t)
    upper = (jnp.arange(tm)[:, None] < jnp.arange(tm)[None, :]).astype(BF16)
    lower = (jnp.arange(N_EXPERTS)[None, :] < jnp.arange(N_EXPERTS)[:, None]).astype(BF16)
    bias = jnp.broadcast_to(router_bias.astype(F32)[:, None], (N_EXPERTS, LANES))
    whole = lambda i: (0, 0)
    col = lambda i: (0, i)
    return pl.pallas_call(
        _router_kernel,
        grid=(t // tm,),
        in_specs=[pl.BlockSpec((tm, D_MODEL), lambda i: (i, 0)),
                  pl.BlockSpec((N_EXPERTS, D_MODEL), whole),
                  pl.BlockSpec((N_EXPERTS, LANES), whole),
                  pl.BlockSpec((tm, tm), whole),
                  pl.BlockSpec((N_EXPERTS, N_EXPERTS), whole)],
        out_specs=[pl.BlockSpec((TOPK, tm), col),
                   pl.BlockSpec((TOPK, tm), col),
                   pl.BlockSpec((TOPK, tm), col),
                   pl.BlockSpec((N_EXPERTS, LANES), whole)],
        out_shape=[jax.ShapeDtypeStruct((TOPK, t), I32),
                   jax.ShapeDtypeStruct((TOPK, t), I32),
                   jax.ShapeDtypeStruct((TOPK, t), F32),
                   jax.ShapeDtypeStruct((N_EXPERTS, LANES), I32)],
        scratch_shapes=[pltpu.VMEM((N_EXPERTS, LANES), F32)],
        compiler_params=_cparams(("arbitrary",)),
        name="router",
    )(x, router_w.T, bias, upper, lower)


def _row_tile(ref, r):
    return ref.at[pl.ds(pl.multiple_of(r * ROW_SUBLANES, ROW_SUBLANES), ROW_SUBLANES)]


def _dispatch_kernel(pos_ref, x_ref, xs_ref, sem):
    tm = x_ref.shape[0] // ROW_SUBLANES

    def row_copy(t, k):
        return pltpu.make_async_copy(_row_tile(x_ref, t), _row_tile(xs_ref, pos_ref[k, t]), sem)

    def start(t, c):
        for k in range(TOPK):
            row_copy(t, k).start(priority=k % 2)
        return c
    lax.fori_loop(0, tm, start, 0)

    def wait(t, c):
        for k in range(TOPK):
            row_copy(t, k).wait()
        return c
    lax.fori_loop(0, tm, wait, 0)


def _dispatch(xp, pos8, n_rows):
    t = xp.shape[0] // ROW_SUBLANES
    tm = min(ROW_TM, t)
    return pl.pallas_call(
        _dispatch_kernel,
        grid=(t // tm,),
        in_specs=[pl.BlockSpec((TOPK, tm), lambda i: (0, i), memory_space=pltpu.SMEM),
                  pl.BlockSpec((tm * ROW_SUBLANES, LANES), lambda i: (i, 0))],
        out_specs=pl.BlockSpec(memory_space=pl.ANY),
        scratch_shapes=[pltpu.SemaphoreType.DMA(())],
        out_shape=jax.ShapeDtypeStruct((n_rows * ROW_SUBLANES, LANES), U32),
        compiler_params=_cparams(("arbitrary",)),
        name="dispatch",
    )(pos8, xp)


def _expert_kernel(te_ref, rows_ref, nv_ref, xs_ref, wg_ref, wu_ref, wd_ref, ys_ref,
                   wgb_ref, wub_ref, wdb_ref):
    i = pl.program_id(0)

    @pl.when(i < nv_ref[0])
    def _():
        @pl.when((i == 0) | (te_ref[i] != te_ref[jnp.maximum(i - 1, 0)]))
        def _():
            wgb_ref[...] = wg_ref[0].astype(BF16)
            wub_ref[...] = wu_ref[0].astype(BF16)
            wdb_ref[...] = wd_ref[0].astype(BF16)

        half = D_MODEL // 2
        tm = EXPERT_TM
        u = jnp.concatenate([_load_row_tile_chunk(xs_ref, a, tm) for a in range(ROW_SUBLANES)], axis=1)
        live = lax.broadcasted_iota(I32, (tm, 1), 0) < rows_ref[i]
        lo, hi = _unpack_halves(jnp.where(live, u, jnp.zeros_like(u)))
        xl = lo.astype(BF16)
        xh = hi.astype(BF16)
        gate = (jnp.dot(xl, wgb_ref[:half, :], preferred_element_type=F32)
                + jnp.dot(xh, wgb_ref[half:, :], preferred_element_type=F32))
        up = (jnp.dot(xl, wub_ref[:half, :], preferred_element_type=F32)
              + jnp.dot(xh, wub_ref[half:, :], preferred_element_type=F32))
        hid = (gate / (1.0 + jnp.exp(-gate))) * up
        y = jnp.dot(hid.astype(BF16), wdb_ref[...], preferred_element_type=F32)
        _store_row_tiles(ys_ref, _pack_halves(y))


def _experts(xs, tile_expert, tile_rows, n_valid, w_gate, w_up, w_down, layer):
    n_rows = xs.shape[0] // ROW_SUBLANES
    n_tiles = n_rows // EXPERT_TM
    row = lambda i, te, tr, nv: (jnp.minimum(i, nv[0] - 1), 0)
    wsel = lambda i, te, tr, nv: (layer * N_EXPERTS + te[i], 0, 0)
    grid_spec = pltpu.PrefetchScalarGridSpec(
        num_scalar_prefetch=3,
        grid=(n_tiles,),
        in_specs=[pl.BlockSpec((EXPERT_TM * ROW_SUBLANES, LANES), row),
                  pl.BlockSpec((1, D_MODEL, EXPERT_FF), wsel),
                  pl.BlockSpec((1, D_MODEL, EXPERT_FF), wsel),
                  pl.BlockSpec((1, EXPERT_FF, D_MODEL), wsel)],
        out_specs=pl.BlockSpec((EXPERT_TM * ROW_SUBLANES, LANES), row),
        scratch_shapes=[pltpu.VMEM((D_MODEL, EXPERT_FF), BF16), pltpu.VMEM((D_MODEL, EXPERT_FF), BF16),
                        pltpu.VMEM((EXPERT_FF, D_MODEL), BF16)],
    )
    return pl.pallas_call(
        _expert_kernel,
        grid_spec=grid_spec,
        out_shape=jax.ShapeDtypeStruct((n_rows * ROW_SUBLANES, LANES), U32),
        compiler_params=_cparams(("arbitrary",)),
        name="experts",
    )(tile_expert, tile_rows, n_valid, xs, w_gate, w_up, w_down)


def _combine_kernel(pos_ref, w8_ref, ys_ref, x_ref, xb_ref, wsg_ref, wsu_ref, wsd_ref,
                    lng_ref, lnb_ref, xo_ref, xob_ref, buf_ref, sem):
    tm = x_ref.shape[0]

    def row_copy(t, k):
        return pltpu.make_async_copy(_row_tile(ys_ref, pos_ref[k, t]), _row_tile(buf_ref.at[k], t), sem)

    def start(t, c):
        for k in range(TOPK):
            row_copy(t, k).start(priority=k % 2)
        return c
    lax.fori_loop(0, tm, start, 0)

    xb = xb_ref[...]
    sg = jnp.dot(xb, wsg_ref[...], preferred_element_type=F32)
    su = jnp.dot(xb, wsu_ref[...], preferred_element_type=F32)
    hid = (sg / (1.0 + jnp.exp(-sg))) * su
    shared = jnp.dot(hid.astype(BF16), wsd_ref[...], preferred_element_type=F32)

    def wait(t, c):
        for k in range(TOPK):
            row_copy(t, k).wait()
        return c
    lax.fori_loop(0, tm, wait, 0)

    w8 = w8_ref[...]
    wk = [jnp.broadcast_to(w8[:, k:k + 1], (tm, LANES)) for k in range(TOPK)]
    lo_parts, hi_parts = [], []
    for a in range(ROW_SUBLANES):
        acc_lo = jnp.zeros((tm, LANES), F32)
        acc_hi = jnp.zeros((tm, LANES), F32)
        for k in range(TOPK):
            lo, hi = _unpack_halves(_load_row_tile_chunk(buf_ref.at[k], a, tm))
            acc_lo = acc_lo + wk[k] * lo
            acc_hi = acc_hi + wk[k] * hi
        lo_parts.append(acc_lo)
        hi_parts.append(acc_hi)
    routed = jnp.concatenate(lo_parts + hi_parts, axis=1)
    z = _layer_norm_rows(DEEPNORM_ALPHA * x_ref[...] + (routed + shared), lng_ref[...], lnb_ref[...])
    xo_ref[...] = z
    xob_ref[...] = z.astype(BF16)


def _combine(ys, pos8, w8t, x, xb, ws_gate, ws_up, ws_down, ln_g, ln_b):
    t = x.shape[0]
    tm = min(ROW_TM, t)
    row = lambda i: (i, 0)
    whole = lambda i: (0, 0)
    grid_spec = pl.GridSpec(
        grid=(t // tm,),
        in_specs=[pl.BlockSpec((TOPK, tm), lambda i: (0, i), memory_space=pltpu.SMEM),
                  pl.BlockSpec((tm, TOPK), row),
                  pl.BlockSpec(memory_space=pl.ANY),
                  pl.BlockSpec((tm, D_MODEL), row),
                  pl.BlockSpec((tm, D_MODEL), row),
                  pl.BlockSpec((D_MODEL, EXPERT_FF), whole),
                  pl.BlockSpec((D_MODEL, EXPERT_FF), whole),
                  pl.BlockSpec((EXPERT_FF, D_MODEL), whole),
                  pl.BlockSpec((1, D_MODEL), whole),
                  pl.BlockSpec((1, D_MODEL), whole)],
        out_specs=[pl.BlockSpec((tm, D_MODEL), row),
                   pl.BlockSpec((tm, D_MODEL), row)],
        scratch_shapes=[pltpu.VMEM((TOPK, tm * ROW_SUBLANES, LANES), U32), pltpu.SemaphoreType.DMA(())],
    )
    return pl.pallas_call(
        _combine_kernel,
        grid_spec=grid_spec,
        out_shape=[jax.ShapeDtypeStruct((t, D_MODEL), F32),
                   jax.ShapeDtypeStruct((t, D_MODEL), BF16)],
        compiler_params=_cparams(("arbitrary",)),
        name="combine_shared_ln",
    )(pos8, w8t, ys, x, xb, ws_gate, ws_up, ws_down, ln_g.reshape(1, -1), ln_b.reshape(1, -1))


def _in_proj_weight(w_in):
    sizes = (512, 512, 512, MLA_Q_RANK, MLA_KV_RANK, MLA_ROPE, 512, 512, 512, 3 * D_MODEL)
    offs = np.concatenate([[0], np.cumsum(sizes)])
    seg = [w_in[:, offs[n]:offs[n + 1]] for n in range(len(sizes))]
    dq, dk, dv, cq, ckv, kpe, mq, mk, mv, gates = seg
    pad = jnp.zeros((w_in.shape[0], PROJ_TN - MLA_KV_RANK - 2 * MLA_ROPE), w_in.dtype)
    return jnp.concatenate([dq, dk, dv, cq, ckv, kpe, kpe, pad, mq, mk, mv, gates], axis=1).astype(BF16)


def _uq_weight(w_uq):
    w = w_uq.reshape(MLA_Q_RANK, MLA_HEADS, MLA_NOPE + MLA_ROPE)
    return jnp.concatenate([w[:, :, :MLA_NOPE].reshape(MLA_Q_RANK, -1),
                            w[:, :, MLA_NOPE:].reshape(MLA_Q_RANK, -1)], axis=1).astype(BF16)


def _ukv_weight(w_ukv):
    w = w_ukv.reshape(MLA_KV_RANK, MLA_HEADS, MLA_NOPE + MLA_V)
    return jnp.concatenate([w[:, :, :MLA_NOPE].reshape(MLA_KV_RANK, -1),
                            w[:, :, MLA_NOPE:].reshape(MLA_KV_RANK, -1)], axis=1).astype(BF16)


def _expert_tiling(counts, n_tiles):
    tiles = (counts + EXPERT_TM - 1) // EXPERT_TM
    ends = jnp.cumsum(tiles)
    offsets = ((ends - tiles) * EXPERT_TM).astype(I32)
    tile_ids = jnp.arange(n_tiles, dtype=I32)
    tile_expert = jnp.minimum(
        jnp.sum((ends[None, :] <= tile_ids[:, None]).astype(I32), axis=1), N_EXPERTS - 1).astype(I32)
    first_tile = (ends - tiles).astype(I32)
    tile_rows = jnp.clip(counts[tile_expert] - (tile_ids - first_tile[tile_expert]) * EXPERT_TM,
                         0, EXPERT_TM).astype(I32)
    return offsets, tile_expert, tile_rows, ends[-1:].astype(I32)


def kernel(x, positions, w_in, diff_lambda_q1, diff_lambda_k1, diff_lambda_q2, diff_lambda_k2, diff_subln_g, mla_q_norm_g, mla_w_uq, mla_kv_norm_g, mla_w_ukv, w_branch, w_out, ln_mix_g, ln_mix_b, router_w, router_bias, expert_w_gate, expert_w_up, expert_w_down, shared_w_gate, shared_w_up, shared_w_down, ln_ffn_g, ln_ffn_b):
    b, s, d = x.shape
    t = b * s
    depth = w_in.shape[0]
    cos, sin = _rope_tables(positions)
    xf = x.reshape(t, d)
    xb = xf.astype(BF16)
    n_tiles = t * TOPK // EXPERT_TM + N_EXPERTS
    proj_tm = min(1024, t)

    for l in range(depth):
        gains = jnp.zeros((8, PROJ_TN), F32)
        gains = gains.at[0, :].set(mla_q_norm_g[l]).at[1, :MLA_KV_RANK].set(mla_kv_norm_g[l])
        proj = _proj(xb, 0, d, _in_proj_weight(w_in[l]), IN_TILE_MODES, IN_TILE_SCALES,
                     cos, sin, gains, proj_tm)
        qbuf = _proj(proj, 3, MLA_Q_RANK, _uq_weight(mla_w_uq[l]),
                     [MODE_PLAIN, MODE_PLAIN, MODE_ROPE_MLA], [(MLA_NOPE + MLA_ROPE) ** -0.5] * 3,
                     cos, sin, gains, proj_tm)
        kvbuf = _proj(proj, 4 * PROJ_TN // MLA_KV_RANK, MLA_KV_RANK, _ukv_weight(mla_w_ukv[l]),
                      [MODE_PLAIN] * 4, [1.0] * 4, cos, sin, gains, proj_tm)

        lambda_init = 0.8 - 0.6 * math.exp(-0.3 * l)
        lam_parts = jnp.stack([diff_lambda_q1[l], diff_lambda_k1[l],
                               diff_lambda_q2[l], diff_lambda_k2[l]]).astype(F32)
        oa = _diff_attention(proj, lam_parts, diff_subln_g[l], b, s, lambda_init)
        ob = _mla_attention(qbuf, kvbuf, proj, b, s)
        oc = _moba_attention(proj, b, s)

        x1, x1b, x1p = _merge(oa, ob, oc, proj, xf, w_branch[l].astype(BF16), w_out[l].astype(BF16),
                              ln_mix_g[l], ln_mix_b[l])

        e8, r8, w8, counts = _router(x1, router_w[l], router_bias[l])
        offsets, tile_expert, tile_rows, n_valid = _expert_tiling(counts[:, 0], n_tiles)
        expert_ids = jnp.arange(N_EXPERTS, dtype=I32)[:, None, None]
        pos8 = jnp.sum(jnp.where(e8[None] == expert_ids, offsets[:, None, None], 0), axis=0) + r8
        xs = _dispatch(x1p, pos8, n_tiles * EXPERT_TM)
        ys = _experts(xs, tile_expert, tile_rows, n_valid,
                      expert_w_gate.reshape(depth * N_EXPERTS, d, EXPERT_FF),
                      expert_w_up.reshape(depth * N_EXPERTS, d, EXPERT_FF),
                      expert_w_down.reshape(depth * N_EXPERTS, EXPERT_FF, d), l)
        xf, xb = _combine(ys, pos8, w8.T, x1, x1b, shared_w_gate[l].astype(BF16),
                          shared_w_up[l].astype(BF16), shared_w_down[l].astype(BF16),
                          ln_ffn_g[l], ln_ffn_b[l])
    return xf.reshape(b, s, d)
```

```python
import functools
import math

import numpy as np
import jax
import jax.numpy as jnp
from jax import lax
from jax.experimental import pallas as pl
from jax.experimental.pallas import tpu as pltpu

F32 = jnp.float32
BF16 = jnp.bfloat16
I32 = jnp.int32
U32 = jnp.uint32

D_MODEL = 2048
DEPTH = 2
ROPE_THETA = 500000.0
NORM_EPS = 1e-5

DIFF_HEADS = 4
DIFF_HALF_DIM = 64
DIFF_ROT = 16
MLA_HEADS = 8
MLA_Q_RANK = 512
MLA_KV_RANK = 256
MLA_NOPE = 128
MLA_ROPE = 64
MLA_V = 128
MOBA_HEADS = 4
MOBA_HEAD_DIM = 128
MOBA_BLOCK = 256
MOBA_TOPK = 3
MOBA_ROT = 32
A_WIDTH = 512
B_WIDTH = 1024
C_WIDTH = 512

N_EXPERTS = 64
N_GROUPS = 8
GROUP_SIZE = N_EXPERTS // N_GROUPS
TOPK_GROUPS = 4
TOPK = 8
EXPERT_FF = 512
ROUTED_SCALE = 2.5

DEEPNORM_ALPHA = (2 * DEPTH) ** 0.25

LANES = 128
NEG_BIG = -1e30

PROJ_TN = 512
MODE_PLAIN, MODE_ROPE_DIFF, MODE_ROPE_MLA, MODE_ROPE_MOBA, MODE_RMS, MODE_CKV, MODE_SIGMOID = range(7)
_ROPE_OF_MODE = {MODE_ROPE_DIFF: (0, DIFF_ROT // 2), MODE_ROPE_MLA: (1, MLA_ROPE // 2),
                 MODE_ROPE_MOBA: (2, MOBA_ROT // 2)}
_ROPE_PERIOD = (DIFF_HALF_DIM, MLA_ROPE, MOBA_HEAD_DIM)
_ROPE_ROT = (DIFF_ROT, MLA_ROPE, MOBA_ROT)

IN_TILE_MODES = ([MODE_ROPE_DIFF, MODE_ROPE_DIFF, MODE_PLAIN, MODE_RMS, MODE_CKV,
                  MODE_ROPE_MOBA, MODE_ROPE_MOBA, MODE_PLAIN] + [MODE_SIGMOID] * 12)
IN_TILE_SCALES = [DIFF_HALF_DIM ** -0.5] + [1.0] * 4 + [MOBA_HEAD_DIM ** -0.5] + [1.0] * 14
IN_COLS_PADDED = PROJ_TN * len(IN_TILE_MODES)

VMEM_LIMIT = 56 * 1024 * 1024

ATT_TQ = 256
EXPERT_TM = 512
ROW_TM = 256
ROUTER_TM = 1024


def _cparams(sem):
    return pltpu.CompilerParams(dimension_semantics=sem, vmem_limit_bytes=VMEM_LIMIT)


def _rope_table_kernel(pos_ref, c_ref, cos_ref, sin_ref):
    pos = pos_ref[...].astype(F32)
    for p in range(3):
        ang = pos * c_ref[p:p + 1, :]
        cos_ref[p] = jnp.cos(ang)
        sin_ref[p] = jnp.sin(ang) * c_ref[3 + p:4 + p, :]


def _rope_tables(positions):
    t = positions.size
    lane = jnp.arange(LANES)
    rows = []
    signs = []
    for period, rot in zip(_ROPE_PERIOD, _ROPE_ROT):
        half = rot // 2
        inv_freq = ROPE_THETA ** (-jnp.arange(0, rot, 2, dtype=F32) / rot)
        cp = lane % period
        active = cp < rot
        rows.append(jnp.where(active, inv_freq[cp % half], 0.0))
        signs.append(jnp.where(active, jnp.where(cp < half, -1.0, 1.0), 0.0))
    consts = jnp.stack(rows + signs + [jnp.zeros((LANES,), F32)] * 2).astype(F32)
    tm = min(1024, t)
    cos, sin = pl.pallas_call(
        _rope_table_kernel,
        grid=(t // tm,),
        in_specs=[pl.BlockSpec((tm, 1), lambda i: (i, 0)),
                  pl.BlockSpec((8, LANES), lambda i: (0, 0))],
        out_specs=[pl.BlockSpec((3, tm, LANES), lambda i: (0, i, 0)),
                   pl.BlockSpec((3, tm, LANES), lambda i: (0, i, 0))],
        out_shape=[jax.ShapeDtypeStruct((3, t, LANES), F32)] * 2,
        compiler_params=_cparams(("parallel",)),
        name="rope_tables",
    )(positions.reshape(t, 1), consts)
    return cos, sin


def _proj_kernel(mode_ref, scale_ref, x_ref, w_ref, cos_ref, sin_ref, g_ref, o_ref):
    j = pl.program_id(1)
    mode = mode_ref[j]
    y = jnp.dot(x_ref[...], w_ref[...], preferred_element_type=F32) * scale_ref[j]
    tm, tn = y.shape
    lane = lax.broadcasted_iota(I32, (tm, LANES), 1)

    def rope_chunk(yc, table, half):
        first = (lane % _ROPE_PERIOD[table]) < half
        swapped = jnp.where(first, pltpu.roll(yc, LANES - half, 1), pltpu.roll(yc, half, 1))
        return yc * cos_ref[table] + swapped * sin_ref[table]

    @pl.when(mode == MODE_PLAIN)
    def _():
        o_ref[...] = y.astype(o_ref.dtype)

    for rope_mode, (table, half) in _ROPE_OF_MODE.items():
        @pl.when(mode == rope_mode)
        def _(table=table, half=half):
            for c in range(tn // LANES):
                sl = slice(c * LANES, (c + 1) * LANES)
                o_ref[:, sl] = rope_chunk(y[:, sl], table, half).astype(o_ref.dtype)

    @pl.when(mode == MODE_RMS)
    def _():
        r = lax.rsqrt(jnp.mean(y * y, axis=-1, keepdims=True) + NORM_EPS)
        o_ref[...] = (y * r * g_ref[0:1, :]).astype(o_ref.dtype)

    @pl.when(mode == MODE_CKV)
    def _():
        ckv = y[:, :MLA_KV_RANK]
        r = lax.rsqrt(jnp.mean(ckv * ckv, axis=-1, keepdims=True) + NORM_EPS)
        o_ref[:, :MLA_KV_RANK] = (ckv * r * g_ref[1:2, :MLA_KV_RANK]).astype(o_ref.dtype)
        sl = slice(MLA_KV_RANK, MLA_KV_RANK + LANES)
        table, half = _ROPE_OF_MODE[MODE_ROPE_MLA]
        o_ref[:, sl] = rope_chunk(y[:, sl], table, half).astype(o_ref.dtype)
        o_ref[:, MLA_KV_RANK + LANES:] = jnp.zeros((tm, tn - MLA_KV_RANK - LANES), o_ref.dtype)

    @pl.when(mode == MODE_SIGMOID)
    def _():
        o_ref[...] = (1.0 / (1.0 + jnp.exp(-y))).astype(o_ref.dtype)


def _proj(x, x_col_block, k_dim, w, modes, scales, cos, sin, gains, tm):
    t = x.shape[0]
    n = w.shape[1]
    assert n % PROJ_TN == 0 and len(modes) == len(scales) == n // PROJ_TN and w.shape[0] == k_dim
    grid_spec = pltpu.PrefetchScalarGridSpec(
        num_scalar_prefetch=1,
        grid=(t // tm, n // PROJ_TN),
        in_specs=[pl.BlockSpec(memory_space=pltpu.SMEM),
                  pl.BlockSpec((tm, k_dim), lambda i, j, m: (i, x_col_block)),
                  pl.BlockSpec((k_dim, PROJ_TN), lambda i, j, m: (0, j)),
                  pl.BlockSpec((3, tm, LANES), lambda i, j, m: (0, i, 0)),
                  pl.BlockSpec((3, tm, LANES), lambda i, j, m: (0, i, 0)),
                  pl.BlockSpec((8, PROJ_TN), lambda i, j, m: (0, 0))],
        out_specs=pl.BlockSpec((tm, PROJ_TN), lambda i, j, m: (i, j)),
    )
    return pl.pallas_call(
        _proj_kernel,
        grid_spec=grid_spec,
        out_shape=jax.ShapeDtypeStruct((t, n), BF16),
        compiler_params=_cparams(("parallel", "arbitrary")),
        name="proj",
    )(jnp.asarray(modes, I32), jnp.asarray(scales, F32), x, w, cos, sin, gains)


_NT_DIMS = (((1,), (1,)), ((), ()))


def _causal_mask(tq):
    row = lax.broadcasted_iota(I32, (tq, tq), 0)
    col = lax.broadcasted_iota(I32, (tq, tq), 1)
    return col <= row


def _score_strip(s_ref, i, tq, block_scores, block_mask):
    causal = _causal_mask(tq)
    for j in range(i + 1):
        sc = block_scores(j)
        if j == i:
            sc = jnp.where(causal, sc, NEG_BIG)
        elif block_mask is not None:
            sc = jnp.where(block_mask(j), sc, NEG_BIG)
        s_ref[:, j * tq:(j + 1) * tq] = sc


def _softmax_times_v(s_ref, p_ref, v_ref, n):
    tq = s_ref.shape[0]
    nch = n // LANES
    mrun = s_ref[:, 0:LANES]
    for c in range(1, nch):
        mrun = jnp.maximum(mrun, s_ref[:, c * LANES:(c + 1) * LANES])
    m = jnp.broadcast_to(jnp.max(mrun, axis=-1, keepdims=True), (tq, LANES))
    lrun = jnp.zeros((tq, LANES), F32)
    for c in range(nch):
        sl = slice(c * LANES, (c + 1) * LANES)
        p = jnp.exp(s_ref[:, sl] - m)
        lrun = lrun + p
        p_ref[:, sl] = p.astype(BF16)
    l = jnp.sum(lrun, axis=-1, keepdims=True)
    o = jnp.dot(p_ref[:, :n], v_ref[0:n, :], preferred_element_type=F32)
    return o / l


def _diff_attn_kernel(lam_ref, g_ref, q_ref, k_ref, v_ref, o_ref, s1_ref, p1_ref, s2_ref, p2_ref,
                      *, tq, lambda_init):
    nq = q_ref.shape[0] // tq
    lp = lam_ref[...]
    lam = (jnp.exp(jnp.sum(lp[0:1] * lp[1:2], axis=-1, keepdims=True))
           - jnp.exp(jnp.sum(lp[2:3] * lp[3:4], axis=-1, keepdims=True)) + lambda_init)
    lane = lax.broadcasted_iota(I32, (tq, LANES), 1)
    for i in range(nq):
        q = q_ref[i * tq:(i + 1) * tq, :]
        q1 = jnp.where(lane < DIFF_HALF_DIM, q, jnp.zeros_like(q))
        q2 = jnp.where(lane >= DIFF_HALF_DIM, q, jnp.zeros_like(q))
        kblk = lambda j: k_ref[j * tq:(j + 1) * tq, :]
        b = i % 2
        _score_strip(s1_ref.at[b], i, tq,
                     lambda j: lax.dot_general(q1, kblk(j), _NT_DIMS, preferred_element_type=F32), None)
        _score_strip(s2_ref.at[b], i, tq,
                     lambda j: lax.dot_general(q2, kblk(j), _NT_DIMS, preferred_element_type=F32), None)
        n = (i + 1) * tq
        o = (_softmax_times_v(s1_ref.at[b], p1_ref.at[b], v_ref, n)
             - lam * _softmax_times_v(s2_ref.at[b], p2_ref.at[b], v_ref, n))
        r = lax.rsqrt(jnp.mean(o * o, axis=-1, keepdims=True) + NORM_EPS)
        o_ref[i * tq:(i + 1) * tq, :] = (o * r * g_ref[...] * (1.0 - lambda_init)).astype(o_ref.dtype)


def _diff_attention(proj, lam_parts, subln_g, b, s, lambda_init):
    t = b * s
    tq = min(ATT_TQ, s)
    kern = functools.partial(_diff_attn_kernel, tq=tq, lambda_init=lambda_init)
    return pl.pallas_call(
        kern,
        grid=(b, DIFF_HEADS),
        in_specs=[pl.BlockSpec((4, DIFF_HALF_DIM), lambda bi, h: (0, 0)),
                  pl.BlockSpec((1, LANES), lambda bi, h: (0, 0)),
                  pl.BlockSpec((s, LANES), lambda bi, h: (bi, h)),
                  pl.BlockSpec((s, LANES), lambda bi, h: (bi, DIFF_HEADS + h)),
                  pl.BlockSpec((s, LANES), lambda bi, h: (bi, 2 * DIFF_HEADS + h))],
        out_specs=pl.BlockSpec((s, LANES), lambda bi, h: (bi, h)),
        out_shape=jax.ShapeDtypeStruct((t, A_WIDTH), BF16),
        scratch_shapes=[pltpu.VMEM((2, tq, s), F32), pltpu.VMEM((2, tq, s), BF16),
                        pltpu.VMEM((2, tq, s), F32), pltpu.VMEM((2, tq, s), BF16)],
        compiler_params=_cparams(("parallel", "parallel")),
        name="diff_attention",
    )(lam_parts, subln_g.reshape(1, LANES), proj, proj, proj)


def _mla_attn_kernel(qn_ref, qr_ref, kn_ref, kpe_ref, v_ref, o_ref, s_ref, p_ref, *, tq):
    h = pl.program_id(1)
    nq = qn_ref.shape[0] // tq
    lane = lax.broadcasted_iota(I32, (tq, LANES), 1)
    mine = (lane >= MLA_ROPE) == (h % 2 == 1)
    for i in range(nq):
        qn = qn_ref[i * tq:(i + 1) * tq, :]
        qr = qr_ref[i * tq:(i + 1) * tq, :]
        qr = jnp.where(mine, qr, jnp.zeros_like(qr))

        qcat = jnp.concatenate([qn, qr], axis=1)

        def block_scores(j, qcat=qcat):
            rows = slice(j * tq, (j + 1) * tq)
            kcat = jnp.concatenate([kn_ref[rows, :], kpe_ref[rows, :]], axis=1)
            return lax.dot_general(qcat, kcat, _NT_DIMS, preferred_element_type=F32)

        _score_strip(s_ref.at[i % 2], i, tq, block_scores, None)
        o = _softmax_times_v(s_ref.at[i % 2], p_ref.at[i % 2], v_ref, (i + 1) * tq)
        o_ref[i * tq:(i + 1) * tq, :] = o.astype(o_ref.dtype)


def _mla_attention(qbuf, kvbuf, proj, b, s):
    t = b * s
    tq = min(ATT_TQ, s)
    kpe_block = (4 * PROJ_TN + MLA_KV_RANK) // LANES
    return pl.pallas_call(
        functools.partial(_mla_attn_kernel, tq=tq),
        grid=(b, MLA_HEADS),
        in_specs=[pl.BlockSpec((s, LANES), lambda bi, h: (bi, h)),
                  pl.BlockSpec((s, LANES), lambda bi, h: (bi, MLA_HEADS + h // 2)),
                  pl.BlockSpec((s, LANES), lambda bi, h: (bi, h)),
                  pl.BlockSpec((s, LANES), lambda bi, h: (bi, kpe_block)),
                  pl.BlockSpec((s, LANES), lambda bi, h: (bi, MLA_HEADS + h))],
        out_specs=pl.BlockSpec((s, LANES), lambda bi, h: (bi, h)),
        out_shape=jax.ShapeDtypeStruct((t, B_WIDTH), BF16),
        scratch_shapes=[pltpu.VMEM((2, tq, s), F32), pltpu.VMEM((2, tq, s), BF16)],
        compiler_params=_cparams(("parallel", "parallel")),
        name="mla_attention",
    )(qbuf, qbuf, kvbuf, proj, kvbuf)


def _moba_attn_kernel(q_ref, k_ref, v_ref, o_ref, s_ref, p_ref, kmean_ref):
    tq = MOBA_BLOCK
    nb = q_ref.shape[0] // tq
    kmean_ref[...] = jnp.zeros(kmean_ref.shape, F32)
    for j in range(nb):
        kb = k_ref[j * tq:(j + 1) * tq, :].astype(F32)
        kmean_ref[j:j + 1, :] = jnp.sum(kb, axis=0, keepdims=True) * (1.0 / tq)
    km = kmean_ref[...]
    km_hi = km.astype(BF16)
    km_lo = (km - km_hi.astype(F32)).astype(BF16)
    lane = lax.broadcasted_iota(I32, (tq, LANES), 1)
    for i in range(nb):
        q = q_ref[i * tq:(i + 1) * tq, :]
        gate = (lax.dot_general(q, km_hi, _NT_DIMS, preferred_element_type=F32)
                + lax.dot_general(q, km_lo, _NT_DIMS, preferred_element_type=F32))
        fully_past = lane < i
        g = jnp.where(fully_past, gate, -jnp.inf)
        cnt = jnp.zeros(gate.shape, F32)
        for jp in range(i):
            col = g[:, jp:jp + 1]
            ahead = (col > g) | ((col == g) & (jp < lane))
            cnt = cnt + jnp.where(ahead, 1.0, 0.0)
        kept = (cnt < MOBA_TOPK) & fully_past
        _score_strip(s_ref.at[i % 2], i, tq,
                     lambda j, q=q: lax.dot_general(q, k_ref[j * tq:(j + 1) * tq, :], _NT_DIMS,
                                                    preferred_element_type=F32),
                     lambda j, kept=kept: kept[:, j:j + 1])
        o = _softmax_times_v(s_ref.at[i % 2], p_ref.at[i % 2], v_ref, (i + 1) * tq)
        o_ref[i * tq:(i + 1) * tq, :] = o.astype(o_ref.dtype)


def _moba_attention(proj, b, s):
    assert s % MOBA_BLOCK == 0 and s // MOBA_BLOCK <= LANES
    t = b * s
    base = 5 * PROJ_TN // LANES
    return pl.pallas_call(
        _moba_attn_kernel,
        grid=(b, MOBA_HEADS),
        in_specs=[pl.BlockSpec((s, LANES), lambda bi, h: (bi, base + h)),
                  pl.BlockSpec((s, LANES), lambda bi, h: (bi, base + MOBA_HEADS + h)),
                  pl.BlockSpec((s, LANES), lambda bi, h: (bi, base + 2 * MOBA_HEADS + h))],
        out_specs=pl.BlockSpec((s, LANES), lambda bi, h: (bi, h)),
        out_shape=jax.ShapeDtypeStruct((t, C_WIDTH), BF16),
        scratch_shapes=[pltpu.VMEM((2, MOBA_BLOCK, s), F32), pltpu.VMEM((2, MOBA_BLOCK, s), BF16),
                        pltpu.VMEM((LANES, LANES), F32)],
        compiler_params=_cparams(("parallel", "parallel")),
        name="moba_attention",
    )(proj, proj, proj)


def _layer_norm_rows(z, g, b):
    mu = jnp.mean(z, axis=-1, keepdims=True)
    zc = z - mu
    var = jnp.mean(zc * zc, axis=-1, keepdims=True)
    return zc * lax.rsqrt(var + NORM_EPS) * g + b


def _pack_halves(z):
    n = z.shape[1] // 2
    lo = lax.bitcast_convert_type(z[:, :n].astype(BF16).astype(F32), U32)
    hi = lax.bitcast_convert_type(z[:, n:].astype(BF16).astype(F32), U32)
    return (lo >> 16) | hi


def _unpack_halves(u):
    lo = lax.bitcast_convert_type(u << 16, F32)
    hi = lax.bitcast_convert_type(u & jnp.uint32(0xFFFF0000), F32)
    return lo, hi


ROW_WORDS = D_MODEL // 2
ROW_SUBLANES = ROW_WORDS // LANES


def _store_row_tiles(ref, packed):
    tm = packed.shape[0]
    for a in range(ROW_SUBLANES):
        ref[pl.ds(a, tm, stride=ROW_SUBLANES), :] = packed[:, a * LANES:(a + 1) * LANES]


def _load_row_tile_chunk(ref, a, tm):
    return ref[pl.ds(a, tm, stride=ROW_SUBLANES), :]


def _merge_kernel(oa_ref, ob_ref, oc_ref, g0_ref, g1_ref, g2_ref, x_ref, wb_ref, wo_ref,
                  lng_ref, lnb_ref, xo_ref, xb_ref, xp_ref):
    ya = jnp.dot(oa_ref[...], wb_ref[:A_WIDTH, :], preferred_element_type=F32)
    yb = jnp.dot(ob_ref[...], wb_ref[A_WIDTH:A_WIDTH + B_WIDTH, :], preferred_element_type=F32)
    yc = jnp.dot(oc_ref[...], wb_ref[A_WIDTH + B_WIDTH:, :], preferred_element_type=F32)
    y = g0_ref[...].astype(F32) * ya + g1_ref[...].astype(F32) * yb + g2_ref[...].astype(F32) * yc
    mix = jnp.dot(y.astype(BF16), wo_ref[...], preferred_element_type=F32)
    z = _layer_norm_rows(DEEPNORM_ALPHA * x_ref[...] + mix, lng_ref[...], lnb_ref[...])
    xo_ref[...] = z
    xb_ref[...] = z.astype(BF16)
    _store_row_tiles(xp_ref, _pack_halves(z))


def _merge(oa, ob, oc, proj, x, w_branch, w_out, ln_g, ln_b):
    t = x.shape[0]
    tm = min(256, t)
    gate_base = 8 * PROJ_TN // D_MODEL
    row = lambda i: (i, 0)
    whole = lambda i: (0, 0)
    return pl.pallas_call(
        _merge_kernel,
        grid=(t // tm,),
        in_specs=[pl.BlockSpec((tm, A_WIDTH), row),
                  pl.BlockSpec((tm, B_WIDTH), row),
                  pl.BlockSpec((tm, C_WIDTH), row),
                  pl.BlockSpec((tm, D_MODEL), lambda i: (i, gate_base)),
                  pl.BlockSpec((tm, D_MODEL), lambda i: (i, gate_base + 1)),
                  pl.BlockSpec((tm, D_MODEL), lambda i: (i, gate_base + 2)),
                  pl.BlockSpec((tm, D_MODEL), row),
                  pl.BlockSpec((D_MODEL, D_MODEL), whole),
                  pl.BlockSpec((D_MODEL, D_MODEL), whole),
                  pl.BlockSpec((1, D_MODEL), whole),
                  pl.BlockSpec((1, D_MODEL), whole)],
        out_specs=[pl.BlockSpec((tm, D_MODEL), row),
                   pl.BlockSpec((tm, D_MODEL), row),
                   pl.BlockSpec((tm * ROW_SUBLANES, LANES), row)],
        out_shape=[jax.ShapeDtypeStruct((t, D_MODEL), F32),
                   jax.ShapeDtypeStruct((t, D_MODEL), BF16),
                   jax.ShapeDtypeStruct((t * ROW_SUBLANES, LANES), U32)],
        compiler_params=_cparams(("parallel",)),
        name="merge_outproj_ln",
    )(oa, ob, oc, proj, proj, proj, x, w_branch, w_out, ln_g.reshape(1, -1), ln_b.reshape(1, -1))


def _split_bf16(a):
    hi = a.astype(BF16)
    lo = (a - hi.astype(F32)).astype(BF16)
    return hi, lo


def _router_kernel(x_ref, wt_ref, bias_ref, upper_ref, lower_ref,
                   e8_ref, r8_ref, w8_ref, cnt_ref, carry_ref):
    i = pl.program_id(0)

    @pl.when(i == 0)
    def _():
        carry_ref[...] = jnp.zeros(carry_ref.shape, F32)

    xh, xl = _split_bf16(x_ref[...])
    wh, wl = _split_bf16(wt_ref[...])
    logits = (lax.dot_general(wh, xh, _NT_DIMS, preferred_element_type=F32)
              + lax.dot_general(wh, xl, _NT_DIMS, preferred_element_type=F32)
              + lax.dot_general(wl, xh, _NT_DIMS, preferred_element_type=F32))
    scores = 1.0 / (1.0 + jnp.exp(-logits))
    choice = scores + bias_ref[:, 0:1]
    tm = choice.shape[1]
    sub = lax.broadcasted_iota(I32, (GROUP_SIZE, tm), 0)

    group_rows = []
    for g in range(N_GROUPS):
        cg = choice[g * GROUP_SIZE:(g + 1) * GROUP_SIZE, :]
        m1 = jnp.max(cg, axis=0, keepdims=True)
        first = jnp.min(jnp.where(cg == m1, sub, GROUP_SIZE), axis=0, keepdims=True)
        m2 = jnp.max(jnp.where(sub == first, -jnp.inf, cg), axis=0, keepdims=True)
        group_rows.append(m1 + m2)
    gs = jnp.concatenate(group_rows, axis=0)
    ahead = jnp.zeros(gs.shape, F32)
    for gp in range(N_GROUPS):
        rowv = gs[gp:gp + 1, :]
        ahead = ahead + jnp.where((rowv > gs) | ((rowv == gs) & (gp < sub)), 1.0, 0.0)
    keep_group = ahead < TOPK_GROUPS
    masked = jnp.concatenate(
        [jnp.where(keep_group[g:g + 1, :], choice[g * GROUP_SIZE:(g + 1) * GROUP_SIZE, :], -jnp.inf)
         for g in range(N_GROUPS)], axis=0)

    eidx = lax.broadcasted_iota(I32, masked.shape, 0)
    ahead = jnp.zeros(masked.shape, F32)
    for ep in range(N_EXPERTS):
        rowv = masked[ep:ep + 1, :]
        ahead = ahead + jnp.where((rowv > masked) | ((rowv == masked) & (ep < eidx)), 1.0, 0.0)
    sel = ahead < TOPK
    picked = jnp.where(sel, scores, 0.0)
    gates = picked / jnp.sum(picked, axis=0, keepdims=True) * ROUTED_SCALE
    self32 = jnp.where(sel, 1.0, 0.0)
    selb = self32.astype(BF16)

    rank = jnp.dot(selb, upper_ref[...], preferred_element_type=F32) + carry_ref[:, 0:1]
    slot = jnp.dot(lower_ref[...], selb, preferred_element_type=F32)
    carry_ref[...] = carry_ref[...] + jnp.sum(self32, axis=1, keepdims=True)
    cnt_ref[...] = carry_ref[...].astype(I32)

    eidf = eidx.astype(F32)
    e_rows, r_rows, w_rows = [], [], []
    for k in range(TOPK):
        hit = sel & (slot == float(k))
        e_rows.append(jnp.sum(jnp.where(hit, eidf, 0.0), axis=0, keepdims=True))
        r_rows.append(jnp.sum(jnp.where(hit, rank, 0.0), axis=0, keepdims=True))
        w_rows.append(jnp.sum(jnp.where(hit, gates, 0.0), axis=0, keepdims=True))
    e8_ref[...] = jnp.concatenate(e_rows, axis=0).astype(I32)
    r8_ref[...] = jnp.concatenate(r_rows, axis=0).astype(I32)
    w8_ref[...] = jnp.concatenate(w_rows, axis=0)


def _router(x, router_w, router_bias):
    t = x.shape[0]
    tm = min(ROUTER_TM, t)
    upper = (jnp.arange(tm)[:, None] < jnp.arange(tm)[None, :]).astype(BF16)
    lower = (jnp.arange(N_EXPERTS)[None, :] < jnp.arange(N_EXPERTS)[:, None]).astype(BF16)
    bias = jnp.broadcast_to(router_bias.astype(F32)[:, None], (N_EXPERTS, LANES))
    whole = lambda i: (0, 0)
    col = lambda i: (0, i)
    return pl.pallas_call(
        _router_kernel,
        grid=(t // tm,),
        in_specs=[pl.BlockSpec((tm, D_MODEL), lambda i: (i, 0)),
                  pl.BlockSpec((N_EXPERTS, D_MODEL), whole),
                  pl.BlockSpec((N_EXPERTS, LANES), whole),
                  pl.BlockSpec((tm, tm), whole),
                  pl.BlockSpec((N_EXPERTS, N_EXPERTS), whole)],
        out_specs=[pl.BlockSpec((TOPK, tm), col),
                   pl.BlockSpec((TOPK, tm), col),
                   pl.BlockSpec((TOPK, tm), col),
                   pl.BlockSpec((N_EXPERTS, LANES), whole)],
        out_shape=[jax.ShapeDtypeStruct((TOPK, t), I32),
                   jax.ShapeDtypeStruct((TOPK, t), I32),
                   jax.ShapeDtypeStruct((TOPK, t), F32),
                   jax.ShapeDtypeStruct((N_EXPERTS, LANES), I32)],
        scratch_shapes=[pltpu.VMEM((N_EXPERTS, LANES), F32)],
        compiler_params=_cparams(("arbitrary",)),
        name="router",
    )(x, router_w.T, bias, upper, lower)


def _row_tile(ref, r):
    return ref.at[pl.ds(pl.multiple_of(r * ROW_SUBLANES, ROW_SUBLANES), ROW_SUBLANES)]


def _dispatch_kernel(pos_ref, x_ref, xs_ref, sem):
    tm = x_ref.shape[0] // ROW_SUBLANES

    def row_copy(t, k):
        return pltpu.make_async_copy(_row_tile(x_ref, t), _row_tile(xs_ref, pos_ref[k, t]), sem)

    def start(t, c):
        for k in range(TOPK):
            row_copy(t, k).start(priority=k % 2)
        return c
    lax.fori_loop(0, tm, start, 0)

    def wait(t, c):
        for k in range(TOPK):
            row_copy(t, k).wait()
        return c
    lax.fori_loop(0, tm, wait, 0)


def _dispatch(xp, pos8, n_rows):
    t = xp.shape[0] // ROW_SUBLANES
    tm = min(ROW_TM, t)
    return pl.pallas_call(
        _dispatch_kernel,
        grid=(t // tm,),
        in_specs=[pl.BlockSpec((TOPK, tm), lambda i: (0, i), memory_space=pltpu.SMEM),
                  pl.BlockSpec((tm * ROW_SUBLANES, LANES), lambda i: (i, 0))],
        out_specs=pl.BlockSpec(memory_space=pl.ANY),
        scratch_shapes=[pltpu.SemaphoreType.DMA(())],
        out_shape=jax.ShapeDtypeStruct((n_rows * ROW_SUBLANES, LANES), U32),
        compiler_params=_cparams(("arbitrary",)),
        name="dispatch",
    )(pos8, xp)


def _expert_kernel(te_ref, rows_ref, nv_ref, xs_ref, wg_ref, wu_ref, wd_ref, ys_ref,
                   wgb_ref, wub_ref, wdb_ref):
    i = pl.program_id(0)

    @pl.when(i < nv_ref[0])
    def _():
        @pl.when((i == 0) | (te_ref[i] != te_ref[jnp.maximum(i - 1, 0)]))
        def _():
            wgb_ref[...] = wg_ref[0].astype(BF16)
            wub_ref[...] = wu_ref[0].astype(BF16)
            wdb_ref[...] = wd_ref[0].astype(BF16)

        half = D_MODEL // 2
        tm = EXPERT_TM
        u = jnp.concatenate([_load_row_tile_chunk(xs_ref, a, tm) for a in range(ROW_SUBLANES)], axis=1)
        live = lax.broadcasted_iota(I32, (tm, 1), 0) < rows_ref[i]
        lo, hi = _unpack_halves(jnp.where(live, u, jnp.zeros_like(u)))
        xl = lo.astype(BF16)
        xh = hi.astype(BF16)
        gate = (jnp.dot(xl, wgb_ref[:half, :], preferred_element_type=F32)
                + jnp.dot(xh, wgb_ref[half:, :], preferred_element_type=F32))
        up = (jnp.dot(xl, wub_ref[:half, :], preferred_element_type=F32)
              + jnp.dot(xh, wub_ref[half:, :], preferred_element_type=F32))
        hid = (gate / (1.0 + jnp.exp(-gate))) * up
        y = jnp.dot(hid.astype(BF16), wdb_ref[...], preferred_element_type=F32)
        _store_row_tiles(ys_ref, _pack_halves(y))


def _experts(xs, tile_expert, tile_rows, n_valid, w_gate, w_up, w_down, layer):
    n_rows = xs.shape[0] // ROW_SUBLANES
    n_tiles = n_rows // EXPERT_TM
    row = lambda i, te, tr, nv: (jnp.minimum(i, nv[0] - 1), 0)
    wsel = lambda i, te, tr, nv: (layer * N_EXPERTS + te[i], 0, 0)
    grid_spec = pltpu.PrefetchScalarGridSpec(
        num_scalar_prefetch=3,
        grid=(n_tiles,),
        in_specs=[pl.BlockSpec((EXPERT_TM * ROW_SUBLANES, LANES), row),
                  pl.BlockSpec((1, D_MODEL, EXPERT_FF), wsel),
                  pl.BlockSpec((1, D_MODEL, EXPERT_FF), wsel),
                  pl.BlockSpec((1, EXPERT_FF, D_MODEL), wsel)],
        out_specs=pl.BlockSpec((EXPERT_TM * ROW_SUBLANES, LANES), row),
        scratch_shapes=[pltpu.VMEM((D_MODEL, EXPERT_FF), BF16), pltpu.VMEM((D_MODEL, EXPERT_FF), BF16),
                        pltpu.VMEM((EXPERT_FF, D_MODEL), BF16)],
    )
    return pl.pallas_call(
        _expert_kernel,
        grid_spec=grid_spec,
        out_shape=jax.ShapeDtypeStruct((n_rows * ROW_SUBLANES, LANES), U32),
        compiler_params=_cparams(("arbitrary",)),
        name="experts",
    )(tile_expert, tile_rows, n_valid, xs, w_gate, w_up, w_down)


def _combine_kernel(pos_ref, pos_next_ref, w8_ref, ys_ref, x_ref, xb_ref, wsg_ref, wsu_ref, wsd_ref,
                    lng_ref, lnb_ref, xo_ref, xob_ref, buf_ref, sem):
    i = pl.program_id(0)
    tm = x_ref.shape[0]
    slot = i % 2

    def row_copy(p_ref, s, t, k):
        return pltpu.make_async_copy(_row_tile(ys_ref, p_ref[k, t]), _row_tile(buf_ref.at[s, k], t),
                                     sem.at[s])

    def start_all(p_ref, s):
        def start(t, c):
            for k in range(TOPK):
                row_copy(p_ref, s, t, k).start(priority=k % 2)
            return c
        lax.fori_loop(0, tm, start, 0)

    @pl.when(i == 0)
    def _():
        start_all(pos_ref, 0)

    @pl.when(i + 1 < pl.num_programs(0))
    def _():
        start_all(pos_next_ref, 1 - slot)

    xb = xb_ref[...]
    sg = jnp.dot(xb, wsg_ref[...], preferred_element_type=F32)
    su = jnp.dot(xb, wsu_ref[...], preferred_element_type=F32)
    hid = (sg / (1.0 + jnp.exp(-sg))) * su
    shared = jnp.dot(hid.astype(BF16), wsd_ref[...], preferred_element_type=F32)

    def wait(t, c):
        for k in range(TOPK):
            row_copy(pos_ref, slot, t, k).wait()
        return c
    lax.fori_loop(0, tm, wait, 0)

    w8 = w8_ref[...]
    wk = [jnp.broadcast_to(w8[:, k:k + 1], (tm, LANES)) for k in range(TOPK)]
    lo_parts, hi_parts = [], []
    for a in range(ROW_SUBLANES):
        acc_lo = jnp.zeros((tm, LANES), F32)
        acc_hi = jnp.zeros((tm, LANES), F32)
        for k in range(TOPK):
            lo, hi = _unpack_halves(_load_row_tile_chunk(buf_ref.at[slot, k], a, tm))
            acc_lo = acc_lo + wk[k] * lo
            acc_hi = acc_hi + wk[k] * hi
        lo_parts.append(acc_lo)
        hi_parts.append(acc_hi)
    routed = jnp.concatenate(lo_parts + hi_parts, axis=1)
    z = _layer_norm_rows(DEEPNORM_ALPHA * x_ref[...] + (routed + shared), lng_ref[...], lnb_ref[...])
    xo_ref[...] = z
    xob_ref[...] = z.astype(BF16)


def _combine(ys, pos8, w8t, x, xb, ws_gate, ws_up, ws_down, ln_g, ln_b):
    t = x.shape[0]
    tm = min(ROW_TM, t)
    row = lambda i: (i, 0)
    whole = lambda i: (0, 0)
    n_steps = t // tm
    grid_spec = pl.GridSpec(
        grid=(n_steps,),
        in_specs=[pl.BlockSpec((TOPK, tm), lambda i: (0, i), memory_space=pltpu.SMEM),
                  pl.BlockSpec((TOPK, tm), lambda i: (0, jnp.minimum(i + 1, n_steps - 1)),
                               memory_space=pltpu.SMEM),
                  pl.BlockSpec((tm, TOPK), row),
                  pl.BlockSpec(memory_space=pl.ANY),
                  pl.BlockSpec((tm, D_MODEL), row),
                  pl.BlockSpec((tm, D_MODEL), row),
                  pl.BlockSpec((D_MODEL, EXPERT_FF), whole),
                  pl.BlockSpec((D_MODEL, EXPERT_FF), whole),
                  pl.BlockSpec((EXPERT_FF, D_MODEL), whole),
                  pl.BlockSpec((1, D_MODEL), whole),
                  pl.BlockSpec((1, D_MODEL), whole)],
        out_specs=[pl.BlockSpec((tm, D_MODEL), row),
                   pl.BlockSpec((tm, D_MODEL), row)],
        scratch_shapes=[pltpu.VMEM((2, TOPK, tm * ROW_SUBLANES, LANES), U32),
                        pltpu.SemaphoreType.DMA((2,))],
    )
    return pl.pallas_call(
        _combine_kernel,
        grid_spec=grid_spec,
        out_shape=[jax.ShapeDtypeStruct((t, D_MODEL), F32),
                   jax.ShapeDtypeStruct((t, D_MODEL), BF16)],
        compiler_params=_cparams(("arbitrary",)),
        name="combine_shared_ln",
    )(pos8, pos8, w8t, ys, x, xb, ws_gate, ws_up, ws_down, ln_g.reshape(1, -1), ln_b.reshape(1, -1))


def _in_proj_weight(w_in):
    sizes = (512, 512, 512, MLA_Q_RANK, MLA_KV_RANK, MLA_ROPE, 512, 512, 512, 3 * D_MODEL)
    offs = np.concatenate([[0], np.cumsum(sizes)])
    seg = [w_in[:, offs[n]:offs[n + 1]] for n in range(len(sizes))]
    dq, dk, dv, cq, ckv, kpe, mq, mk, mv, gates = seg
    pad = jnp.zeros((w_in.shape[0], PROJ_TN - MLA_KV_RANK - 2 * MLA_ROPE), w_in.dtype)
    return jnp.concatenate([dq, dk, dv, cq, ckv, kpe, kpe, pad, mq, mk, mv, gates], axis=1).astype(BF16)


def _uq_weight(w_uq):
    w = w_uq.reshape(MLA_Q_RANK, MLA_HEADS, MLA_NOPE + MLA_ROPE)
    return jnp.concatenate([w[:, :, :MLA_NOPE].reshape(MLA_Q_RANK, -1),
                            w[:, :, MLA_NOPE:].reshape(MLA_Q_RANK, -1)], axis=1).astype(BF16)


def _ukv_weight(w_ukv):
    w = w_ukv.reshape(MLA_KV_RANK, MLA_HEADS, MLA_NOPE + MLA_V)
    return jnp.concatenate([w[:, :, :MLA_NOPE].reshape(MLA_KV_RANK, -1),
                            w[:, :, MLA_NOPE:].reshape(MLA_KV_RANK, -1)], axis=1).astype(BF16)


def _expert_tiling(counts, n_tiles):
    tiles = (counts + EXPERT_TM - 1) // EXPERT_TM
    ends = jnp.cumsum(tiles)
    offsets = ((ends - tiles) * EXPERT_TM).astype(I32)
    tile_ids = jnp.arange(n_tiles, dtype=I32)
    tile_expert = jnp.minimum(
        jnp.sum((ends[None, :] <= tile_ids[:, None]).astype(I32), axis=1), N_EXPERTS - 1).astype(I32)
    first_tile = (ends - tiles).astype(I32)
    tile_rows = jnp.clip(counts[tile_expert] - (tile_ids - first_tile[tile_expert]) * EXPERT_TM,
                         0, EXPERT_TM).astype(I32)
    return offsets, tile_expert, tile_rows, ends[-1:].astype(I32)


def kernel(x, positions, w_in, diff_lambda_q1, diff_lambda_k1, diff_lambda_q2, diff_lambda_k2, diff_subln_g, mla_q_norm_g, mla_w_uq, mla_kv_norm_g, mla_w_ukv, w_branch, w_out, ln_mix_g, ln_mix_b, router_w, router_bias, expert_w_gate, expert_w_up, expert_w_down, shared_w_gate, shared_w_up, shared_w_down, ln_ffn_g, ln_ffn_b):
    b, s, d = x.shape
    t = b * s
    depth = w_in.shape[0]
    cos, sin = _rope_tables(positions)
    xf = x.reshape(t, d)
    xb = xf.astype(BF16)
    n_tiles = t * TOPK // EXPERT_TM + N_EXPERTS
    proj_tm = min(1024, t)

    for l in range(depth):
        gains = jnp.zeros((8, PROJ_TN), F32)
        gains = gains.at[0, :].set(mla_q_norm_g[l]).at[1, :MLA_KV_RANK].set(mla_kv_norm_g[l])
        proj = _proj(xb, 0, d, _in_proj_weight(w_in[l]), IN_TILE_MODES, IN_TILE_SCALES,
                     cos, sin, gains, proj_tm)
        qbuf = _proj(proj, 3, MLA_Q_RANK, _uq_weight(mla_w_uq[l]),
                     [MODE_PLAIN, MODE_PLAIN, MODE_ROPE_MLA], [(MLA_NOPE + MLA_ROPE) ** -0.5] * 3,
                     cos, sin, gains, proj_tm)
        kvbuf = _proj(proj, 4 * PROJ_TN // MLA_KV_RANK, MLA_KV_RANK, _ukv_weight(mla_w_ukv[l]),
                      [MODE_PLAIN] * 4, [1.0] * 4, cos, sin, gains, proj_tm)

        lambda_init = 0.8 - 0.6 * math.exp(-0.3 * l)
        lam_parts = jnp.stack([diff_lambda_q1[l], diff_lambda_k1[l],
                               diff_lambda_q2[l], diff_lambda_k2[l]]).astype(F32)
        oa = _diff_attention(proj, lam_parts, diff_subln_g[l], b, s, lambda_init)
        ob = _mla_attention(qbuf, kvbuf, proj, b, s)
        oc = _moba_attention(proj, b, s)

        x1, x1b, x1p = _merge(oa, ob, oc, proj, xf, w_branch[l].astype(BF16), w_out[l].astype(BF16),
                              ln_mix_g[l], ln_mix_b[l])

        e8, r8, w8, counts = _router(x1, router_w[l], router_bias[l])
        offsets, tile_expert, tile_rows, n_valid = _expert_tiling(counts[:, 0], n_tiles)
        expert_ids = jnp.arange(N_EXPERTS, dtype=I32)[:, None, None]
        pos8 = jnp.sum(jnp.where(e8[None] == expert_ids, offsets[:, None, None], 0), axis=0) + r8
        xs = _dispatch(x1p, pos8, n_tiles * EXPERT_TM)
        ys = _experts(xs, tile_expert, tile_rows, n_valid,
                      expert_w_gate.reshape(depth * N_EXPERTS, d, EXPERT_FF),
                      expert_w_up.reshape(depth * N_EXPERTS, d, EXPERT_FF),
                      expert_w_down.reshape(depth * N_EXPERTS, EXPERT_FF, d), l)
        xf, xb = _combine(ys, pos8, w8.T, x1, x1b, shared_w_gate[l].astype(BF16),
                          shared_w_up[l].astype(BF16), shared_w_down[l].astype(BF16),
                          ln_ffn_g[l], ln_ffn_b[l])
    return xf.reshape(b, s, d)
```

```python
import functools
import math

import jax
import jax.numpy as jnp
from jax import lax
from jax.experimental import pallas as pl
from jax.experimental.pallas import tpu as pltpu

F32 = jnp.float32
BF16 = jnp.bfloat16
I32 = jnp.int32
U32 = jnp.uint32

D_MODEL = 2048
DEPTH = 2
ROPE_THETA = 500000.0
NORM_EPS = 1e-5

DIFF_HEADS = 4
DIFF_HALF_DIM = 64
DIFF_ROT = 16
MLA_HEADS = 8
MLA_Q_RANK = 512
MLA_KV_RANK = 256
MLA_NOPE = 128
MLA_ROPE = 64
MLA_V = 128
MOBA_HEADS = 4
MOBA_HEAD_DIM = 128
MOBA_BLOCK = 256
MOBA_TOPK = 3
MOBA_ROT = 32
A_WIDTH = 512
B_WIDTH = 1024
C_WIDTH = 512

N_EXPERTS = 64
N_GROUPS = 8
GROUP_SIZE = N_EXPERTS // N_GROUPS
TOPK_GROUPS = 4
TOPK = 8
EXPERT_FF = 512
ROUTED_SCALE = 2.5

DEEPNORM_ALPHA = (2 * DEPTH) ** 0.25

LANES = 128
NEG_BIG = -1e30

PROJ_TN = 512
MODE_PLAIN, MODE_ROPE_DIFF, MODE_ROPE_MLA, MODE_ROPE_MOBA, MODE_RMS, MODE_CKV, MODE_SIGMOID = range(7)
_ROPE_OF_MODE = {MODE_ROPE_DIFF: (0, DIFF_ROT // 2), MODE_ROPE_MLA: (1, MLA_ROPE // 2),
                 MODE_ROPE_MOBA: (2, MOBA_ROT // 2)}
_ROPE_PERIOD = (DIFF_HALF_DIM, MLA_ROPE, MOBA_HEAD_DIM)
_ROPE_ROT = (DIFF_ROT, MLA_ROPE, MOBA_ROT)

IN_TILE_MODES = ([MODE_ROPE_DIFF, MODE_ROPE_DIFF, MODE_PLAIN, MODE_RMS, MODE_CKV,
                  MODE_ROPE_MOBA, MODE_ROPE_MOBA, MODE_PLAIN] + [MODE_SIGMOID] * 12)
IN_TILE_SCALES = [DIFF_HALF_DIM ** -0.5] + [1.0] * 4 + [MOBA_HEAD_DIM ** -0.5] + [1.0] * 14
IN_COLS_PADDED = PROJ_TN * len(IN_TILE_MODES)

VMEM_LIMIT = 56 * 1024 * 1024

ATT_TQ = 256
PROJ_TM = 2048
EXPERT_TM = 512
ROW_TM = 256
ROUTER_TM = 1024


def _cparams(sem):
    return pltpu.CompilerParams(dimension_semantics=sem, vmem_limit_bytes=VMEM_LIMIT)


def _rope_table_kernel(pos_ref, c_ref, cos_ref, sin_ref):
    pos = pos_ref[...].astype(F32)
    for p in range(3):
        ang = pos * c_ref[p:p + 1, :]
        cos_ref[p] = jnp.cos(ang)
        sin_ref[p] = jnp.sin(ang) * c_ref[3 + p:4 + p, :]


def _rope_tables(positions):
    t = positions.size
    lane = jnp.arange(LANES)
    rows = []
    signs = []
    for period, rot in zip(_ROPE_PERIOD, _ROPE_ROT):
        half = rot // 2
        inv_freq = ROPE_THETA ** (-jnp.arange(0, rot, 2, dtype=F32) / rot)
        cp = lane % period
        active = cp < rot
        rows.append(jnp.where(active, inv_freq[cp % half], 0.0))
        signs.append(jnp.where(active, jnp.where(cp < half, -1.0, 1.0), 0.0))
    consts = jnp.stack(rows + signs + [jnp.zeros((LANES,), F32)] * 2).astype(F32)
    tm = min(1024, t)
    cos, sin = pl.pallas_call(
        _rope_table_kernel,
        grid=(t // tm,),
        in_specs=[pl.BlockSpec((tm, 1), lambda i: (i, 0)),
                  pl.BlockSpec((8, LANES), lambda i: (0, 0))],
        out_specs=[pl.BlockSpec((3, tm, LANES), lambda i: (0, i, 0)),
                   pl.BlockSpec((3, tm, LANES), lambda i: (0, i, 0))],
        out_shape=[jax.ShapeDtypeStruct((3, t, LANES), F32)] * 2,
        compiler_params=_cparams(("parallel",)),
        name="rope_tables",
    )(positions.reshape(t, 1), consts)
    return cos, sin


def _proj_kernel(mode_ref, scale_ref, x_ref, w_ref, cos_ref, sin_ref, g_ref, o_ref):
    j = pl.program_id(1)
    mode = mode_ref[j]
    y = jnp.dot(x_ref[...], w_ref[...], preferred_element_type=F32) * scale_ref[j]
    tm, tn = y.shape
    lane = lax.broadcasted_iota(I32, (tm, LANES), 1)

    def rope_chunk(yc, table, half):
        first = (lane % _ROPE_PERIOD[table]) < half
        swapped = jnp.where(first, pltpu.roll(yc, LANES - half, 1), pltpu.roll(yc, half, 1))
        return yc * cos_ref[table] + swapped * sin_ref[table]

    @pl.when(mode == MODE_PLAIN)
    def _():
        o_ref[...] = y.astype(o_ref.dtype)

    for rope_mode, (table, half) in _ROPE_OF_MODE.items():
        @pl.when(mode == rope_mode)
        def _(table=table, half=half):
            for c in range(tn // LANES):
                sl = slice(c * LANES, (c + 1) * LANES)
                o_ref[:, sl] = rope_chunk(y[:, sl], table, half).astype(o_ref.dtype)

    @pl.when(mode == MODE_RMS)
    def _():
        r = lax.rsqrt(jnp.mean(y * y, axis=-1, keepdims=True) + NORM_EPS)
        o_ref[...] = (y * r * g_ref[0:1, :]).astype(o_ref.dtype)

    @pl.when(mode == MODE_CKV)
    def _():
        ckv = y[:, :MLA_KV_RANK]
        r = lax.rsqrt(jnp.mean(ckv * ckv, axis=-1, keepdims=True) + NORM_EPS)
        o_ref[:, :MLA_KV_RANK] = (ckv * r * g_ref[1:2, :MLA_KV_RANK]).astype(o_ref.dtype)
        sl = slice(MLA_KV_RANK, MLA_KV_RANK + LANES)
        table, half = _ROPE_OF_MODE[MODE_ROPE_MLA]
        o_ref[:, sl] = rope_chunk(y[:, sl], table, half).astype(o_ref.dtype)
        o_ref[:, MLA_KV_RANK + LANES:] = jnp.zeros((tm, tn - MLA_KV_RANK - LANES), o_ref.dtype)

    @pl.when(mode == MODE_SIGMOID)
    def _():
        o_ref[...] = (0.5 * jnp.tanh(0.5 * y) + 0.5).astype(o_ref.dtype)


def _proj(x, x_col_block, k_dim, w, modes, scales, cos, sin, gains, tm):
    t = x.shape[0]
    n = w.shape[1]
    assert n % PROJ_TN == 0 and len(modes) == len(scales) == n // PROJ_TN and w.shape[0] == k_dim
    grid_spec = pltpu.PrefetchScalarGridSpec(
        num_scalar_prefetch=1,
        grid=(t // tm, n // PROJ_TN),
        in_specs=[pl.BlockSpec(memory_space=pltpu.SMEM),
                  pl.BlockSpec((tm, k_dim), lambda i, j, m: (i, x_col_block)),
                  pl.BlockSpec((k_dim, PROJ_TN), lambda i, j, m: (0, j)),
                  pl.BlockSpec((3, tm, LANES), lambda i, j, m: (0, i, 0)),
                  pl.BlockSpec((3, tm, LANES), lambda i, j, m: (0, i, 0)),
                  pl.BlockSpec((8, PROJ_TN), lambda i, j, m: (0, 0))],
        out_specs=pl.BlockSpec((tm, PROJ_TN), lambda i, j, m: (i, j)),
    )
    return pl.pallas_call(
        _proj_kernel,
        grid_spec=grid_spec,
        out_shape=jax.ShapeDtypeStruct((t, n), BF16),
        compiler_params=_cparams(("parallel", "arbitrary")),
        name="proj",
    )(jnp.asarray(modes, I32), jnp.asarray(scales, F32), x, w, cos, sin, gains)


_NT_DIMS = (((1,), (1,)), ((), ()))


def _causal_mask(tq):
    row = lax.broadcasted_iota(I32, (tq, tq), 0)
    col = lax.broadcasted_iota(I32, (tq, tq), 1)
    return col <= row


def _score_strip(s_ref, i, tq, block_scores, block_mask):
    causal = _causal_mask(tq)
    for j in range(i + 1):
        sc = block_scores(j)
        if j == i:
            sc = jnp.where(causal, sc, NEG_BIG)
        elif block_mask is not None:
            sc = jnp.where(block_mask(j), sc, NEG_BIG)
        s_ref[:, j * tq:(j + 1) * tq] = sc


def _softmax_times_v(s_ref, p_ref, v_ref, n):
    tq = s_ref.shape[0]
    nch = n // LANES
    mrun = s_ref[:, 0:LANES]
    for c in range(1, nch):
        mrun = jnp.maximum(mrun, s_ref[:, c * LANES:(c + 1) * LANES])
    m = jnp.broadcast_to(jnp.max(mrun, axis=-1, keepdims=True), (tq, LANES))
    lrun = jnp.zeros((tq, LANES), F32)
    for c in range(nch):
        sl = slice(c * LANES, (c + 1) * LANES)
        p = jnp.exp(s_ref[:, sl] - m)
        lrun = lrun + p
        p_ref[:, sl] = p.astype(BF16)
    l = jnp.sum(lrun, axis=-1, keepdims=True)
    o = jnp.dot(p_ref[:, :n], v_ref[0:n, :], preferred_element_type=F32)
    return o / l


def _diff_attn_kernel(lam_ref, g_ref, q_ref, k_ref, v_ref, o_ref, s1_ref, p1_ref, s2_ref, p2_ref,
                      *, tq, lambda_init):
    nq = q_ref.shape[0] // tq
    lp = lam_ref[...]
    lam = (jnp.exp(jnp.sum(lp[0:1] * lp[1:2], axis=-1, keepdims=True))
           - jnp.exp(jnp.sum(lp[2:3] * lp[3:4], axis=-1, keepdims=True)) + lambda_init)
    lane = lax.broadcasted_iota(I32, (tq, LANES), 1)
    for i in range(nq):
        q = q_ref[i * tq:(i + 1) * tq, :]
        q1 = jnp.where(lane < DIFF_HALF_DIM, q, jnp.zeros_like(q))
        q2 = jnp.where(lane >= DIFF_HALF_DIM, q, jnp.zeros_like(q))
        kblk = lambda j: k_ref[j * tq:(j + 1) * tq, :]
        b = i % 2
        _score_strip(s1_ref.at[b], i, tq,
                     lambda j: lax.dot_general(q1, kblk(j), _NT_DIMS, preferred_element_type=F32), None)
        _score_strip(s2_ref.at[b], i, tq,
                     lambda j: lax.dot_general(q2, kblk(j), _NT_DIMS, preferred_element_type=F32), None)
        n = (i + 1) * tq
        o = (_softmax_times_v(s1_ref.at[b], p1_ref.at[b], v_ref, n)
             - lam * _softmax_times_v(s2_ref.at[b], p2_ref.at[b], v_ref, n))
        r = lax.rsqrt(jnp.mean(o * o, axis=-1, keepdims=True) + NORM_EPS)
        o_ref[i * tq:(i + 1) * tq, :] = (o * r * g_ref[...] * (1.0 - lambda_init)).astype(o_ref.dtype)


def _diff_attention(proj, lam_parts, subln_g, b, s, lambda_init):
    t = b * s
    tq = min(ATT_TQ, s)
    kern = functools.partial(_diff_attn_kernel, tq=tq, lambda_init=lambda_init)
    return pl.pallas_call(
        kern,
        grid=(b, DIFF_HEADS),
        in_specs=[pl.BlockSpec((4, DIFF_HALF_DIM), lambda bi, h: (0, 0)),
                  pl.BlockSpec((1, LANES), lambda bi, h: (0, 0)),
                  pl.BlockSpec((s, LANES), lambda bi, h: (bi, h)),
                  pl.BlockSpec((s, LANES), lambda bi, h: (bi, DIFF_HEADS + h)),
                  pl.BlockSpec((s, LANES), lambda bi, h: (bi, 2 * DIFF_HEADS + h))],
        out_specs=pl.BlockSpec((s, LANES), lambda bi, h: (bi, h)),
        out_shape=jax.ShapeDtypeStruct((t, A_WIDTH), BF16),
        scratch_shapes=[pltpu.VMEM((2, tq, s), F32), pltpu.VMEM((2, tq, s), BF16),
                        pltpu.VMEM((2, tq, s), F32), pltpu.VMEM((2, tq, s), BF16)],
        compiler_params=_cparams(("parallel", "parallel")),
        name="diff_attention",
    )(lam_parts, subln_g.reshape(1, LANES), proj, proj, proj)


def _mla_attn_kernel(qn_ref, qr_ref, kn_ref, kpe_ref, v_ref, o_ref, s_ref, p_ref, *, tq):
    h = pl.program_id(1)
    nq = qn_ref.shape[0] // tq
    lane = lax.broadcasted_iota(I32, (tq, LANES), 1)
    mine = (lane >= MLA_ROPE) == (h % 2 == 1)
    for i in range(nq):
        qn = qn_ref[i * tq:(i + 1) * tq, :]
        qr = qr_ref[i * tq:(i + 1) * tq, :]
        qr = jnp.where(mine, qr, jnp.zeros_like(qr))

        qcat = jnp.concatenate([qn, qr], axis=1)

        def block_scores(j, qcat=qcat):
            rows = slice(j * tq, (j + 1) * tq)
            kcat = jnp.concatenate([kn_ref[rows, :], kpe_ref[rows, :]], axis=1)
            return lax.dot_general(qcat, kcat, _NT_DIMS, preferred_element_type=F32)

        _score_strip(s_ref.at[i % 2], i, tq, block_scores, None)
        o = _softmax_times_v(s_ref.at[i % 2], p_ref.at[i % 2], v_ref, (i + 1) * tq)
        o_ref[i * tq:(i + 1) * tq, :] = o.astype(o_ref.dtype)


def _mla_attention(qbuf, kvbuf, proj, b, s):
    t = b * s
    tq = min(ATT_TQ, s)
    kpe_block = (4 * PROJ_TN + MLA_KV_RANK) // LANES
    return pl.pallas_call(
        functools.partial(_mla_attn_kernel, tq=tq),
        grid=(b, MLA_HEADS),
        in_specs=[pl.BlockSpec((s, LANES), lambda bi, h: (bi, h)),
                  pl.BlockSpec((s, LANES), lambda bi, h: (bi, MLA_HEADS + h // 2)),
                  pl.BlockSpec((s, LANES), lambda bi, h: (bi, h)),
                  pl.BlockSpec((s, LANES), lambda bi, h: (bi, kpe_block)),
                  pl.BlockSpec((s, LANES), lambda bi, h: (bi, MLA_HEADS + h))],
        out_specs=pl.BlockSpec((s, LANES), lambda bi, h: (bi, h)),
        out_shape=jax.ShapeDtypeStruct((t, B_WIDTH), BF16),
        scratch_shapes=[pltpu.VMEM((2, tq, s), F32), pltpu.VMEM((2, tq, s), BF16)],
        compiler_params=_cparams(("parallel", "parallel")),
        name="mla_attention",
    )(qbuf, qbuf, kvbuf, proj, kvbuf)


def _moba_attn_kernel(q_ref, k_ref, v_ref, o_ref, s_ref, p_ref, kmean_ref):
    tq = MOBA_BLOCK
    nb = q_ref.shape[0] // tq
    kmean_ref[...] = jnp.zeros(kmean_ref.shape, F32)
    for j in range(nb):
        kb = k_ref[j * tq:(j + 1) * tq, :].astype(F32)
        kmean_ref[j:j + 1, :] = jnp.sum(kb, axis=0, keepdims=True) * (1.0 / tq)
    km = kmean_ref[...]
    km_hi = km.astype(BF16)
    km_lo = (km - km_hi.astype(F32)).astype(BF16)
    lane = lax.broadcasted_iota(I32, (tq, LANES), 1)
    for i in range(nb):
        q = q_ref[i * tq:(i + 1) * tq, :]
        gate = (lax.dot_general(q, km_hi, _NT_DIMS, preferred_element_type=F32)
                + lax.dot_general(q, km_lo, _NT_DIMS, preferred_element_type=F32))
        fully_past = lane < i
        g = jnp.where(fully_past, gate, -jnp.inf)
        cnt = jnp.zeros(gate.shape, F32)
        for jp in range(i):
            col = g[:, jp:jp + 1]
            ahead = (col > g) | ((col == g) & (jp < lane))
            cnt = cnt + jnp.where(ahead, 1.0, 0.0)
        kept = (cnt < MOBA_TOPK) & fully_past
        _score_strip(s_ref.at[i % 2], i, tq,
                     lambda j, q=q: lax.dot_general(q, k_ref[j * tq:(j + 1) * tq, :], _NT_DIMS,
                                                    preferred_element_type=F32),
                     lambda j, kept=kept: kept[:, j:j + 1])
        o = _softmax_times_v(s_ref.at[i % 2], p_ref.at[i % 2], v_ref, (i + 1) * tq)
        o_ref[i * tq:(i + 1) * tq, :] = o.astype(o_ref.dtype)


def _moba_attention(proj, b, s):
    assert s % MOBA_BLOCK == 0 and s // MOBA_BLOCK <= LANES
    t = b * s
    base = 5 * PROJ_TN // LANES
    return pl.pallas_call(
        _moba_attn_kernel,
        grid=(b, MOBA_HEADS),
        in_specs=[pl.BlockSpec((s, LANES), lambda bi, h: (bi, base + h)),
                  pl.BlockSpec((s, LANES), lambda bi, h: (bi, base + MOBA_HEADS + h)),
                  pl.BlockSpec((s, LANES), lambda bi, h: (bi, base + 2 * MOBA_HEADS + h))],
        out_specs=pl.BlockSpec((s, LANES), lambda bi, h: (bi, h)),
        out_shape=jax.ShapeDtypeStruct((t, C_WIDTH), BF16),
        scratch_shapes=[pltpu.VMEM((2, MOBA_BLOCK, s), F32), pltpu.VMEM((2, MOBA_BLOCK, s), BF16),
                        pltpu.VMEM((LANES, LANES), F32)],
        compiler_params=_cparams(("parallel", "parallel")),
        name="moba_attention",
    )(proj, proj, proj)


def _layer_norm_rows(z, g, b):
    mu = jnp.mean(z, axis=-1, keepdims=True)
    zc = z - mu
    var = jnp.mean(zc * zc, axis=-1, keepdims=True)
    return zc * lax.rsqrt(var + NORM_EPS) * g + b


def _pack_halves(z):
    n = z.shape[1] // 2
    lo = lax.bitcast_convert_type(z[:, :n].astype(BF16).astype(F32), U32)
    hi = lax.bitcast_convert_type(z[:, n:].astype(BF16).astype(F32), U32)
    return (lo >> 16) | hi


def _unpack_halves(u):
    lo = lax.bitcast_convert_type(u << 16, F32)
    hi = lax.bitcast_convert_type(u & jnp.uint32(0xFFFF0000), F32)
    return lo, hi


ROW_WORDS = D_MODEL // 2
ROW_SUBLANES = ROW_WORDS // LANES


def _store_row_tiles(ref, packed):
    tm = packed.shape[0]
    for a in range(ROW_SUBLANES):
        ref[pl.ds(a, tm, stride=ROW_SUBLANES), :] = packed[:, a * LANES:(a + 1) * LANES]


def _load_row_tile_chunk(ref, a, tm):
    return ref[pl.ds(a, tm, stride=ROW_SUBLANES), :]


def _merge_kernel(oa_ref, ob_ref, oc_ref, g0_ref, g1_ref, g2_ref, x_ref, wb_ref, wo_ref,
                  lng_ref, lnb_ref, xo_ref, xb_ref, xp_ref):
    ya = jnp.dot(oa_ref[...], wb_ref[:A_WIDTH, :], preferred_element_type=F32)
    yb = jnp.dot(ob_ref[...], wb_ref[A_WIDTH:A_WIDTH + B_WIDTH, :], preferred_element_type=F32)
    yc = jnp.dot(oc_ref[...], wb_ref[A_WIDTH + B_WIDTH:, :], preferred_element_type=F32)
    y = g0_ref[...].astype(F32) * ya + g1_ref[...].astype(F32) * yb + g2_ref[...].astype(F32) * yc
    mix = jnp.dot(y.astype(BF16), wo_ref[...], preferred_element_type=F32)
    z = _layer_norm_rows(DEEPNORM_ALPHA * x_ref[...] + mix, lng_ref[...], lnb_ref[...])
    xo_ref[...] = z
    xb_ref[...] = z.astype(BF16)
    _store_row_tiles(xp_ref, _pack_halves(z))


def _merge(oa, ob, oc, proj, x, w_branch, w_out, ln_g, ln_b):
    t = x.shape[0]
    tm = min(256, t)
    gate_base = 8 * PROJ_TN // D_MODEL
    row = lambda i: (i, 0)
    whole = lambda i: (0, 0)
    return pl.pallas_call(
        _merge_kernel,
        grid=(t // tm,),
        in_specs=[pl.BlockSpec((tm, A_WIDTH), row),
                  pl.BlockSpec((tm, B_WIDTH), row),
                  pl.BlockSpec((tm, C_WIDTH), row),
                  pl.BlockSpec((tm, D_MODEL), lambda i: (i, gate_base)),
                  pl.BlockSpec((tm, D_MODEL), lambda i: (i, gate_base + 1)),
                  pl.BlockSpec((tm, D_MODEL), lambda i: (i, gate_base + 2)),
                  pl.BlockSpec((tm, D_MODEL), row),
                  pl.BlockSpec((D_MODEL, D_MODEL), whole),
                  pl.BlockSpec((D_MODEL, D_MODEL), whole),
                  pl.BlockSpec((1, D_MODEL), whole),
                  pl.BlockSpec((1, D_MODEL), whole)],
        out_specs=[pl.BlockSpec((tm, D_MODEL), row),
                   pl.BlockSpec((tm, D_MODEL), row),
                   pl.BlockSpec((tm * ROW_SUBLANES, LANES), row)],
        out_shape=[jax.ShapeDtypeStruct((t, D_MODEL), F32),
                   jax.ShapeDtypeStruct((t, D_MODEL), BF16),
                   jax.ShapeDtypeStruct((t * ROW_SUBLANES, LANES), U32)],
        compiler_params=_cparams(("parallel",)),
        name="merge_outproj_ln",
    )(oa, ob, oc, proj, proj, proj, x, w_branch, w_out, ln_g.reshape(1, -1), ln_b.reshape(1, -1))


def _split_bf16(a):
    hi = a.astype(BF16)
    lo = (a - hi.astype(F32)).astype(BF16)
    return hi, lo


def _router_kernel(x_ref, wt_ref, bias_ref, upper_ref, lower_ref,
                   e8_ref, r8_ref, w8_ref, cnt_ref, carry_ref):
    i = pl.program_id(0)

    @pl.when(i == 0)
    def _():
        carry_ref[...] = jnp.zeros(carry_ref.shape, F32)

    xh, xl = _split_bf16(x_ref[...])
    wh, wl = _split_bf16(wt_ref[...])
    logits = (lax.dot_general(wh, xh, _NT_DIMS, preferred_element_type=F32)
              + lax.dot_general(wh, xl, _NT_DIMS, preferred_element_type=F32)
              + lax.dot_general(wl, xh, _NT_DIMS, preferred_element_type=F32))
    scores = 1.0 / (1.0 + jnp.exp(-logits))
    choice = scores + bias_ref[:, 0:1]
    tm = choice.shape[1]
    sub = lax.broadcasted_iota(I32, (GROUP_SIZE, tm), 0)

    group_rows = []
    for g in range(N_GROUPS):
        cg = choice[g * GROUP_SIZE:(g + 1) * GROUP_SIZE, :]
        m1 = jnp.max(cg, axis=0, keepdims=True)
        first = jnp.min(jnp.where(cg == m1, sub, GROUP_SIZE), axis=0, keepdims=True)
        m2 = jnp.max(jnp.where(sub == first, -jnp.inf, cg), axis=0, keepdims=True)
        group_rows.append(m1 + m2)
    gs = jnp.concatenate(group_rows, axis=0)
    ahead = jnp.zeros(gs.shape, F32)
    for gp in range(N_GROUPS):
        rowv = gs[gp:gp + 1, :]
        ahead = ahead + jnp.where((rowv > gs) | ((rowv == gs) & (gp < sub)), 1.0, 0.0)
    keep_group = ahead < TOPK_GROUPS
    masked = jnp.concatenate(
        [jnp.where(keep_group[g:g + 1, :], choice[g * GROUP_SIZE:(g + 1) * GROUP_SIZE, :], -jnp.inf)
         for g in range(N_GROUPS)], axis=0)

    eidx = lax.broadcasted_iota(I32, masked.shape, 0)
    ahead = jnp.zeros(masked.shape, F32)
    for ep in range(N_EXPERTS):
        rowv = masked[ep:ep + 1, :]
        ahead = ahead + jnp.where((rowv > masked) | ((rowv == masked) & (ep < eidx)), 1.0, 0.0)
    sel = ahead < TOPK
    picked = jnp.where(sel, scores, 0.0)
    gates = picked / jnp.sum(picked, axis=0, keepdims=True) * ROUTED_SCALE
    self32 = jnp.where(sel, 1.0, 0.0)
    selb = self32.astype(BF16)

    rank = jnp.dot(selb, upper_ref[...], preferred_element_type=F32) + carry_ref[:, 0:1]
    slot = jnp.dot(lower_ref[...], selb, preferred_element_type=F32)
    carry_ref[...] = carry_ref[...] + jnp.sum(self32, axis=1, keepdims=True)
    cnt_ref[...] = carry_ref[...].astype(I32)

    eidf = eidx.astype(F32)
    e_rows, r_rows, w_rows = [], [], []
    for k in range(TOPK):
        hit = sel & (slot == float(k))
        e_rows.append(jnp.sum(jnp.where(hit, eidf, 0.0), axis=0, keepdims=True))
        r_rows.append(jnp.sum(jnp.where(hit, rank, 0.0), axis=0, keepdims=True))
        w_rows.append(jnp.sum(jnp.where(hit, gates, 0.0), axis=0, keepdims=True))
    e8_ref[...] = jnp.concatenate(e_rows, axis=0).astype(I32)
    r8_ref[...] = jnp.concatenate(r_rows, axis=0).astype(I32)
    w8_ref[...] = jnp.concatenate(w_rows, axis=0)


def _router(x, router_w, router_bias):
    t = x.shape[0]
    tm = min(ROUTER_TM, t)
    upper = (jnp.arange(tm)[:, None] < jnp.arange(tm)[None, :]).astype(BF16)
    lower = (jnp.arange(N_EXPERTS)[None, :] < jnp.arange(N_EXPERTS)[:, None]).astype(BF16)
    bias = jnp.broadcast_to(router_bias.astype(F32)[:, None], (N_EXPERTS, LANES))
    whole = lambda i: (0, 0)
    col = lambda i: (0, i)
    return pl.pallas_call(
        _router_kernel,
        grid=(t // tm,),
        in_specs=[pl.BlockSpec((tm, D_MODEL), lambda i: (i, 0)),
                  pl.BlockSpec((N_EXPERTS, D_MODEL), whole),
                  pl.BlockSpec((N_EXPERTS, LANES), whole),
                  pl.BlockSpec((tm, tm), whole),
                  pl.BlockSpec((N_EXPERTS, N_EXPERTS), whole)],
        out_specs=[pl.BlockSpec((TOPK, tm), col),
                   pl.BlockSpec((TOPK, tm), col),
                   pl.BlockSpec((TOPK, tm), col),
                   pl.BlockSpec((N_EXPERTS, LANES), whole)],
        out_shape=[jax.ShapeDtypeStruct((TOPK, t), I32),
                   jax.ShapeDtypeStruct((TOPK, t), I32),
                   jax.ShapeDtypeStruct((TOPK, t), F32),
                   jax.ShapeDtypeStruct((N_EXPERTS, LANES), I32)],
        scratch_shapes=[pltpu.VMEM((N_EXPERTS, LANES), F32)],
        compiler_params=_cparams(("arbitrary",)),
        name="router",
    )(x, router_w.T, bias, upper, lower)


def _row_tile(ref, r):
    return ref.at[pl.ds(pl.multiple_of(r * ROW_SUBLANES, ROW_SUBLANES), ROW_SUBLANES)]


def _dispatch_kernel(pos_ref, x_ref, xs_ref, sem):
    tm = x_ref.shape[0] // ROW_SUBLANES

    def row_copy(t, k):
        return pltpu.make_async_copy(_row_tile(x_ref, t), _row_tile(xs_ref, pos_ref[k, t]), sem)

    def start(t, c):
        for k in range(TOPK):
            row_copy(t, k).start(priority=k % 2)
        return c
    lax.fori_loop(0, tm, start, 0)

    def wait(t, c):
        for k in range(TOPK):
            row_copy(t, k).wait()
        return c
    lax.fori_loop(0, tm, wait, 0)


def _dispatch(xp, pos8, n_rows):
    t = xp.shape[0] // ROW_SUBLANES
    tm = min(ROW_TM, t)
    return pl.pallas_call(
        _dispatch_kernel,
        grid=(t // tm,),
        in_specs=[pl.BlockSpec((TOPK, tm), lambda i: (0, i), memory_space=pltpu.SMEM),
                  pl.BlockSpec((tm * ROW_SUBLANES, LANES), lambda i: (i, 0))],
        out_specs=pl.BlockSpec(memory_space=pl.ANY),
        scratch_shapes=[pltpu.SemaphoreType.DMA(())],
        out_shape=jax.ShapeDtypeStruct((n_rows * ROW_SUBLANES, LANES), U32),
        compiler_params=_cparams(("arbitrary",)),
        name="dispatch",
    )(pos8, xp)


def _expert_kernel(te_ref, rows_ref, nv_ref, xs_ref, wg_ref, wu_ref, wd_ref, ys_ref,
                   wgb_ref, wub_ref, wdb_ref):
    i = pl.program_id(0)

    @pl.when(i < nv_ref[0])
    def _():
        @pl.when((i == 0) | (te_ref[i] != te_ref[jnp.maximum(i - 1, 0)]))
        def _():
            wgb_ref[...] = wg_ref[0].astype(BF16)
            wub_ref[...] = wu_ref[0].astype(BF16)
            wdb_ref[...] = wd_ref[0].astype(BF16)

        half = D_MODEL // 2
        tm = EXPERT_TM
        u = jnp.concatenate([_load_row_tile_chunk(xs_ref, a, tm) for a in range(ROW_SUBLANES)], axis=1)
        live = lax.broadcasted_iota(I32, (tm, 1), 0) < rows_ref[i]
        lo, hi = _unpack_halves(jnp.where(live, u, jnp.zeros_like(u)))
        xl = lo.astype(BF16)
        xh = hi.astype(BF16)
        gate = (jnp.dot(xl, wgb_ref[:half, :], preferred_element_type=F32)
                + jnp.dot(xh, wgb_ref[half:, :], preferred_element_type=F32))
        up = (jnp.dot(xl, wub_ref[:half, :], preferred_element_type=F32)
              + jnp.dot(xh, wub_ref[half:, :], preferred_element_type=F32))
        hid = (gate / (1.0 + jnp.exp(-gate))) * up
        y = jnp.dot(hid.astype(BF16), wdb_ref[...], preferred_element_type=F32)
        _store_row_tiles(ys_ref, _pack_halves(y))


def _experts(xs, tile_expert, tile_rows, n_valid, w_gate, w_up, w_down, layer):
    n_rows = xs.shape[0] // ROW_SUBLANES
    n_tiles = n_rows // EXPERT_TM
    row = lambda i, te, tr, nv: (jnp.minimum(i, nv[0] - 1), 0)
    wsel = lambda i, te, tr, nv: (layer * N_EXPERTS + te[i], 0, 0)
    grid_spec = pltpu.PrefetchScalarGridSpec(
        num_scalar_prefetch=3,
        grid=(n_tiles,),
        in_specs=[pl.BlockSpec((EXPERT_TM * ROW_SUBLANES, LANES), row),
                  pl.BlockSpec((1, D_MODEL, EXPERT_FF), wsel),
                  pl.BlockSpec((1, D_MODEL, EXPERT_FF), wsel),
                  pl.BlockSpec((1, EXPERT_FF, D_MODEL), wsel)],
        out_specs=pl.BlockSpec((EXPERT_TM * ROW_SUBLANES, LANES), row),
        scratch_shapes=[pltpu.VMEM((D_MODEL, EXPERT_FF), BF16), pltpu.VMEM((D_MODEL, EXPERT_FF), BF16),
                        pltpu.VMEM((EXPERT_FF, D_MODEL), BF16)],
    )
    return pl.pallas_call(
        _expert_kernel,
        grid_spec=grid_spec,
        out_shape=jax.ShapeDtypeStruct((n_rows * ROW_SUBLANES, LANES), U32),
        compiler_params=_cparams(("arbitrary",)),
        name="experts",
    )(tile_expert, tile_rows, n_valid, xs, w_gate, w_up, w_down)


def _combine_kernel(pos_ref, pos_next_ref, w8_ref, ys_ref, x_ref, xb_ref, wsg_ref, wsu_ref, wsd_ref,
                    lng_ref, lnb_ref, xo_ref, xob_ref, buf_ref, sem):
    i = pl.program_id(0)
    tm = x_ref.shape[0]
    slot = i % 2

    def row_copy(p_ref, s, t, k):
        return pltpu.make_async_copy(_row_tile(ys_ref, p_ref[k, t]), _row_tile(buf_ref.at[s, k], t),
                                     sem.at[s])

    def start_all(p_ref, s):
        def start(t, c):
            for k in range(TOPK):
                row_copy(p_ref, s, t, k).start(priority=k % 2)
            return c
        lax.fori_loop(0, tm, start, 0)

    @pl.when(i == 0)
    def _():
        start_all(pos_ref, 0)

    @pl.when(i + 1 < pl.num_programs(0))
    def _():
        start_all(pos_next_ref, 1 - slot)

    xb = xb_ref[...]
    sg = jnp.dot(xb, wsg_ref[...], preferred_element_type=F32)
    su = jnp.dot(xb, wsu_ref[...], preferred_element_type=F32)
    hid = (sg / (1.0 + jnp.exp(-sg))) * su
    shared = jnp.dot(hid.astype(BF16), wsd_ref[...], preferred_element_type=F32)

    def wait(t, c):
        for k in range(TOPK):
            row_copy(pos_ref, slot, t, k).wait()
        return c
    lax.fori_loop(0, tm, wait, 0)

    w8 = w8_ref[...]
    wk = [jnp.broadcast_to(w8[:, k:k + 1], (tm, LANES)) for k in range(TOPK)]
    lo_parts, hi_parts = [], []
    for a in range(ROW_SUBLANES):
        acc_lo = jnp.zeros((tm, LANES), F32)
        acc_hi = jnp.zeros((tm, LANES), F32)
        for k in range(TOPK):
            lo, hi = _unpack_halves(_load_row_tile_chunk(buf_ref.at[slot, k], a, tm))
            acc_lo = acc_lo + wk[k] * lo
            acc_hi = acc_hi + wk[k] * hi
        lo_parts.append(acc_lo)
        hi_parts.append(acc_hi)
    routed = jnp.concatenate(lo_parts + hi_parts, axis=1)
    z = _layer_norm_rows(DEEPNORM_ALPHA * x_ref[...] + (routed + shared), lng_ref[...], lnb_ref[...])
    xo_ref[...] = z
    xob_ref[...] = z.astype(BF16)


def _combine(ys, pos8, w8t, x, xb, ws_gate, ws_up, ws_down, ln_g, ln_b):
    t = x.shape[0]
    tm = min(ROW_TM, t)
    row = lambda i: (i, 0)
    whole = lambda i: (0, 0)
    n_steps = t // tm
    grid_spec = pl.GridSpec(
        grid=(n_steps,),
        in_specs=[pl.BlockSpec((TOPK, tm), lambda i: (0, i), memory_space=pltpu.SMEM),
                  pl.BlockSpec((TOPK, tm), lambda i: (0, jnp.minimum(i + 1, n_steps - 1)),
                               memory_space=pltpu.SMEM),
                  pl.BlockSpec((tm, TOPK), row),
                  pl.BlockSpec(memory_space=pl.ANY),
                  pl.BlockSpec((tm, D_MODEL), row),
                  pl.BlockSpec((tm, D_MODEL), row),
                  pl.BlockSpec((D_MODEL, EXPERT_FF), whole),
                  pl.BlockSpec((D_MODEL, EXPERT_FF), whole),
                  pl.BlockSpec((EXPERT_FF, D_MODEL), whole),
                  pl.BlockSpec((1, D_MODEL), whole),
                  pl.BlockSpec((1, D_MODEL), whole)],
        out_specs=[pl.BlockSpec((tm, D_MODEL), row),
                   pl.BlockSpec((tm, D_MODEL), row)],
        scratch_shapes=[pltpu.VMEM((2, TOPK, tm * ROW_SUBLANES, LANES), U32),
                        pltpu.SemaphoreType.DMA((2,))],
    )
    return pl.pallas_call(
        _combine_kernel,
        grid_spec=grid_spec,
        out_shape=[jax.ShapeDtypeStruct((t, D_MODEL), F32),
                   jax.ShapeDtypeStruct((t, D_MODEL), BF16)],
        compiler_params=_cparams(("arbitrary",)),
        name="combine_shared_ln",
    )(pos8, pos8, w8t, ys, x, xb, ws_gate, ws_up, ws_down, ln_g.reshape(1, -1), ln_b.reshape(1, -1))


def _in_proj_weight(w_in):
    w = w_in.astype(BF16)
    head = 3 * 512 + MLA_Q_RANK + MLA_KV_RANK
    kpe = w[:, head:head + MLA_ROPE]
    pad = jnp.zeros((w.shape[0], PROJ_TN - MLA_KV_RANK - 2 * MLA_ROPE), BF16)
    return jnp.concatenate([w[:, :head], kpe, kpe, pad, w[:, head + MLA_ROPE:]], axis=1)


def _uq_weight(w_uq):
    w = w_uq.reshape(MLA_Q_RANK, MLA_HEADS, MLA_NOPE + MLA_ROPE)
    return jnp.concatenate([w[:, :, :MLA_NOPE].reshape(MLA_Q_RANK, -1),
                            w[:, :, MLA_NOPE:].reshape(MLA_Q_RANK, -1)], axis=1).astype(BF16)


def _ukv_weight(w_ukv):
    w = w_ukv.reshape(MLA_KV_RANK, MLA_HEADS, MLA_NOPE + MLA_V)
    return jnp.concatenate([w[:, :, :MLA_NOPE].reshape(MLA_KV_RANK, -1),
                            w[:, :, MLA_NOPE:].reshape(MLA_KV_RANK, -1)], axis=1).astype(BF16)


def _expert_tiling(counts, n_tiles):
    tiles = (counts + EXPERT_TM - 1) // EXPERT_TM
    ends = jnp.cumsum(tiles)
    offsets = ((ends - tiles) * EXPERT_TM).astype(I32)
    tile_ids = jnp.arange(n_tiles, dtype=I32)
    tile_expert = jnp.minimum(
        jnp.sum((ends[None, :] <= tile_ids[:, None]).astype(I32), axis=1), N_EXPERTS - 1).astype(I32)
    first_tile = (ends - tiles).astype(I32)
    tile_rows = jnp.clip(counts[tile_expert] - (tile_ids - first_tile[tile_expert]) * EXPERT_TM,
                         0, EXPERT_TM).astype(I32)
    return offsets, tile_expert, tile_rows, ends[-1:].astype(I32)


def kernel(x, positions, w_in, diff_lambda_q1, diff_lambda_k1, diff_lambda_q2, diff_lambda_k2, diff_subln_g, mla_q_norm_g, mla_w_uq, mla_kv_norm_g, mla_w_ukv, w_branch, w_out, ln_mix_g, ln_mix_b, router_w, router_bias, expert_w_gate, expert_w_up, expert_w_down, shared_w_gate, shared_w_up, shared_w_down, ln_ffn_g, ln_ffn_b):
    b, s, d = x.shape
    t = b * s
    depth = w_in.shape[0]
    cos, sin = _rope_tables(positions)
    xf = x.reshape(t, d)
    xb = xf.astype(BF16)
    n_tiles = t * TOPK // EXPERT_TM + N_EXPERTS
    proj_tm = min(PROJ_TM, t)

    for l in range(depth):
        gains = jnp.zeros((8, PROJ_TN), F32)
        gains = gains.at[0, :].set(mla_q_norm_g[l]).at[1, :MLA_KV_RANK].set(mla_kv_norm_g[l])
        proj = _proj(xb, 0, d, _in_proj_weight(w_in[l]), IN_TILE_MODES, IN_TILE_SCALES,
                     cos, sin, gains, proj_tm)
        qbuf = _proj(proj, 3, MLA_Q_RANK, _uq_weight(mla_w_uq[l]),
                     [MODE_PLAIN, MODE_PLAIN, MODE_ROPE_MLA], [(MLA_NOPE + MLA_ROPE) ** -0.5] * 3,
                     cos, sin, gains, proj_tm)
        kvbuf = _proj(proj, 4 * PROJ_TN // MLA_KV_RANK, MLA_KV_RANK, _ukv_weight(mla_w_ukv[l]),
                      [MODE_PLAIN] * 4, [1.0] * 4, cos, sin, gains, proj_tm)

        lambda_init = 0.8 - 0.6 * math.exp(-0.3 * l)
        lam_parts = jnp.stack([diff_lambda_q1[l], diff_lambda_k1[l],
                               diff_lambda_q2[l], diff_lambda_k2[l]]).astype(F32)
        oa = _diff_attention(proj, lam_parts, diff_subln_g[l], b, s, lambda_init)
        ob = _mla_attention(qbuf, kvbuf, proj, b, s)
        oc = _moba_attention(proj, b, s)

        x1, x1b, x1p = _merge(oa, ob, oc, proj, xf, w_branch[l].astype(BF16), w_out[l].astype(BF16),
                              ln_mix_g[l], ln_mix_b[l])

        e8, r8, w8, counts = _router(x1, router_w[l], router_bias[l])
        offsets, tile_expert, tile_rows, n_valid = _expert_tiling(counts[:, 0], n_tiles)
        expert_ids = jnp.arange(N_EXPERTS, dtype=I32)[:, None, None]
        pos8 = jnp.sum(jnp.where(e8[None] == expert_ids, offsets[:, None, None], 0), axis=0) + r8
        xs = _dispatch(x1p, pos8, n_tiles * EXPERT_TM)
        ys = _experts(xs, tile_expert, tile_rows, n_valid,
                      expert_w_gate.reshape(depth * N_EXPERTS, d, EXPERT_FF),
                      expert_w_up.reshape(depth * N_EXPERTS, d, EXPERT_FF),
                      expert_w_down.reshape(depth * N_EXPERTS, EXPERT_FF, d), l)
        xf, xb = _combine(ys, pos8, w8.T, x1, x1b, shared_w_gate[l].astype(BF16),
                          shared_w_up[l].astype(BF16), shared_w_down[l].astype(BF16),
                          ln_ffn_g[l], ln_ffn_b[l])
    return xf.reshape(b, s, d)
```

```python
import functools
import math

import jax
import jax.numpy as jnp
from jax import lax
from jax.experimental import pallas as pl
from jax.experimental.pallas import tpu as pltpu

F32 = jnp.float32
BF16 = jnp.bfloat16
I32 = jnp.int32
U32 = jnp.uint32

D_MODEL = 2048
DEPTH = 2
ROPE_THETA = 500000.0
NORM_EPS = 1e-5

DIFF_HEADS = 4
DIFF_HALF_DIM = 64
DIFF_ROT = 16
MLA_HEADS = 8
MLA_Q_RANK = 512
MLA_KV_RANK = 256
MLA_NOPE = 128
MLA_ROPE = 64
MLA_V = 128
MOBA_HEADS = 4
MOBA_HEAD_DIM = 128
MOBA_BLOCK = 256
MOBA_TOPK = 3
MOBA_ROT = 32
A_WIDTH = 512
B_WIDTH = 1024
C_WIDTH = 512

N_EXPERTS = 64
N_GROUPS = 8
GROUP_SIZE = N_EXPERTS // N_GROUPS
TOPK_GROUPS = 4
TOPK = 8
EXPERT_FF = 512
ROUTED_SCALE = 2.5

DEEPNORM_ALPHA = (2 * DEPTH) ** 0.25

LANES = 128
NEG_BIG = -1e30

PROJ_TN = 512
MODE_PLAIN, MODE_ROPE_DIFF, MODE_ROPE_MLA, MODE_ROPE_MOBA, MODE_RMS, MODE_CKV, MODE_SIGMOID = range(7)
_ROPE_OF_MODE = {MODE_ROPE_DIFF: (0, DIFF_ROT // 2), MODE_ROPE_MLA: (1, MLA_ROPE // 2),
                 MODE_ROPE_MOBA: (2, MOBA_ROT // 2)}
_ROPE_PERIOD = (DIFF_HALF_DIM, MLA_ROPE, MOBA_HEAD_DIM)
_ROPE_ROT = (DIFF_ROT, MLA_ROPE, MOBA_ROT)

IN_TILE_MODES = ([MODE_ROPE_DIFF, MODE_ROPE_DIFF, MODE_PLAIN, MODE_RMS, MODE_CKV,
                  MODE_ROPE_MOBA, MODE_ROPE_MOBA, MODE_PLAIN] + [MODE_SIGMOID] * 12)
IN_TILE_SCALES = [DIFF_HALF_DIM ** -0.5] + [1.0] * 4 + [MOBA_HEAD_DIM ** -0.5] + [1.0] * 14
IN_COLS_PADDED = PROJ_TN * len(IN_TILE_MODES)

VMEM_LIMIT = 56 * 1024 * 1024

ATT_TQ = 256
PROJ_TM = 2048
MERGE_TM = 256
EXPERT_TM = 512
ROW_TM = 256
ROUTER_TM = 1024


def _cparams(sem):
    return pltpu.CompilerParams(dimension_semantics=sem, vmem_limit_bytes=VMEM_LIMIT)


def _rope_table_kernel(pos_ref, c_ref, cos_ref, sin_ref):
    pos = pos_ref[...].astype(F32)
    for p in range(3):
        ang = pos * c_ref[p:p + 1, :]
        cos_ref[p] = jnp.cos(ang)
        sin_ref[p] = jnp.sin(ang) * c_ref[3 + p:4 + p, :]


def _rope_tables(positions):
    t = positions.size
    lane = jnp.arange(LANES)
    rows = []
    signs = []
    for period, rot in zip(_ROPE_PERIOD, _ROPE_ROT):
        half = rot // 2
        inv_freq = ROPE_THETA ** (-jnp.arange(0, rot, 2, dtype=F32) / rot)
        cp = lane % period
        active = cp < rot
        rows.append(jnp.where(active, inv_freq[cp % half], 0.0))
        signs.append(jnp.where(active, jnp.where(cp < half, -1.0, 1.0), 0.0))
    consts = jnp.stack(rows + signs + [jnp.zeros((LANES,), F32)] * 2).astype(F32)
    tm = min(1024, t)
    cos, sin = pl.pallas_call(
        _rope_table_kernel,
        grid=(t // tm,),
        in_specs=[pl.BlockSpec((tm, 1), lambda i: (i, 0)),
                  pl.BlockSpec((8, LANES), lambda i: (0, 0))],
        out_specs=[pl.BlockSpec((3, tm, LANES), lambda i: (0, i, 0)),
                   pl.BlockSpec((3, tm, LANES), lambda i: (0, i, 0))],
        out_shape=[jax.ShapeDtypeStruct((3, t, LANES), F32)] * 2,
        compiler_params=_cparams(("parallel",)),
        name="rope_tables",
    )(positions.reshape(t, 1), consts)
    return cos, sin


def _proj_kernel(mode_ref, scale_ref, x_ref, w_ref, cos_ref, sin_ref, g_ref, o_ref):
    j = pl.program_id(1)
    mode = mode_ref[j]
    y = jnp.dot(x_ref[...], w_ref[...], preferred_element_type=F32) * scale_ref[j]
    tm, tn = y.shape
    lane = lax.broadcasted_iota(I32, (tm, LANES), 1)

    def rope_chunk(yc, table, half):
        first = (lane % _ROPE_PERIOD[table]) < half
        swapped = jnp.where(first, pltpu.roll(yc, LANES - half, 1), pltpu.roll(yc, half, 1))
        return yc * cos_ref[table] + swapped * sin_ref[table]

    @pl.when(mode == MODE_PLAIN)
    def _():
        o_ref[...] = y.astype(o_ref.dtype)

    for rope_mode, (table, half) in _ROPE_OF_MODE.items():
        @pl.when(mode == rope_mode)
        def _(table=table, half=half):
            for c in range(tn // LANES):
                sl = slice(c * LANES, (c + 1) * LANES)
                o_ref[:, sl] = rope_chunk(y[:, sl], table, half).astype(o_ref.dtype)

    @pl.when(mode == MODE_RMS)
    def _():
        r = lax.rsqrt(jnp.mean(y * y, axis=-1, keepdims=True) + NORM_EPS)
        o_ref[...] = (y * r * g_ref[0:1, :]).astype(o_ref.dtype)

    @pl.when(mode == MODE_CKV)
    def _():
        ckv = y[:, :MLA_KV_RANK]
        r = lax.rsqrt(jnp.mean(ckv * ckv, axis=-1, keepdims=True) + NORM_EPS)
        o_ref[:, :MLA_KV_RANK] = (ckv * r * g_ref[1:2, :MLA_KV_RANK]).astype(o_ref.dtype)
        sl = slice(MLA_KV_RANK, MLA_KV_RANK + LANES)
        table, half = _ROPE_OF_MODE[MODE_ROPE_MLA]
        o_ref[:, sl] = rope_chunk(y[:, sl], table, half).astype(o_ref.dtype)
        o_ref[:, MLA_KV_RANK + LANES:] = jnp.zeros((tm, tn - MLA_KV_RANK - LANES), o_ref.dtype)

    @pl.when(mode == MODE_SIGMOID)
    def _():
        o_ref[...] = (0.5 * jnp.tanh(0.5 * y) + 0.5).astype(o_ref.dtype)


def _proj(x, x_col_block, k_dim, w, modes, scales, cos, sin, gains, tm):
    t = x.shape[0]
    n = w.shape[1]
    assert n % PROJ_TN == 0 and len(modes) == len(scales) == n // PROJ_TN and w.shape[0] == k_dim
    grid_spec = pltpu.PrefetchScalarGridSpec(
        num_scalar_prefetch=1,
        grid=(t // tm, n // PROJ_TN),
        in_specs=[pl.BlockSpec(memory_space=pltpu.SMEM),
                  pl.BlockSpec((tm, k_dim), lambda i, j, m: (i, x_col_block)),
                  pl.BlockSpec((k_dim, PROJ_TN), lambda i, j, m: (0, j)),
                  pl.BlockSpec((3, tm, LANES), lambda i, j, m: (0, i, 0)),
                  pl.BlockSpec((3, tm, LANES), lambda i, j, m: (0, i, 0)),
                  pl.BlockSpec((8, PROJ_TN), lambda i, j, m: (0, 0))],
        out_specs=pl.BlockSpec((tm, PROJ_TN), lambda i, j, m: (i, j)),
    )
    return pl.pallas_call(
        _proj_kernel,
        grid_spec=grid_spec,
        out_shape=jax.ShapeDtypeStruct((t, n), BF16),
        compiler_params=_cparams(("parallel", "arbitrary")),
        name="proj",
    )(jnp.asarray(modes, I32), jnp.asarray(scales, F32), x, w, cos, sin, gains)


_NT_DIMS = (((1,), (1,)), ((), ()))


def _causal_mask(tq):
    row = lax.broadcasted_iota(I32, (tq, tq), 0)
    col = lax.broadcasted_iota(I32, (tq, tq), 1)
    return col <= row


def _score_strip(s_ref, i, tq, block_scores, block_mask):
    causal = _causal_mask(tq)
    for j in range(i + 1):
        sc = block_scores(j)
        if j == i:
            sc = jnp.where(causal, sc, NEG_BIG)
        elif block_mask is not None:
            sc = jnp.where(block_mask(j), sc, NEG_BIG)
        s_ref[:, j * tq:(j + 1) * tq] = sc


def _softmax_times_v(s_ref, p_ref, v_ref, n):
    tq = s_ref.shape[0]
    nch = n // LANES
    mrun = s_ref[:, 0:LANES]
    for c in range(1, nch):
        mrun = jnp.maximum(mrun, s_ref[:, c * LANES:(c + 1) * LANES])
    m = jnp.broadcast_to(jnp.max(mrun, axis=-1, keepdims=True), (tq, LANES))
    lrun = jnp.zeros((tq, LANES), F32)
    for c in range(nch):
        sl = slice(c * LANES, (c + 1) * LANES)
        p = jnp.exp(s_ref[:, sl] - m)
        lrun = lrun + p
        p_ref[:, sl] = p.astype(BF16)
    l = jnp.sum(lrun, axis=-1, keepdims=True)
    o = jnp.dot(p_ref[:, :n], v_ref[0:n, :], preferred_element_type=F32)
    return o / l


def _diff_attn_kernel(lam_ref, g_ref, q_ref, k_ref, v_ref, o_ref, s1_ref, p1_ref, s2_ref, p2_ref,
                      *, tq, lambda_init):
    nq = q_ref.shape[0] // tq
    lp = lam_ref[...]
    lam = (jnp.exp(jnp.sum(lp[0:1] * lp[1:2], axis=-1, keepdims=True))
           - jnp.exp(jnp.sum(lp[2:3] * lp[3:4], axis=-1, keepdims=True)) + lambda_init)
    lane = lax.broadcasted_iota(I32, (tq, LANES), 1)
    for i in range(nq):
        q = q_ref[i * tq:(i + 1) * tq, :]
        q1 = jnp.where(lane < DIFF_HALF_DIM, q, jnp.zeros_like(q))
        q2 = jnp.where(lane >= DIFF_HALF_DIM, q, jnp.zeros_like(q))
        kblk = lambda j: k_ref[j * tq:(j + 1) * tq, :]
        b = i % 2
        _score_strip(s1_ref.at[b], i, tq,
                     lambda j: lax.dot_general(q1, kblk(j), _NT_DIMS, preferred_element_type=F32), None)
        _score_strip(s2_ref.at[b], i, tq,
                     lambda j: lax.dot_general(q2, kblk(j), _NT_DIMS, preferred_element_type=F32), None)
        n = (i + 1) * tq
        o = (_softmax_times_v(s1_ref.at[b], p1_ref.at[b], v_ref, n)
             - lam * _softmax_times_v(s2_ref.at[b], p2_ref.at[b], v_ref, n))
        r = lax.rsqrt(jnp.mean(o * o, axis=-1, keepdims=True) + NORM_EPS)
        o_ref[i * tq:(i + 1) * tq, :] = (o * r * g_ref[...] * (1.0 - lambda_init)).astype(o_ref.dtype)


def _diff_attention(proj, lam_parts, subln_g, b, s, lambda_init):
    t = b * s
    tq = min(ATT_TQ, s)
    kern = functools.partial(_diff_attn_kernel, tq=tq, lambda_init=lambda_init)
    return pl.pallas_call(
        kern,
        grid=(b, DIFF_HEADS),
        in_specs=[pl.BlockSpec((4, DIFF_HALF_DIM), lambda bi, h: (0, 0)),
                  pl.BlockSpec((1, LANES), lambda bi, h: (0, 0)),
                  pl.BlockSpec((s, LANES), lambda bi, h: (bi, h)),
                  pl.BlockSpec((s, LANES), lambda bi, h: (bi, DIFF_HEADS + h)),
                  pl.BlockSpec((s, LANES), lambda bi, h: (bi, 2 * DIFF_HEADS + h))],
        out_specs=pl.BlockSpec((s, LANES), lambda bi, h: (bi, h)),
        out_shape=jax.ShapeDtypeStruct((t, A_WIDTH), BF16),
        scratch_shapes=[pltpu.VMEM((2, tq, s), F32), pltpu.VMEM((2, tq, s), BF16),
                        pltpu.VMEM((2, tq, s), F32), pltpu.VMEM((2, tq, s), BF16)],
        compiler_params=_cparams(("parallel", "parallel")),
        name="diff_attention",
    )(lam_parts, subln_g.reshape(1, LANES), proj, proj, proj)


def _mla_attn_kernel(qn_ref, qr_ref, kn_ref, kpe_ref, v_ref, o_ref, s_ref, p_ref, *, tq):
    h = pl.program_id(1)
    nq = qn_ref.shape[0] // tq
    lane = lax.broadcasted_iota(I32, (tq, LANES), 1)
    mine = (lane >= MLA_ROPE) == (h % 2 == 1)
    for i in range(nq):
        qn = qn_ref[i * tq:(i + 1) * tq, :]
        qr = qr_ref[i * tq:(i + 1) * tq, :]
        qr = jnp.where(mine, qr, jnp.zeros_like(qr))

        qcat = jnp.concatenate([qn, qr], axis=1)

        def block_scores(j, qcat=qcat):
            rows = slice(j * tq, (j + 1) * tq)
            kcat = jnp.concatenate([kn_ref[rows, :], kpe_ref[rows, :]], axis=1)
            return lax.dot_general(qcat, kcat, _NT_DIMS, preferred_element_type=F32)

        _score_strip(s_ref.at[i % 2], i, tq, block_scores, None)
        o = _softmax_times_v(s_ref.at[i % 2], p_ref.at[i % 2], v_ref, (i + 1) * tq)
        o_ref[i * tq:(i + 1) * tq, :] = o.astype(o_ref.dtype)


def _mla_attention(qbuf, kvbuf, proj, b, s):
    t = b * s
    tq = min(ATT_TQ, s)
    kpe_block = (4 * PROJ_TN + MLA_KV_RANK) // LANES
    return pl.pallas_call(
        functools.partial(_mla_attn_kernel, tq=tq),
        grid=(b, MLA_HEADS),
        in_specs=[pl.BlockSpec((s, LANES), lambda bi, h: (bi, h)),
                  pl.BlockSpec((s, LANES), lambda bi, h: (bi, MLA_HEADS + h // 2)),
                  pl.BlockSpec((s, LANES), lambda bi, h: (bi, h)),
                  pl.BlockSpec((s, LANES), lambda bi, h: (bi, kpe_block)),
                  pl.BlockSpec((s, LANES), lambda bi, h: (bi, MLA_HEADS + h))],
        out_specs=pl.BlockSpec((s, LANES), lambda bi, h: (bi, h)),
        out_shape=jax.ShapeDtypeStruct((t, B_WIDTH), BF16),
        scratch_shapes=[pltpu.VMEM((2, tq, s), F32), pltpu.VMEM((2, tq, s), BF16)],
        compiler_params=_cparams(("parallel", "parallel")),
        name="mla_attention",
    )(qbuf, qbuf, kvbuf, proj, kvbuf)


def _moba_attn_kernel(q_ref, k_ref, v_ref, o_ref, s_ref, p_ref, kmean_ref):
    tq = MOBA_BLOCK
    nb = q_ref.shape[0] // tq
    kmean_ref[...] = jnp.zeros(kmean_ref.shape, F32)
    for j in range(nb):
        kb = k_ref[j * tq:(j + 1) * tq, :].astype(F32)
        kmean_ref[j:j + 1, :] = jnp.sum(kb, axis=0, keepdims=True) * (1.0 / tq)
    km = kmean_ref[...]
    km_hi = km.astype(BF16)
    km_lo = (km - km_hi.astype(F32)).astype(BF16)
    nbp = -(-nb // 8) * 8
    blk = lax.broadcasted_iota(I32, (nbp, tq), 0)
    for i in range(nb):
        q = q_ref[i * tq:(i + 1) * tq, :]
        gate = (lax.dot_general(km_hi, q, _NT_DIMS, preferred_element_type=F32)
                + lax.dot_general(km_lo, q, _NT_DIMS, preferred_element_type=F32))[:nbp, :]
        fully_past = blk < i
        g = jnp.where(fully_past, gate, -jnp.inf)
        cnt = jnp.zeros(g.shape, F32)
        for jp in range(i):
            row = g[jp:jp + 1, :]
            ahead = (row > g) | ((row == g) & (jp < blk))
            cnt = cnt + jnp.where(ahead, 1.0, 0.0)
        kept_t = jnp.where((cnt < MOBA_TOPK) & fully_past, 1.0, 0.0)
        kept_t = jnp.concatenate([kept_t, jnp.zeros((LANES - nbp, tq), F32)], axis=0)
        kept = kept_t.T > 0.5
        _score_strip(s_ref.at[i % 2], i, tq,
                     lambda j, q=q: lax.dot_general(q, k_ref[j * tq:(j + 1) * tq, :], _NT_DIMS,
                                                    preferred_element_type=F32),
                     lambda j, kept=kept: kept[:, j:j + 1])
        o = _softmax_times_v(s_ref.at[i % 2], p_ref.at[i % 2], v_ref, (i + 1) * tq)
        o_ref[i * tq:(i + 1) * tq, :] = o.astype(o_ref.dtype)


def _moba_attention(proj, b, s):
    assert s % MOBA_BLOCK == 0 and s // MOBA_BLOCK <= LANES
    t = b * s
    base = 5 * PROJ_TN // LANES
    return pl.pallas_call(
        _moba_attn_kernel,
        grid=(b, MOBA_HEADS),
        in_specs=[pl.BlockSpec((s, LANES), lambda bi, h: (bi, base + h)),
                  pl.BlockSpec((s, LANES), lambda bi, h: (bi, base + MOBA_HEADS + h)),
                  pl.BlockSpec((s, LANES), lambda bi, h: (bi, base + 2 * MOBA_HEADS + h))],
        out_specs=pl.BlockSpec((s, LANES), lambda bi, h: (bi, h)),
        out_shape=jax.ShapeDtypeStruct((t, C_WIDTH), BF16),
        scratch_shapes=[pltpu.VMEM((2, MOBA_BLOCK, s), F32), pltpu.VMEM((2, MOBA_BLOCK, s), BF16),
                        pltpu.VMEM((LANES, LANES), F32)],
        compiler_params=_cparams(("parallel", "parallel")),
        name="moba_attention",
    )(proj, proj, proj)


def _layer_norm_rows(z, g, b):
    mu = jnp.mean(z, axis=-1, keepdims=True)
    zc = z - mu
    var = jnp.mean(zc * zc, axis=-1, keepdims=True)
    return zc * lax.rsqrt(var + NORM_EPS) * g + b


def _pack_halves(z):
    n = z.shape[1] // 2
    lo = lax.bitcast_convert_type(z[:, :n].astype(BF16).astype(F32), U32)
    hi = lax.bitcast_convert_type(z[:, n:].astype(BF16).astype(F32), U32)
    return (lo >> 16) | hi


def _unpack_halves(u):
    lo = lax.bitcast_convert_type(u << 16, F32)
    hi = lax.bitcast_convert_type(u & jnp.uint32(0xFFFF0000), F32)
    return lo, hi


ROW_WORDS = D_MODEL // 2
ROW_SUBLANES = ROW_WORDS // LANES


def _store_row_tiles(ref, packed):
    tm = packed.shape[0]
    for a in range(ROW_SUBLANES):
        ref[pl.ds(a, tm, stride=ROW_SUBLANES), :] = packed[:, a * LANES:(a + 1) * LANES]


def _load_row_tile_chunk(ref, a, tm):
    return ref[pl.ds(a, tm, stride=ROW_SUBLANES), :]


def _merge_kernel(oa_ref, ob_ref, oc_ref, g0_ref, g1_ref, g2_ref, x_ref, wb_ref, wo_ref,
                  lng_ref, lnb_ref, xo_ref, xb_ref, xp_ref):
    ya = jnp.dot(oa_ref[...], wb_ref[:A_WIDTH, :], preferred_element_type=F32)
    yb = jnp.dot(ob_ref[...], wb_ref[A_WIDTH:A_WIDTH + B_WIDTH, :], preferred_element_type=F32)
    yc = jnp.dot(oc_ref[...], wb_ref[A_WIDTH + B_WIDTH:, :], preferred_element_type=F32)
    y = g0_ref[...].astype(F32) * ya + g1_ref[...].astype(F32) * yb + g2_ref[...].astype(F32) * yc
    mix = jnp.dot(y.astype(BF16), wo_ref[...], preferred_element_type=F32)
    z = _layer_norm_rows(DEEPNORM_ALPHA * x_ref[...] + mix, lng_ref[...], lnb_ref[...])
    xo_ref[...] = z
    xb_ref[...] = z.astype(BF16)
    _store_row_tiles(xp_ref, _pack_halves(z))


def _merge(oa, ob, oc, proj, x, w_branch, w_out, ln_g, ln_b):
    t = x.shape[0]
    tm = min(MERGE_TM, t)
    gate_base = 8 * PROJ_TN // D_MODEL
    row = lambda i: (i, 0)
    whole = lambda i: (0, 0)
    resident = dict(pipeline_mode=pl.Buffered(1))
    return pl.pallas_call(
        _merge_kernel,
        grid=(t // tm,),
        in_specs=[pl.BlockSpec((tm, A_WIDTH), row),
                  pl.BlockSpec((tm, B_WIDTH), row),
                  pl.BlockSpec((tm, C_WIDTH), row),
                  pl.BlockSpec((tm, D_MODEL), lambda i: (i, gate_base)),
                  pl.BlockSpec((tm, D_MODEL), lambda i: (i, gate_base + 1)),
                  pl.BlockSpec((tm, D_MODEL), lambda i: (i, gate_base + 2)),
                  pl.BlockSpec((tm, D_MODEL), row),
                  pl.BlockSpec((D_MODEL, D_MODEL), whole, **resident),
                  pl.BlockSpec((D_MODEL, D_MODEL), whole, **resident),
                  pl.BlockSpec((1, D_MODEL), whole),
                  pl.BlockSpec((1, D_MODEL), whole)],
        out_specs=[pl.BlockSpec((tm, D_MODEL), row),
                   pl.BlockSpec((tm, D_MODEL), row),
                   pl.BlockSpec((tm * ROW_SUBLANES, LANES), row)],
        out_shape=[jax.ShapeDtypeStruct((t, D_MODEL), F32),
                   jax.ShapeDtypeStruct((t, D_MODEL), BF16),
                   jax.ShapeDtypeStruct((t * ROW_SUBLANES, LANES), U32)],
        compiler_params=_cparams(("parallel",)),
        name="merge_outproj_ln",
    )(oa, ob, oc, proj, proj, proj, x, w_branch, w_out, ln_g.reshape(1, -1), ln_b.reshape(1, -1))


def _split_bf16(a):
    hi = a.astype(BF16)
    lo = (a - hi.astype(F32)).astype(BF16)
    return hi, lo


def _router_kernel(x_ref, wt_ref, bias_ref, upper_ref, lower_ref,
                   e8_ref, r8_ref, w8_ref, cnt_ref, carry_ref):
    i = pl.program_id(0)

    @pl.when(i == 0)
    def _():
        carry_ref[...] = jnp.zeros(carry_ref.shape, F32)

    xh, xl = _split_bf16(x_ref[...])
    wh, wl = _split_bf16(wt_ref[...])
    logits = (lax.dot_general(wh, xh, _NT_DIMS, preferred_element_type=F32)
              + lax.dot_general(wh, xl, _NT_DIMS, preferred_element_type=F32)
              + lax.dot_general(wl, xh, _NT_DIMS, preferred_element_type=F32))
    scores = 1.0 / (1.0 + jnp.exp(-logits))
    choice = scores + bias_ref[:, 0:1]
    tm = choice.shape[1]
    sub = lax.broadcasted_iota(I32, (GROUP_SIZE, tm), 0)

    group_rows = []
    for g in range(N_GROUPS):
        cg = choice[g * GROUP_SIZE:(g + 1) * GROUP_SIZE, :]
        m1 = jnp.max(cg, axis=0, keepdims=True)
        first = jnp.min(jnp.where(cg == m1, sub, GROUP_SIZE), axis=0, keepdims=True)
        m2 = jnp.max(jnp.where(sub == first, -jnp.inf, cg), axis=0, keepdims=True)
        group_rows.append(m1 + m2)
    gs = jnp.concatenate(group_rows, axis=0)
    ahead = jnp.zeros(gs.shape, F32)
    for gp in range(N_GROUPS):
        rowv = gs[gp:gp + 1, :]
        ahead = ahead + jnp.where((rowv > gs) | ((rowv == gs) & (gp < sub)), 1.0, 0.0)
    keep_group = ahead < TOPK_GROUPS
    masked = jnp.concatenate(
        [jnp.where(keep_group[g:g + 1, :], choice[g * GROUP_SIZE:(g + 1) * GROUP_SIZE, :], -jnp.inf)
         for g in range(N_GROUPS)], axis=0)

    eidx = lax.broadcasted_iota(I32, masked.shape, 0)
    ahead = jnp.zeros(masked.shape, F32)
    for ep in range(N_EXPERTS):
        rowv = masked[ep:ep + 1, :]
        ahead = ahead + jnp.where((rowv > masked) | ((rowv == masked) & (ep < eidx)), 1.0, 0.0)
    sel = ahead < TOPK
    picked = jnp.where(sel, scores, 0.0)
    gates = picked / jnp.sum(picked, axis=0, keepdims=True) * ROUTED_SCALE
    self32 = jnp.where(sel, 1.0, 0.0)
    selb = self32.astype(BF16)

    rank = jnp.dot(selb, upper_ref[...], preferred_element_type=F32) + carry_ref[:, 0:1]
    slot = jnp.dot(lower_ref[...], selb, preferred_element_type=F32)
    carry_ref[...] = carry_ref[...] + jnp.sum(self32, axis=1, keepdims=True)
    cnt_ref[...] = carry_ref[...].astype(I32)

    eidf = eidx.astype(F32)
    e_rows, r_rows, w_rows = [], [], []
    for k in range(TOPK):
        hit = sel & (slot == float(k))
        e_rows.append(jnp.sum(jnp.where(hit, eidf, 0.0), axis=0, keepdims=True))
        r_rows.append(jnp.sum(jnp.where(hit, rank, 0.0), axis=0, keepdims=True))
        w_rows.append(jnp.sum(jnp.where(hit, gates, 0.0), axis=0, keepdims=True))
    e8_ref[...] = jnp.concatenate(e_rows, axis=0).astype(I32)
    r8_ref[...] = jnp.concatenate(r_rows, axis=0).astype(I32)
    w8_ref[...] = jnp.concatenate(w_rows, axis=0)


def _router(x, router_w, router_bias):
    t = x.shape[0]
    tm = min(ROUTER_TM, t)
    upper = (jnp.arange(tm)[:, None] < jnp.arange(tm)[None, :]).astype(BF16)
    lower = (jnp.arange(N_EXPERTS)[None, :] < jnp.arange(N_EXPERTS)[:, None]).astype(BF16)
    bias = jnp.broadcast_to(router_bias.astype(F32)[:, None], (N_EXPERTS, LANES))
    whole = lambda i: (0, 0)
    col = lambda i: (0, i)
    return pl.pallas_call(
        _router_kernel,
        grid=(t // tm,),
        in_specs=[pl.BlockSpec((tm, D_MODEL), lambda i: (i, 0)),
                  pl.BlockSpec((N_EXPERTS, D_MODEL), whole),
                  pl.BlockSpec((N_EXPERTS, LANES), whole),
                  pl.BlockSpec((tm, tm), whole),
                  pl.BlockSpec((N_EXPERTS, N_EXPERTS), whole)],
        out_specs=[pl.BlockSpec((TOPK, tm), col),
                   pl.BlockSpec((TOPK, tm), col),
                   pl.BlockSpec((TOPK, tm), col),
                   pl.BlockSpec((N_EXPERTS, LANES), whole)],
        out_shape=[jax.ShapeDtypeStruct((TOPK, t), I32),
                   jax.ShapeDtypeStruct((TOPK, t), I32),
                   jax.ShapeDtypeStruct((TOPK, t), F32),
                   jax.ShapeDtypeStruct((N_EXPERTS, LANES), I32)],
        scratch_shapes=[pltpu.VMEM((N_EXPERTS, LANES), F32)],
        compiler_params=_cparams(("arbitrary",)),
        name="router",
    )(x, router_w.T, bias, upper, lower)


def _row_tile(ref, r):
    return ref.at[pl.ds(pl.multiple_of(r * ROW_SUBLANES, ROW_SUBLANES), ROW_SUBLANES)]


def _dispatch_kernel(pos_ref, x_ref, xs_ref, sem):
    tm = x_ref.shape[0] // ROW_SUBLANES

    def row_copy(t, k):
        return pltpu.make_async_copy(_row_tile(x_ref, t), _row_tile(xs_ref, pos_ref[k, t]), sem)

    def start(t, c):
        for k in range(TOPK):
            row_copy(t, k).start(priority=k % 2)
        return c
    lax.fori_loop(0, tm, start, 0)

    def wait(t, c):
        for k in range(TOPK):
            row_copy(t, k).wait()
        return c
    lax.fori_loop(0, tm, wait, 0)


def _dispatch(xp, pos8, n_rows):
    t = xp.shape[0] // ROW_SUBLANES
    tm = min(ROW_TM, t)
    return pl.pallas_call(
        _dispatch_kernel,
        grid=(t // tm,),
        in_specs=[pl.BlockSpec((TOPK, tm), lambda i: (0, i), memory_space=pltpu.SMEM),
                  pl.BlockSpec((tm * ROW_SUBLANES, LANES), lambda i: (i, 0))],
        out_specs=pl.BlockSpec(memory_space=pl.ANY),
        scratch_shapes=[pltpu.SemaphoreType.DMA(())],
        out_shape=jax.ShapeDtypeStruct((n_rows * ROW_SUBLANES, LANES), U32),
        compiler_params=_cparams(("arbitrary",)),
        name="dispatch",
    )(pos8, xp)


def _expert_kernel(te_ref, rows_ref, nv_ref, xs_ref, wg_ref, wu_ref, wd_ref, ys_ref,
                   wgb_ref, wub_ref, wdb_ref):
    i = pl.program_id(0)

    @pl.when(i < nv_ref[0])
    def _():
        @pl.when((i == 0) | (te_ref[i] != te_ref[jnp.maximum(i - 1, 0)]))
        def _():
            wgb_ref[...] = wg_ref[0].astype(BF16)
            wub_ref[...] = wu_ref[0].astype(BF16)
            wdb_ref[...] = wd_ref[0].astype(BF16)

        half = D_MODEL // 2
        tm = EXPERT_TM
        u = jnp.concatenate([_load_row_tile_chunk(xs_ref, a, tm) for a in range(ROW_SUBLANES)], axis=1)
        live = lax.broadcasted_iota(I32, (tm, 1), 0) < rows_ref[i]
        lo, hi = _unpack_halves(jnp.where(live, u, jnp.zeros_like(u)))
        xl = lo.astype(BF16)
        xh = hi.astype(BF16)
        gate = (jnp.dot(xl, wgb_ref[:half, :], preferred_element_type=F32)
                + jnp.dot(xh, wgb_ref[half:, :], preferred_element_type=F32))
        up = (jnp.dot(xl, wub_ref[:half, :], preferred_element_type=F32)
              + jnp.dot(xh, wub_ref[half:, :], preferred_element_type=F32))
        hid = (gate / (1.0 + jnp.exp(-gate))) * up
        y = jnp.dot(hid.astype(BF16), wdb_ref[...], preferred_element_type=F32)
        _store_row_tiles(ys_ref, _pack_halves(y))


def _experts(xs, tile_expert, tile_rows, n_valid, w_gate, w_up, w_down, layer):
    n_rows = xs.shape[0] // ROW_SUBLANES
    n_tiles = n_rows // EXPERT_TM
    row = lambda i, te, tr, nv: (jnp.minimum(i, nv[0] - 1), 0)
    wsel = lambda i, te, tr, nv: (layer * N_EXPERTS + te[i], 0, 0)
    grid_spec = pltpu.PrefetchScalarGridSpec(
        num_scalar_prefetch=3,
        grid=(n_tiles,),
        in_specs=[pl.BlockSpec((EXPERT_TM * ROW_SUBLANES, LANES), row),
                  pl.BlockSpec((1, D_MODEL, EXPERT_FF), wsel),
                  pl.BlockSpec((1, D_MODEL, EXPERT_FF), wsel),
                  pl.BlockSpec((1, EXPERT_FF, D_MODEL), wsel)],
        out_specs=pl.BlockSpec((EXPERT_TM * ROW_SUBLANES, LANES), row),
        scratch_shapes=[pltpu.VMEM((D_MODEL, EXPERT_FF), BF16), pltpu.VMEM((D_MODEL, EXPERT_FF), BF16),
                        pltpu.VMEM((EXPERT_FF, D_MODEL), BF16)],
    )
    return pl.pallas_call(
        _expert_kernel,
        grid_spec=grid_spec,
        out_shape=jax.ShapeDtypeStruct((n_rows * ROW_SUBLANES, LANES), U32),
        compiler_params=_cparams(("arbitrary",)),
        name="experts",
    )(tile_expert, tile_rows, n_valid, xs, w_gate, w_up, w_down)


def _combine_kernel(pos_ref, pos_next_ref, w8_ref, ys_ref, x_ref, xb_ref, wsg_ref, wsu_ref, wsd_ref,
                    lng_ref, lnb_ref, xo_ref, xob_ref, buf_ref, sem):
    i = pl.program_id(0)
    tm = x_ref.shape[0]
    slot = i % 2

    def row_copy(p_ref, s, t, k):
        return pltpu.make_async_copy(_row_tile(ys_ref, p_ref[k, t]), _row_tile(buf_ref.at[s, k], t),
                                     sem.at[s])

    def start_all(p_ref, s):
        def start(t, c):
            for k in range(TOPK):
                row_copy(p_ref, s, t, k).start(priority=k % 2)
            return c
        lax.fori_loop(0, tm, start, 0)

    @pl.when(i == 0)
    def _():
        start_all(pos_ref, 0)

    @pl.when(i + 1 < pl.num_programs(0))
    def _():
        start_all(pos_next_ref, 1 - slot)

    xb = xb_ref[...]
    sg = jnp.dot(xb, wsg_ref[...], preferred_element_type=F32)
    su = jnp.dot(xb, wsu_ref[...], preferred_element_type=F32)
    hid = (sg / (1.0 + jnp.exp(-sg))) * su
    shared = jnp.dot(hid.astype(BF16), wsd_ref[...], preferred_element_type=F32)

    def wait(t, c):
        for k in range(TOPK):
            row_copy(pos_ref, slot, t, k).wait()
        return c
    lax.fori_loop(0, tm, wait, 0)

    w8 = w8_ref[...]
    wk = [jnp.broadcast_to(w8[:, k:k + 1], (tm, LANES)) for k in range(TOPK)]
    lo_parts, hi_parts = [], []
    for a in range(ROW_SUBLANES):
        acc_lo = jnp.zeros((tm, LANES), F32)
        acc_hi = jnp.zeros((tm, LANES), F32)
        for k in range(TOPK):
            lo, hi = _unpack_halves(_load_row_tile_chunk(buf_ref.at[slot, k], a, tm))
            acc_lo = acc_lo + wk[k] * lo
            acc_hi = acc_hi + wk[k] * hi
        lo_parts.append(acc_lo)
        hi_parts.append(acc_hi)
    routed = jnp.concatenate(lo_parts + hi_parts, axis=1)
    z = _layer_norm_rows(DEEPNORM_ALPHA * x_ref[...] + (routed + shared), lng_ref[...], lnb_ref[...])
    xo_ref[...] = z
    xob_ref[...] = z.astype(BF16)


def _combine(ys, pos8, w8t, x, xb, ws_gate, ws_up, ws_down, ln_g, ln_b):
    t = x.shape[0]
    tm = min(ROW_TM, t)
    row = lambda i: (i, 0)
    whole = lambda i: (0, 0)
    n_steps = t // tm
    grid_spec = pl.GridSpec(
        grid=(n_steps,),
        in_specs=[pl.BlockSpec((TOPK, tm), lambda i: (0, i), memory_space=pltpu.SMEM),
                  pl.BlockSpec((TOPK, tm), lambda i: (0, jnp.minimum(i + 1, n_steps - 1)),
                               memory_space=pltpu.SMEM),
                  pl.BlockSpec((tm, TOPK), row),
                  pl.BlockSpec(memory_space=pl.ANY),
                  pl.BlockSpec((tm, D_MODEL), row),
                  pl.BlockSpec((tm, D_MODEL), row),
                  pl.BlockSpec((D_MODEL, EXPERT_FF), whole),
                  pl.BlockSpec((D_MODEL, EXPERT_FF), whole),
                  pl.BlockSpec((EXPERT_FF, D_MODEL), whole),
                  pl.BlockSpec((1, D_MODEL), whole),
                  pl.BlockSpec((1, D_MODEL), whole)],
        out_specs=[pl.BlockSpec((tm, D_MODEL), row),
                   pl.BlockSpec((tm, D_MODEL), row)],
        scratch_shapes=[pltpu.VMEM((2, TOPK, tm * ROW_SUBLANES, LANES), U32),
                        pltpu.SemaphoreType.DMA((2,))],
    )
    return pl.pallas_call(
        _combine_kernel,
        grid_spec=grid_spec,
        out_shape=[jax.ShapeDtypeStruct((t, D_MODEL), F32),
                   jax.ShapeDtypeStruct((t, D_MODEL), BF16)],
        compiler_params=_cparams(("arbitrary",)),
        name="combine_shared_ln",
    )(pos8, pos8, w8t, ys, x, xb, ws_gate, ws_up, ws_down, ln_g.reshape(1, -1), ln_b.reshape(1, -1))


_CKV_TILE = 4
_TAIL_SHIFT = PROJ_TN - MLA_KV_RANK - MLA_ROPE


def _in_proj_weight_kernel(prev_ref, cur_ref, o_ref):
    j = pl.program_id(1)

    @pl.when(j < _CKV_TILE)
    def _():
        o_ref[...] = cur_ref[0].astype(BF16)

    @pl.when(j == _CKV_TILE)
    def _():
        w = cur_ref[0]
        kpe = w[:, MLA_KV_RANK:MLA_KV_RANK + MLA_ROPE]
        pad = jnp.zeros((w.shape[0], _TAIL_SHIFT - MLA_ROPE), F32)
        o_ref[...] = jnp.concatenate([w[:, :MLA_KV_RANK], kpe, kpe, pad], axis=1).astype(BF16)

    @pl.when(j > _CKV_TILE)
    def _():
        keep = PROJ_TN - _TAIL_SHIFT
        o_ref[...] = jnp.concatenate([prev_ref[0][:, keep:], cur_ref[0][:, :keep]], axis=1).astype(BF16)


def _in_proj_weight(w_in_all, layer):
    d = w_in_all.shape[1]
    tr = min(1024, d)
    return pl.pallas_call(
        _in_proj_weight_kernel,
        grid=(d // tr, IN_COLS_PADDED // PROJ_TN),
        in_specs=[pl.BlockSpec((1, tr, PROJ_TN), lambda i, j: (layer, i, jnp.maximum(j - 1, 0))),
                  pl.BlockSpec((1, tr, PROJ_TN), lambda i, j: (layer, i, j))],
        out_specs=pl.BlockSpec((tr, PROJ_TN), lambda i, j: (i, j)),
        out_shape=jax.ShapeDtypeStruct((d, IN_COLS_PADDED), BF16),
        compiler_params=_cparams(("parallel", "parallel")),
        name="in_proj_weight",
    )(w_in_all, w_in_all)


def _uq_weight(w_uq):
    w = w_uq.reshape(MLA_Q_RANK, MLA_HEADS, MLA_NOPE + MLA_ROPE)
    return jnp.concatenate([w[:, :, :MLA_NOPE].reshape(MLA_Q_RANK, -1),
                            w[:, :, MLA_NOPE:].reshape(MLA_Q_RANK, -1)], axis=1).astype(BF16)


def _ukv_weight(w_ukv):
    w = w_ukv.reshape(MLA_KV_RANK, MLA_HEADS, MLA_NOPE + MLA_V)
    return jnp.concatenate([w[:, :, :MLA_NOPE].reshape(MLA_KV_RANK, -1),
                            w[:, :, MLA_NOPE:].reshape(MLA_KV_RANK, -1)], axis=1).astype(BF16)


def _expert_tiling(counts, n_tiles):
    tiles = (counts + EXPERT_TM - 1) // EXPERT_TM
    ends = jnp.cumsum(tiles)
    offsets = ((ends - tiles) * EXPERT_TM).astype(I32)
    tile_ids = jnp.arange(n_tiles, dtype=I32)
    tile_expert = jnp.minimum(
        jnp.sum((ends[None, :] <= tile_ids[:, None]).astype(I32), axis=1), N_EXPERTS - 1).astype(I32)
    first_tile = (ends - tiles).astype(I32)
    tile_rows = jnp.clip(counts[tile_expert] - (tile_ids - first_tile[tile_expert]) * EXPERT_TM,
                         0, EXPERT_TM).astype(I32)
    return offsets, tile_expert, tile_rows, ends[-1:].astype(I32)


def kernel(x, positions, w_in, diff_lambda_q1, diff_lambda_k1, diff_lambda_q2, diff_lambda_k2, diff_subln_g, mla_q_norm_g, mla_w_uq, mla_kv_norm_g, mla_w_ukv, w_branch, w_out, ln_mix_g, ln_mix_b, router_w, router_bias, expert_w_gate, expert_w_up, expert_w_down, shared_w_gate, shared_w_up, shared_w_down, ln_ffn_g, ln_ffn_b):
    b, s, d = x.shape
    t = b * s
    depth = w_in.shape[0]
    cos, sin = _rope_tables(positions)
    xf = x.reshape(t, d)
    xb = xf.astype(BF16)
    n_tiles = t * TOPK // EXPERT_TM + N_EXPERTS
    proj_tm = min(PROJ_TM, t)

    for l in range(depth):
        gains = jnp.zeros((8, PROJ_TN), F32)
        gains = gains.at[0, :].set(mla_q_norm_g[l]).at[1, :MLA_KV_RANK].set(mla_kv_norm_g[l])
        proj = _proj(xb, 0, d, _in_proj_weight(w_in, l), IN_TILE_MODES, IN_TILE_SCALES,
                     cos, sin, gains, proj_tm)
        qbuf = _proj(proj, 3, MLA_Q_RANK, _uq_weight(mla_w_uq[l]),
                     [MODE_PLAIN, MODE_PLAIN, MODE_ROPE_MLA], [(MLA_NOPE + MLA_ROPE) ** -0.5] * 3,
                     cos, sin, gains, proj_tm)
        kvbuf = _proj(proj, 4 * PROJ_TN // MLA_KV_RANK, MLA_KV_RANK, _ukv_weight(mla_w_ukv[l]),
                      [MODE_PLAIN] * 4, [1.0] * 4, cos, sin, gains, proj_tm)

        lambda_init = 0.8 - 0.6 * math.exp(-0.3 * l)
        lam_parts = jnp.stack([diff_lambda_q1[l], diff_lambda_k1[l],
                               diff_lambda_q2[l], diff_lambda_k2[l]]).astype(F32)
        oa = _diff_attention(proj, lam_parts, diff_subln_g[l], b, s, lambda_init)
        ob = _mla_attention(qbuf, kvbuf, proj, b, s)
        oc = _moba_attention(proj, b, s)

        x1, x1b, x1p = _merge(oa, ob, oc, proj, xf, w_branch[l].astype(BF16), w_out[l].astype(BF16),
                              ln_mix_g[l], ln_mix_b[l])

        e8, r8, w8, counts = _router(x1, router_w[l], router_bias[l])
        offsets, tile_expert, tile_rows, n_valid = _expert_tiling(counts[:, 0], n_tiles)
        expert_ids = jnp.arange(N_EXPERTS, dtype=I32)[:, None, None]
        pos8 = jnp.sum(jnp.where(e8[None] == expert_ids, offsets[:, None, None], 0), axis=0) + r8
        xs = _dispatch(x1p, pos8, n_tiles * EXPERT_TM)
        ys = _experts(xs, tile_expert, tile_rows, n_valid,
                      expert_w_gate.reshape(depth * N_EXPERTS, d, EXPERT_FF),
                      expert_w_up.reshape(depth * N_EXPERTS, d, EXPERT_FF),
                      expert_w_down.reshape(depth * N_EXPERTS, EXPERT_FF, d), l)
        xf, xb = _combine(ys, pos8, w8.T, x1, x1b, shared_w_gate[l].astype(BF16),
                          shared_w_up[l].astype(BF16), shared_w_down[l].astype(BF16),
                          ln_ffn_g[l], ln_ffn_b[l])
    return xf.reshape(b, s, d)
```

```python
import functools
import math

import jax
import jax.numpy as jnp
from jax import lax
from jax.experimental import pallas as pl
from jax.experimental.pallas import tpu as pltpu

F32 = jnp.float32
BF16 = jnp.bfloat16
I32 = jnp.int32
U32 = jnp.uint32

D_MODEL = 2048
DEPTH = 2
ROPE_THETA = 500000.0
NORM_EPS = 1e-5

DIFF_HEADS = 4
DIFF_HALF_DIM = 64
DIFF_ROT = 16
MLA_HEADS = 8
MLA_Q_RANK = 512
MLA_KV_RANK = 256
MLA_NOPE = 128
MLA_ROPE = 64
MLA_V = 128
MOBA_HEADS = 4
MOBA_HEAD_DIM = 128
MOBA_BLOCK = 256
MOBA_TOPK = 3
MOBA_ROT = 32
A_WIDTH = 512
B_WIDTH = 1024
C_WIDTH = 512

N_EXPERTS = 64
N_GROUPS = 8
GROUP_SIZE = N_EXPERTS // N_GROUPS
TOPK_GROUPS = 4
TOPK = 8
EXPERT_FF = 512
ROUTED_SCALE = 2.5

DEEPNORM_ALPHA = (2 * DEPTH) ** 0.25

LANES = 128
NEG_BIG = -1e30

PROJ_TN = 512
MODE_PLAIN, MODE_ROPE_DIFF, MODE_ROPE_MLA, MODE_ROPE_MOBA, MODE_RMS, MODE_CKV, MODE_SIGMOID = range(7)
_ROPE_OF_MODE = {MODE_ROPE_DIFF: (0, DIFF_ROT // 2), MODE_ROPE_MLA: (1, MLA_ROPE // 2),
                 MODE_ROPE_MOBA: (2, MOBA_ROT // 2)}
_ROPE_PERIOD = (DIFF_HALF_DIM, MLA_ROPE, MOBA_HEAD_DIM)
_ROPE_ROT = (DIFF_ROT, MLA_ROPE, MOBA_ROT)

IN_TILE_MODES = ([MODE_ROPE_DIFF, MODE_ROPE_DIFF, MODE_PLAIN, MODE_RMS, MODE_CKV,
                  MODE_ROPE_MOBA, MODE_ROPE_MOBA, MODE_PLAIN] + [MODE_SIGMOID] * 12)
IN_TILE_SCALES = [DIFF_HALF_DIM ** -0.5] + [1.0] * 4 + [MOBA_HEAD_DIM ** -0.5] + [1.0] * 14
IN_COLS_PADDED = PROJ_TN * len(IN_TILE_MODES)

VMEM_LIMIT = 56 * 1024 * 1024

ATT_TQ = 256
PROJ_TM = 2048
MERGE_TM = 256
EXPERT_TM = 512
ROW_TM = 256
ROUTER_TM = 1024


def _cparams(sem):
    return pltpu.CompilerParams(dimension_semantics=sem, vmem_limit_bytes=VMEM_LIMIT)


def _rope_table_kernel(pos_ref, c_ref, cos_ref, sin_ref):
    pos = pos_ref[...].astype(F32)
    for p in range(3):
        ang = pos * c_ref[p:p + 1, :]
        cos_ref[p] = jnp.cos(ang)
        sin_ref[p] = jnp.sin(ang) * c_ref[3 + p:4 + p, :]


def _rope_tables(positions):
    t = positions.size
    lane = jnp.arange(LANES)
    rows = []
    signs = []
    for period, rot in zip(_ROPE_PERIOD, _ROPE_ROT):
        half = rot // 2
        inv_freq = ROPE_THETA ** (-jnp.arange(0, rot, 2, dtype=F32) / rot)
        cp = lane % period
        active = cp < rot
        rows.append(jnp.where(active, inv_freq[cp % half], 0.0))
        signs.append(jnp.where(active, jnp.where(cp < half, -1.0, 1.0), 0.0))
    consts = jnp.stack(rows + signs + [jnp.zeros((LANES,), F32)] * 2).astype(F32)
    tm = min(1024, t)
    cos, sin = pl.pallas_call(
        _rope_table_kernel,
        grid=(t // tm,),
        in_specs=[pl.BlockSpec((tm, 1), lambda i: (i, 0)),
                  pl.BlockSpec((8, LANES), lambda i: (0, 0))],
        out_specs=[pl.BlockSpec((3, tm, LANES), lambda i: (0, i, 0)),
                   pl.BlockSpec((3, tm, LANES), lambda i: (0, i, 0))],
        out_shape=[jax.ShapeDtypeStruct((3, t, LANES), F32)] * 2,
        compiler_params=_cparams(("parallel",)),
        name="rope_tables",
    )(positions.reshape(t, 1), consts)
    return cos, sin


def _proj_kernel(mode_ref, scale_ref, x_ref, w_ref, cos_ref, sin_ref, g_ref, o_ref, *, w_is_nk):
    j = pl.program_id(1)
    mode = mode_ref[j]
    if w_is_nk:
        y = lax.dot_general(x_ref[...], w_ref[...], (((1,), (1,)), ((), ())), preferred_element_type=F32)
    else:
        y = jnp.dot(x_ref[...], w_ref[...], preferred_element_type=F32)
    y = y * scale_ref[j]
    tm, tn = y.shape
    lane = lax.broadcasted_iota(I32, (tm, LANES), 1)

    def rope_chunk(yc, table, half):
        first = (lane % _ROPE_PERIOD[table]) < half
        swapped = jnp.where(first, pltpu.roll(yc, LANES - half, 1), pltpu.roll(yc, half, 1))
        return yc * cos_ref[table] + swapped * sin_ref[table]

    @pl.when(mode == MODE_PLAIN)
    def _():
        o_ref[...] = y.astype(o_ref.dtype)

    for rope_mode, (table, half) in _ROPE_OF_MODE.items():
        @pl.when(mode == rope_mode)
        def _(table=table, half=half):
            for c in range(tn // LANES):
                sl = slice(c * LANES, (c + 1) * LANES)
                o_ref[:, sl] = rope_chunk(y[:, sl], table, half).astype(o_ref.dtype)

    @pl.when(mode == MODE_RMS)
    def _():
        r = lax.rsqrt(jnp.mean(y * y, axis=-1, keepdims=True) + NORM_EPS)
        o_ref[...] = (y * r * g_ref[0:1, :]).astype(o_ref.dtype)

    @pl.when(mode == MODE_CKV)
    def _():
        ckv = y[:, :MLA_KV_RANK]
        r = lax.rsqrt(jnp.mean(ckv * ckv, axis=-1, keepdims=True) + NORM_EPS)
        o_ref[:, :MLA_KV_RANK] = (ckv * r * g_ref[1:2, :MLA_KV_RANK]).astype(o_ref.dtype)
        sl = slice(MLA_KV_RANK, MLA_KV_RANK + LANES)
        table, half = _ROPE_OF_MODE[MODE_ROPE_MLA]
        o_ref[:, sl] = rope_chunk(y[:, sl], table, half).astype(o_ref.dtype)
        o_ref[:, MLA_KV_RANK + LANES:] = jnp.zeros((tm, tn - MLA_KV_RANK - LANES), o_ref.dtype)

    @pl.when(mode == MODE_SIGMOID)
    def _():
        o_ref[...] = (0.5 * jnp.tanh(0.5 * y) + 0.5).astype(o_ref.dtype)


def _proj(x, x_col_block, k_dim, w, modes, scales, cos, sin, gains, tm, w_is_nk=False):
    t = x.shape[0]
    n = w.shape[0] if w_is_nk else w.shape[1]
    assert n % PROJ_TN == 0 and len(modes) == len(scales) == n // PROJ_TN
    assert w.shape == ((n, k_dim) if w_is_nk else (k_dim, n))
    w_spec = (pl.BlockSpec((PROJ_TN, k_dim), lambda i, j, m: (j, 0)) if w_is_nk
              else pl.BlockSpec((k_dim, PROJ_TN), lambda i, j, m: (0, j)))
    grid_spec = pltpu.PrefetchScalarGridSpec(
        num_scalar_prefetch=1,
        grid=(t // tm, n // PROJ_TN),
        in_specs=[pl.BlockSpec(memory_space=pltpu.SMEM),
                  pl.BlockSpec((tm, k_dim), lambda i, j, m: (i, x_col_block)),
                  w_spec,
                  pl.BlockSpec((3, tm, LANES), lambda i, j, m: (0, i, 0)),
                  pl.BlockSpec((3, tm, LANES), lambda i, j, m: (0, i, 0)),
                  pl.BlockSpec((8, PROJ_TN), lambda i, j, m: (0, 0))],
        out_specs=pl.BlockSpec((tm, PROJ_TN), lambda i, j, m: (i, j)),
    )
    return pl.pallas_call(
        functools.partial(_proj_kernel, w_is_nk=w_is_nk),
        grid_spec=grid_spec,
        out_shape=jax.ShapeDtypeStruct((t, n), BF16),
        compiler_params=_cparams(("parallel", "arbitrary")),
        name="proj",
    )(jnp.asarray(modes, I32), jnp.asarray(scales, F32), x, w, cos, sin, gains)


_NT_DIMS = (((1,), (1,)), ((), ()))


def _causal_mask(tq):
    row = lax.broadcasted_iota(I32, (tq, tq), 0)
    col = lax.broadcasted_iota(I32, (tq, tq), 1)
    return col <= row


def _score_strip(s_ref, i, tq, block_scores, block_mask):
    causal = _causal_mask(tq)
    for j in range(i + 1):
        sc = block_scores(j)
        if j == i:
            sc = jnp.where(causal, sc, NEG_BIG)
        elif block_mask is not None:
            sc = jnp.where(block_mask(j), sc, NEG_BIG)
        s_ref[:, j * tq:(j + 1) * tq] = sc


def _softmax_times_v(s_ref, p_ref, v_ref, n):
    tq = s_ref.shape[0]
    nch = n // LANES
    mrun = s_ref[:, 0:LANES]
    for c in range(1, nch):
        mrun = jnp.maximum(mrun, s_ref[:, c * LANES:(c + 1) * LANES])
    m = jnp.broadcast_to(jnp.max(mrun, axis=-1, keepdims=True), (tq, LANES))
    lrun = jnp.zeros((tq, LANES), F32)
    for c in range(nch):
        sl = slice(c * LANES, (c + 1) * LANES)
        p = jnp.exp(s_ref[:, sl] - m)
        lrun = lrun + p
        p_ref[:, sl] = p.astype(BF16)
    l = jnp.sum(lrun, axis=-1, keepdims=True)
    o = jnp.dot(p_ref[:, :n], v_ref[0:n, :], preferred_element_type=F32)
    return o / l


def _diff_attn_kernel(lam_ref, g_ref, q_ref, k_ref, v_ref, o_ref, s1_ref, p1_ref, s2_ref, p2_ref,
                      *, tq, lambda_init):
    nq = q_ref.shape[0] // tq
    lp = lam_ref[...]
    lam = (jnp.exp(jnp.sum(lp[0:1] * lp[1:2], axis=-1, keepdims=True))
           - jnp.exp(jnp.sum(lp[2:3] * lp[3:4], axis=-1, keepdims=True)) + lambda_init)
    lane = lax.broadcasted_iota(I32, (tq, LANES), 1)
    for i in range(nq):
        q = q_ref[i * tq:(i + 1) * tq, :]
        q1 = jnp.where(lane < DIFF_HALF_DIM, q, jnp.zeros_like(q))
        q2 = jnp.where(lane >= DIFF_HALF_DIM, q, jnp.zeros_like(q))
        kblk = lambda j: k_ref[j * tq:(j + 1) * tq, :]
        b = i % 2
        _score_strip(s1_ref.at[b], i, tq,
                     lambda j: lax.dot_general(q1, kblk(j), _NT_DIMS, preferred_element_type=F32), None)
        _score_strip(s2_ref.at[b], i, tq,
                     lambda j: lax.dot_general(q2, kblk(j), _NT_DIMS, preferred_element_type=F32), None)
        n = (i + 1) * tq
        o = (_softmax_times_v(s1_ref.at[b], p1_ref.at[b], v_ref, n)
             - lam * _softmax_times_v(s2_ref.at[b], p2_ref.at[b], v_ref, n))
        r = lax.rsqrt(jnp.mean(o * o, axis=-1, keepdims=True) + NORM_EPS)
        o_ref[i * tq:(i + 1) * tq, :] = (o * r * g_ref[...] * (1.0 - lambda_init)).astype(o_ref.dtype)


def _diff_attention(proj, lam_parts, subln_g, b, s, lambda_init):
    t = b * s
    tq = min(ATT_TQ, s)
    kern = functools.partial(_diff_attn_kernel, tq=tq, lambda_init=lambda_init)
    return pl.pallas_call(
        kern,
        grid=(b, DIFF_HEADS),
        in_specs=[pl.BlockSpec((4, DIFF_HALF_DIM), lambda bi, h: (0, 0)),
                  pl.BlockSpec((1, LANES), lambda bi, h: (0, 0)),
                  pl.BlockSpec((s, LANES), lambda bi, h: (bi, h)),
                  pl.BlockSpec((s, LANES), lambda bi, h: (bi, DIFF_HEADS + h)),
                  pl.BlockSpec((s, LANES), lambda bi, h: (bi, 2 * DIFF_HEADS + h))],
        out_specs=pl.BlockSpec((s, LANES), lambda bi, h: (bi, h)),
        out_shape=jax.ShapeDtypeStruct((t, A_WIDTH), BF16),
        scratch_shapes=[pltpu.VMEM((2, tq, s), F32), pltpu.VMEM((2, tq, s), BF16),
                        pltpu.VMEM((2, tq, s), F32), pltpu.VMEM((2, tq, s), BF16)],
        compiler_params=_cparams(("parallel", "parallel")),
        name="diff_attention",
    )(lam_parts, subln_g.reshape(1, LANES), proj, proj, proj)


def _mla_attn_kernel(qn_ref, qr_ref, kn_ref, kpe_ref, v_ref, o_ref, s_ref, p_ref, *, tq):
    h = pl.program_id(1)
    nq = qn_ref.shape[0] // tq
    lane = lax.broadcasted_iota(I32, (tq, LANES), 1)
    mine = (lane >= MLA_ROPE) == (h % 2 == 1)
    for i in range(nq):
        qn = qn_ref[i * tq:(i + 1) * tq, :]
        qr = qr_ref[i * tq:(i + 1) * tq, :]
        qr = jnp.where(mine, qr, jnp.zeros_like(qr))

        qcat = jnp.concatenate([qn, qr], axis=1)

        def block_scores(j, qcat=qcat):
            rows = slice(j * tq, (j + 1) * tq)
            kcat = jnp.concatenate([kn_ref[rows, :], kpe_ref[rows, :]], axis=1)
            return lax.dot_general(qcat, kcat, _NT_DIMS, preferred_element_type=F32)

        _score_strip(s_ref.at[i % 2], i, tq, block_scores, None)
        o = _softmax_times_v(s_ref.at[i % 2], p_ref.at[i % 2], v_ref, (i + 1) * tq)
        o_ref[i * tq:(i + 1) * tq, :] = o.astype(o_ref.dtype)


def _mla_attention(qbuf, kvbuf, proj, b, s):
    t = b * s
    tq = min(ATT_TQ, s)
    kpe_block = (4 * PROJ_TN + MLA_KV_RANK) // LANES
    return pl.pallas_call(
        functools.partial(_mla_attn_kernel, tq=tq),
        grid=(b, MLA_HEADS),
        in_specs=[pl.BlockSpec((s, LANES), lambda bi, h: (bi, h)),
                  pl.BlockSpec((s, LANES), lambda bi, h: (bi, MLA_HEADS + h // 2)),
                  pl.BlockSpec((s, LANES), lambda bi, h: (bi, h)),
                  pl.BlockSpec((s, LANES), lambda bi, h: (bi, kpe_block)),
                  pl.BlockSpec((s, LANES), lambda bi, h: (bi, MLA_HEADS + h))],
        out_specs=pl.BlockSpec((s, LANES), lambda bi, h: (bi, h)),
        out_shape=jax.ShapeDtypeStruct((t, B_WIDTH), BF16),
        scratch_shapes=[pltpu.VMEM((2, tq, s), F32), pltpu.VMEM((2, tq, s), BF16)],
        compiler_params=_cparams(("parallel", "parallel")),
        name="mla_attention",
    )(qbuf, qbuf, kvbuf, proj, kvbuf)


def _moba_attn_kernel(q_ref, k_ref, v_ref, o_ref, s_ref, p_ref, kmean_ref):
    tq = MOBA_BLOCK
    nb = q_ref.shape[0] // tq
    kmean_ref[...] = jnp.zeros(kmean_ref.shape, F32)
    for j in range(nb):
        kb = k_ref[j * tq:(j + 1) * tq, :].astype(F32)
        kmean_ref[j:j + 1, :] = jnp.sum(kb, axis=0, keepdims=True) * (1.0 / tq)
    km = kmean_ref[...]
    km_hi = km.astype(BF16)
    km_lo = (km - km_hi.astype(F32)).astype(BF16)
    nbp = -(-nb // 8) * 8
    blk = lax.broadcasted_iota(I32, (nbp, tq), 0)
    for i in range(nb):
        q = q_ref[i * tq:(i + 1) * tq, :]
        gate = (lax.dot_general(km_hi, q, _NT_DIMS, preferred_element_type=F32)
                + lax.dot_general(km_lo, q, _NT_DIMS, preferred_element_type=F32))[:nbp, :]
        fully_past = blk < i
        g = jnp.where(fully_past, gate, -jnp.inf)
        cnt = jnp.zeros(g.shape, F32)
        for jp in range(i):
            row = g[jp:jp + 1, :]
            ahead = (row > g) | ((row == g) & (jp < blk))
            cnt = cnt + jnp.where(ahead, 1.0, 0.0)
        kept_t = jnp.where((cnt < MOBA_TOPK) & fully_past, 1.0, 0.0)
        kept_t = jnp.concatenate([kept_t, jnp.zeros((LANES - nbp, tq), F32)], axis=0)
        kept = kept_t.T > 0.5
        _score_strip(s_ref.at[i % 2], i, tq,
                     lambda j, q=q: lax.dot_general(q, k_ref[j * tq:(j + 1) * tq, :], _NT_DIMS,
                                                    preferred_element_type=F32),
                     lambda j, kept=kept: kept[:, j:j + 1])
        o = _softmax_times_v(s_ref.at[i % 2], p_ref.at[i % 2], v_ref, (i + 1) * tq)
        o_ref[i * tq:(i + 1) * tq, :] = o.astype(o_ref.dtype)


def _moba_attention(proj, b, s):
    assert s % MOBA_BLOCK == 0 and s // MOBA_BLOCK <= LANES
    t = b * s
    base = 5 * PROJ_TN // LANES
    return pl.pallas_call(
        _moba_attn_kernel,
        grid=(b, MOBA_HEADS),
        in_specs=[pl.BlockSpec((s, LANES), lambda bi, h: (bi, base + h)),
                  pl.BlockSpec((s, LANES), lambda bi, h: (bi, base + MOBA_HEADS + h)),
                  pl.BlockSpec((s, LANES), lambda bi, h: (bi, base + 2 * MOBA_HEADS + h))],
        out_specs=pl.BlockSpec((s, LANES), lambda bi, h: (bi, h)),
        out_shape=jax.ShapeDtypeStruct((t, C_WIDTH), BF16),
        scratch_shapes=[pltpu.VMEM((2, MOBA_BLOCK, s), F32), pltpu.VMEM((2, MOBA_BLOCK, s), BF16),
                        pltpu.VMEM((LANES, LANES), F32)],
        compiler_params=_cparams(("parallel", "parallel")),
        name="moba_attention",
    )(proj, proj, proj)


def _layer_norm_rows(z, g, b):
    mu = jnp.mean(z, axis=-1, keepdims=True)
    zc = z - mu
    var = jnp.mean(zc * zc, axis=-1, keepdims=True)
    return zc * lax.rsqrt(var + NORM_EPS) * g + b


def _pack_halves(z):
    n = z.shape[1] // 2
    lo = lax.bitcast_convert_type(z[:, :n].astype(BF16).astype(F32), U32)
    hi = lax.bitcast_convert_type(z[:, n:].astype(BF16).astype(F32), U32)
    return (lo >> 16) | hi


def _unpack_halves(u):
    lo = lax.bitcast_convert_type(u << 16, F32)
    hi = lax.bitcast_convert_type(u & jnp.uint32(0xFFFF0000), F32)
    return lo, hi


ROW_WORDS = D_MODEL // 2
ROW_SUBLANES = ROW_WORDS // LANES


def _store_row_tiles(ref, packed):
    tm = packed.shape[0]
    for a in range(ROW_SUBLANES):
        ref[pl.ds(a, tm, stride=ROW_SUBLANES), :] = packed[:, a * LANES:(a + 1) * LANES]


def _load_row_tile_chunk(ref, a, tm):
    return ref[pl.ds(a, tm, stride=ROW_SUBLANES), :]


def _merge_kernel(oa_ref, ob_ref, oc_ref, g0_ref, g1_ref, g2_ref, x_ref, wb_ref, wo_ref,
                  lng_ref, lnb_ref, xo_ref, xb_ref, xp_ref):
    ya = jnp.dot(oa_ref[...], wb_ref[:A_WIDTH, :], preferred_element_type=F32)
    yb = jnp.dot(ob_ref[...], wb_ref[A_WIDTH:A_WIDTH + B_WIDTH, :], preferred_element_type=F32)
    yc = jnp.dot(oc_ref[...], wb_ref[A_WIDTH + B_WIDTH:, :], preferred_element_type=F32)
    y = g0_ref[...].astype(F32) * ya + g1_ref[...].astype(F32) * yb + g2_ref[...].astype(F32) * yc
    mix = jnp.dot(y.astype(BF16), wo_ref[...], preferred_element_type=F32)
    z = _layer_norm_rows(DEEPNORM_ALPHA * x_ref[...] + mix, lng_ref[...], lnb_ref[...])
    xo_ref[...] = z
    xb_ref[...] = z.astype(BF16)
    _store_row_tiles(xp_ref, _pack_halves(z))


def _merge(oa, ob, oc, proj, x, w_branch, w_out, ln_g, ln_b):
    t = x.shape[0]
    tm = min(MERGE_TM, t)
    gate_base = 8 * PROJ_TN // D_MODEL
    row = lambda i: (i, 0)
    whole = lambda i: (0, 0)
    resident = dict(pipeline_mode=pl.Buffered(1))
    return pl.pallas_call(
        _merge_kernel,
        grid=(t // tm,),
        in_specs=[pl.BlockSpec((tm, A_WIDTH), row),
                  pl.BlockSpec((tm, B_WIDTH), row),
                  pl.BlockSpec((tm, C_WIDTH), row),
                  pl.BlockSpec((tm, D_MODEL), lambda i: (i, gate_base)),
                  pl.BlockSpec((tm, D_MODEL), lambda i: (i, gate_base + 1)),
                  pl.BlockSpec((tm, D_MODEL), lambda i: (i, gate_base + 2)),
                  pl.BlockSpec((tm, D_MODEL), row),
                  pl.BlockSpec((D_MODEL, D_MODEL), whole, **resident),
                  pl.BlockSpec((D_MODEL, D_MODEL), whole, **resident),
                  pl.BlockSpec((1, D_MODEL), whole),
                  pl.BlockSpec((1, D_MODEL), whole)],
        out_specs=[pl.BlockSpec((tm, D_MODEL), row),
                   pl.BlockSpec((tm, D_MODEL), row),
                   pl.BlockSpec((tm * ROW_SUBLANES, LANES), row)],
        out_shape=[jax.ShapeDtypeStruct((t, D_MODEL), F32),
                   jax.ShapeDtypeStruct((t, D_MODEL), BF16),
                   jax.ShapeDtypeStruct((t * ROW_SUBLANES, LANES), U32)],
        compiler_params=_cparams(("parallel",)),
        name="merge_outproj_ln",
    )(oa, ob, oc, proj, proj, proj, x, w_branch, w_out, ln_g.reshape(1, -1), ln_b.reshape(1, -1))


def _split_bf16(a):
    hi = a.astype(BF16)
    lo = (a - hi.astype(F32)).astype(BF16)
    return hi, lo


def _router_kernel(x_ref, wt_ref, bias_ref, upper_ref, lower_ref,
                   e8_ref, r8_ref, w8_ref, cnt_ref, carry_ref):
    i = pl.program_id(0)

    @pl.when(i == 0)
    def _():
        carry_ref[...] = jnp.zeros(carry_ref.shape, F32)

    xh, xl = _split_bf16(x_ref[...])
    wh, wl = _split_bf16(wt_ref[...])
    logits = (lax.dot_general(wh, xh, _NT_DIMS, preferred_element_type=F32)
              + lax.dot_general(wh, xl, _NT_DIMS, preferred_element_type=F32)
              + lax.dot_general(wl, xh, _NT_DIMS, preferred_element_type=F32))
    scores = 1.0 / (1.0 + jnp.exp(-logits))
    choice = scores + bias_ref[:, 0:1]
    tm = choice.shape[1]
    sub = lax.broadcasted_iota(I32, (GROUP_SIZE, tm), 0)

    group_rows = []
    for g in range(N_GROUPS):
        cg = choice[g * GROUP_SIZE:(g + 1) * GROUP_SIZE, :]
        m1 = jnp.max(cg, axis=0, keepdims=True)
        first = jnp.min(jnp.where(cg == m1, sub, GROUP_SIZE), axis=0, keepdims=True)
        m2 = jnp.max(jnp.where(sub == first, -jnp.inf, cg), axis=0, keepdims=True)
        group_rows.append(m1 + m2)
    gs = jnp.concatenate(group_rows, axis=0)
    ahead = jnp.zeros(gs.shape, F32)
    for gp in range(N_GROUPS):
        rowv = gs[gp:gp + 1, :]
        ahead = ahead + jnp.where((rowv > gs) | ((rowv == gs) & (gp < sub)), 1.0, 0.0)
    keep_group = ahead < TOPK_GROUPS
    masked = jnp.concatenate(
        [jnp.where(keep_group[g:g + 1, :], choice[g * GROUP_SIZE:(g + 1) * GROUP_SIZE, :], -jnp.inf)
         for g in range(N_GROUPS)], axis=0)

    eidx = lax.broadcasted_iota(I32, masked.shape, 0)
    ahead = jnp.zeros(masked.shape, F32)
    for ep in range(N_EXPERTS):
        rowv = masked[ep:ep + 1, :]
        ahead = ahead + jnp.where((rowv > masked) | ((rowv == masked) & (ep < eidx)), 1.0, 0.0)
    sel = ahead < TOPK
    picked = jnp.where(sel, scores, 0.0)
    gates = picked / jnp.sum(picked, axis=0, keepdims=True) * ROUTED_SCALE
    self32 = jnp.where(sel, 1.0, 0.0)
    selb = self32.astype(BF16)

    rank = jnp.dot(selb, upper_ref[...], preferred_element_type=F32) + carry_ref[:, 0:1]
    slot = jnp.dot(lower_ref[...], selb, preferred_element_type=F32)
    carry_ref[...] = carry_ref[...] + jnp.sum(self32, axis=1, keepdims=True)
    cnt_ref[...] = carry_ref[...].astype(I32)

    eidf = eidx.astype(F32)
    e_rows, r_rows, w_rows = [], [], []
    for k in range(TOPK):
        hit = sel & (slot == float(k))
        e_rows.append(jnp.sum(jnp.where(hit, eidf, 0.0), axis=0, keepdims=True))
        r_rows.append(jnp.sum(jnp.where(hit, rank, 0.0), axis=0, keepdims=True))
        w_rows.append(jnp.sum(jnp.where(hit, gates, 0.0), axis=0, keepdims=True))
    e8_ref[...] = jnp.concatenate(e_rows, axis=0).astype(I32)
    r8_ref[...] = jnp.concatenate(r_rows, axis=0).astype(I32)
    w8_ref[...] = jnp.concatenate(w_rows, axis=0)


def _router(x, router_w, router_bias):
    t = x.shape[0]
    tm = min(ROUTER_TM, t)
    upper = (jnp.arange(tm)[:, None] < jnp.arange(tm)[None, :]).astype(BF16)
    lower = (jnp.arange(N_EXPERTS)[None, :] < jnp.arange(N_EXPERTS)[:, None]).astype(BF16)
    bias = jnp.broadcast_to(router_bias.astype(F32)[:, None], (N_EXPERTS, LANES))
    whole = lambda i: (0, 0)
    col = lambda i: (0, i)
    return pl.pallas_call(
        _router_kernel,
        grid=(t // tm,),
        in_specs=[pl.BlockSpec((tm, D_MODEL), lambda i: (i, 0)),
                  pl.BlockSpec((N_EXPERTS, D_MODEL), whole),
                  pl.BlockSpec((N_EXPERTS, LANES), whole),
                  pl.BlockSpec((tm, tm), whole),
                  pl.BlockSpec((N_EXPERTS, N_EXPERTS), whole)],
        out_specs=[pl.BlockSpec((TOPK, tm), col),
                   pl.BlockSpec((TOPK, tm), col),
                   pl.BlockSpec((TOPK, tm), col),
                   pl.BlockSpec((N_EXPERTS, LANES), whole)],
        out_shape=[jax.ShapeDtypeStruct((TOPK, t), I32),
                   jax.ShapeDtypeStruct((TOPK, t), I32),
                   jax.ShapeDtypeStruct((TOPK, t), F32),
                   jax.ShapeDtypeStruct((N_EXPERTS, LANES), I32)],
        scratch_shapes=[pltpu.VMEM((N_EXPERTS, LANES), F32)],
        compiler_params=_cparams(("arbitrary",)),
        name="router",
    )(x, router_w.T, bias, upper, lower)


def _row_tile(ref, r):
    return ref.at[pl.ds(pl.multiple_of(r * ROW_SUBLANES, ROW_SUBLANES), ROW_SUBLANES)]


def _dispatch_kernel(pos_ref, x_ref, xs_ref, sem):
    tm = x_ref.shape[0] // ROW_SUBLANES

    def row_copy(t, k):
        return pltpu.make_async_copy(_row_tile(x_ref, t), _row_tile(xs_ref, pos_ref[k, t]), sem)

    def start(t, c):
        for k in range(TOPK):
            row_copy(t, k).start(priority=k % 2)
        return c
    lax.fori_loop(0, tm, start, 0)

    def wait(t, c):
        for k in range(TOPK):
            row_copy(t, k).wait()
        return c
    lax.fori_loop(0, tm, wait, 0)


def _dispatch(xp, pos8, n_rows):
    t = xp.shape[0] // ROW_SUBLANES
    tm = min(ROW_TM, t)
    return pl.pallas_call(
        _dispatch_kernel,
        grid=(t // tm,),
        in_specs=[pl.BlockSpec((TOPK, tm), lambda i: (0, i), memory_space=pltpu.SMEM),
                  pl.BlockSpec((tm * ROW_SUBLANES, LANES), lambda i: (i, 0))],
        out_specs=pl.BlockSpec(memory_space=pl.ANY),
        scratch_shapes=[pltpu.SemaphoreType.DMA(())],
        out_shape=jax.ShapeDtypeStruct((n_rows * ROW_SUBLANES, LANES), U32),
        compiler_params=_cparams(("arbitrary",)),
        name="dispatch",
    )(pos8, xp)


def _expert_kernel(te_ref, rows_ref, nv_ref, xs_ref, wg_ref, wu_ref, wd_ref, ys_ref,
                   wgb_ref, wub_ref, wdb_ref):
    i = pl.program_id(0)

    @pl.when(i < nv_ref[0])
    def _():
        @pl.when((i == 0) | (te_ref[i] != te_ref[jnp.maximum(i - 1, 0)]))
        def _():
            wgb_ref[...] = wg_ref[0].astype(BF16)
            wub_ref[...] = wu_ref[0].astype(BF16)
            wdb_ref[...] = wd_ref[0].astype(BF16)

        half = D_MODEL // 2
        tm = EXPERT_TM
        u = jnp.concatenate([_load_row_tile_chunk(xs_ref, a, tm) for a in range(ROW_SUBLANES)], axis=1)
        live = lax.broadcasted_iota(I32, (tm, 1), 0) < rows_ref[i]
        lo, hi = _unpack_halves(jnp.where(live, u, jnp.zeros_like(u)))
        xl = lo.astype(BF16)
        xh = hi.astype(BF16)
        gate = (jnp.dot(xl, wgb_ref[:half, :], preferred_element_type=F32)
                + jnp.dot(xh, wgb_ref[half:, :], preferred_element_type=F32))
        up = (jnp.dot(xl, wub_ref[:half, :], preferred_element_type=F32)
              + jnp.dot(xh, wub_ref[half:, :], preferred_element_type=F32))
        hid = (gate / (1.0 + jnp.exp(-gate))) * up
        y = jnp.dot(hid.astype(BF16), wdb_ref[...], preferred_element_type=F32)
        _store_row_tiles(ys_ref, _pack_halves(y))


def _experts(xs, tile_expert, tile_rows, n_valid, w_gate, w_up, w_down, layer):
    n_rows = xs.shape[0] // ROW_SUBLANES
    n_tiles = n_rows // EXPERT_TM
    row = lambda i, te, tr, nv: (jnp.minimum(i, nv[0] - 1), 0)
    wsel = lambda i, te, tr, nv: (layer * N_EXPERTS + te[i], 0, 0)
    grid_spec = pltpu.PrefetchScalarGridSpec(
        num_scalar_prefetch=3,
        grid=(n_tiles,),
        in_specs=[pl.BlockSpec((EXPERT_TM * ROW_SUBLANES, LANES), row),
                  pl.BlockSpec((1, D_MODEL, EXPERT_FF), wsel),
                  pl.BlockSpec((1, D_MODEL, EXPERT_FF), wsel),
                  pl.BlockSpec((1, EXPERT_FF, D_MODEL), wsel)],
        out_specs=pl.BlockSpec((EXPERT_TM * ROW_SUBLANES, LANES), row),
        scratch_shapes=[pltpu.VMEM((D_MODEL, EXPERT_FF), BF16), pltpu.VMEM((D_MODEL, EXPERT_FF), BF16),
                        pltpu.VMEM((EXPERT_FF, D_MODEL), BF16)],
    )
    return pl.pallas_call(
        _expert_kernel,
        grid_spec=grid_spec,
        out_shape=jax.ShapeDtypeStruct((n_rows * ROW_SUBLANES, LANES), U32),
        compiler_params=_cparams(("arbitrary",)),
        name="experts",
    )(tile_expert, tile_rows, n_valid, xs, w_gate, w_up, w_down)


def _combine_kernel(pos_ref, pos_next_ref, w8_ref, ys_ref, x_ref, xb_ref, wsg_ref, wsu_ref, wsd_ref,
                    lng_ref, lnb_ref, xo_ref, xob_ref, buf_ref, sem):
    i = pl.program_id(0)
    tm = x_ref.shape[0]
    slot = i % 2

    def row_copy(p_ref, s, t, k):
        return pltpu.make_async_copy(_row_tile(ys_ref, p_ref[k, t]), _row_tile(buf_ref.at[s, k], t),
                                     sem.at[s])

    def start_all(p_ref, s):
        def start(t, c):
            for k in range(TOPK):
                row_copy(p_ref, s, t, k).start(priority=k % 2)
            return c
        lax.fori_loop(0, tm, start, 0)

    @pl.when(i == 0)
    def _():
        start_all(pos_ref, 0)

    @pl.when(i + 1 < pl.num_programs(0))
    def _():
        start_all(pos_next_ref, 1 - slot)

    xb = xb_ref[...]
    sg = jnp.dot(xb, wsg_ref[...], preferred_element_type=F32)
    su = jnp.dot(xb, wsu_ref[...], preferred_element_type=F32)
    hid = (sg / (1.0 + jnp.exp(-sg))) * su
    shared = jnp.dot(hid.astype(BF16), wsd_ref[...], preferred_element_type=F32)

    def wait(t, c):
        for k in range(TOPK):
            row_copy(pos_ref, slot, t, k).wait()
        return c
    lax.fori_loop(0, tm, wait, 0)

    w8 = w8_ref[...]
    wk = [jnp.broadcast_to(w8[:, k:k + 1], (tm, LANES)) for k in range(TOPK)]
    lo_parts, hi_parts = [], []
    for a in range(ROW_SUBLANES):
        acc_lo = jnp.zeros((tm, LANES), F32)
        acc_hi = jnp.zeros((tm, LANES), F32)
        for k in range(TOPK):
            lo, hi = _unpack_halves(_load_row_tile_chunk(buf_ref.at[slot, k], a, tm))
            acc_lo = acc_lo + wk[k] * lo
            acc_hi = acc_hi + wk[k] * hi
        lo_parts.append(acc_lo)
        hi_parts.append(acc_hi)
    routed = jnp.concatenate(lo_parts + hi_parts, axis=1)
    z = _layer_norm_rows(DEEPNORM_ALPHA * x_ref[...] + (routed + shared), lng_ref[...], lnb_ref[...])
    xo_ref[...] = z
    xob_ref[...] = z.astype(BF16)


def _combine(ys, pos8, w8t, x, xb, ws_gate, ws_up, ws_down, ln_g, ln_b):
    t = x.shape[0]
    tm = min(ROW_TM, t)
    row = lambda i: (i, 0)
    whole = lambda i: (0, 0)
    n_steps = t // tm
    grid_spec = pl.GridSpec(
        grid=(n_steps,),
        in_specs=[pl.BlockSpec((TOPK, tm), lambda i: (0, i), memory_space=pltpu.SMEM),
                  pl.BlockSpec((TOPK, tm), lambda i: (0, jnp.minimum(i + 1, n_steps - 1)),
                               memory_space=pltpu.SMEM),
                  pl.BlockSpec((tm, TOPK), row),
                  pl.BlockSpec(memory_space=pl.ANY),
                  pl.BlockSpec((tm, D_MODEL), row),
                  pl.BlockSpec((tm, D_MODEL), row),
                  pl.BlockSpec((D_MODEL, EXPERT_FF), whole),
                  pl.BlockSpec((D_MODEL, EXPERT_FF), whole),
                  pl.BlockSpec((EXPERT_FF, D_MODEL), whole),
                  pl.BlockSpec((1, D_MODEL), whole),
                  pl.BlockSpec((1, D_MODEL), whole)],
        out_specs=[pl.BlockSpec((tm, D_MODEL), row),
                   pl.BlockSpec((tm, D_MODEL), row)],
        scratch_shapes=[pltpu.VMEM((2, TOPK, tm * ROW_SUBLANES, LANES), U32),
                        pltpu.SemaphoreType.DMA((2,))],
    )
    return pl.pallas_call(
        _combine_kernel,
        grid_spec=grid_spec,
        out_shape=[jax.ShapeDtypeStruct((t, D_MODEL), F32),
                   jax.ShapeDtypeStruct((t, D_MODEL), BF16)],
        compiler_params=_cparams(("arbitrary",)),
        name="combine_shared_ln",
    )(pos8, pos8, w8t, ys, x, xb, ws_gate, ws_up, ws_down, ln_g.reshape(1, -1), ln_b.reshape(1, -1))


_CKV_TILE = 4
_TAIL_SHIFT = PROJ_TN - MLA_KV_RANK - MLA_ROPE


def _in_proj_weight_kernel(prev_ref, cur_ref, o_ref):
    j = pl.program_id(1)

    @pl.when(j < _CKV_TILE)
    def _():
        o_ref[...] = cur_ref[0].astype(BF16)

    @pl.when(j == _CKV_TILE)
    def _():
        w = cur_ref[0]
        kpe = w[MLA_KV_RANK:MLA_KV_RANK + MLA_ROPE, :]
        pad = jnp.zeros((_TAIL_SHIFT - MLA_ROPE, w.shape[1]), F32)
        o_ref[...] = jnp.concatenate([w[:MLA_KV_RANK, :], kpe, kpe, pad], axis=0).astype(BF16)

    @pl.when(j > _CKV_TILE)
    def _():
        keep = PROJ_TN - _TAIL_SHIFT
        o_ref[...] = jnp.concatenate([prev_ref[0][keep:, :], cur_ref[0][:keep, :]], axis=0).astype(BF16)


def _in_proj_weight(w_in_t_all, layer):
    d = w_in_t_all.shape[2]
    return pl.pallas_call(
        _in_proj_weight_kernel,
        grid=(1, IN_COLS_PADDED // PROJ_TN),
        in_specs=[pl.BlockSpec((1, PROJ_TN, d), lambda i, j: (layer, jnp.maximum(j - 1, 0), 0)),
                  pl.BlockSpec((1, PROJ_TN, d), lambda i, j: (layer, j, 0))],
        out_specs=pl.BlockSpec((PROJ_TN, d), lambda i, j: (j, 0)),
        out_shape=jax.ShapeDtypeStruct((IN_COLS_PADDED, d), BF16),
        compiler_params=_cparams(("parallel", "parallel")),
        name="in_proj_weight",
    )(w_in_t_all, w_in_t_all)


def _uq_weight(w_uq):
    w = w_uq.reshape(MLA_Q_RANK, MLA_HEADS, MLA_NOPE + MLA_ROPE)
    return jnp.concatenate([w[:, :, :MLA_NOPE].reshape(MLA_Q_RANK, -1),
                            w[:, :, MLA_NOPE:].reshape(MLA_Q_RANK, -1)], axis=1).astype(BF16)


def _ukv_weight(w_ukv):
    w = w_ukv.reshape(MLA_KV_RANK, MLA_HEADS, MLA_NOPE + MLA_V)
    return jnp.concatenate([w[:, :, :MLA_NOPE].reshape(MLA_KV_RANK, -1),
                            w[:, :, MLA_NOPE:].reshape(MLA_KV_RANK, -1)], axis=1).astype(BF16)


def _expert_tiling_kernel(cnt_ref, meta_ref, off_ref):
    ntp = meta_ref.shape[1]
    cnt = cnt_ref[...].astype(F32)
    tiles = jnp.floor((cnt + (EXPERT_TM - 1)) * (1.0 / EXPERT_TM))
    row = lax.broadcasted_iota(I32, (N_EXPERTS, N_EXPERTS), 0)
    col = lax.broadcasted_iota(I32, (N_EXPERTS, N_EXPERTS), 1)
    incl = jnp.where(col <= row, 1.0, 0.0).astype(BF16)
    ends = jnp.dot(incl, tiles.astype(BF16), preferred_element_type=F32)
    first = ends - tiles
    off_ref[...] = (first * EXPERT_TM).astype(I32)

    def widen(a):
        return jnp.concatenate([a] * (ntp // LANES), axis=1)
    tile_id = lax.broadcasted_iota(I32, (N_EXPERTS, ntp), 1).astype(F32)
    ends_w, first_w, cnt_w = widen(ends), widen(first), widen(cnt)
    tile_expert = jnp.minimum(jnp.sum(jnp.where(ends_w <= tile_id, 1.0, 0.0), axis=0, keepdims=True),
                              N_EXPERTS - 1.0)
    mine = (first_w <= tile_id) & (tile_id < ends_w)
    rows = jnp.clip(cnt_w - (tile_id - first_w) * EXPERT_TM, 0.0, float(EXPERT_TM))
    tile_rows = jnp.sum(jnp.where(mine, rows, 0.0), axis=0, keepdims=True)
    n_valid = widen(ends[N_EXPERTS - 1:N_EXPERTS, :])
    meta_ref[...] = jnp.concatenate(
        [tile_expert, tile_rows, n_valid, jnp.zeros((5, ntp), F32)], axis=0).astype(I32)


def _expert_tiling(counts, n_tiles):
    ntp = -(-n_tiles // LANES) * LANES
    meta, off = pl.pallas_call(
        _expert_tiling_kernel,
        out_shape=[jax.ShapeDtypeStruct((8, ntp), I32), jax.ShapeDtypeStruct((N_EXPERTS, LANES), I32)],
        name="expert_tiling",
    )(counts)
    return off[:, 0], meta[0, :n_tiles], meta[1, :n_tiles], meta[2, :1]


def kernel(x, positions, w_in, diff_lambda_q1, diff_lambda_k1, diff_lambda_q2, diff_lambda_k2, diff_subln_g, mla_q_norm_g, mla_w_uq, mla_kv_norm_g, mla_w_ukv, w_branch, w_out, ln_mix_g, ln_mix_b, router_w, router_bias, expert_w_gate, expert_w_up, expert_w_down, shared_w_gate, shared_w_up, shared_w_down, ln_ffn_g, ln_ffn_b):
    b, s, d = x.shape
    t = b * s
    depth = w_in.shape[0]
    cos, sin = _rope_tables(positions)
    xf = x.reshape(t, d)
    xb = xf.astype(BF16)
    w_in_t = jnp.swapaxes(w_in, 1, 2)
    n_tiles = t * TOPK // EXPERT_TM + N_EXPERTS
    proj_tm = min(PROJ_TM, t)

    for l in range(depth):
        gains = jnp.zeros((8, PROJ_TN), F32)
        gains = gains.at[0, :].set(mla_q_norm_g[l]).at[1, :MLA_KV_RANK].set(mla_kv_norm_g[l])
        proj = _proj(xb, 0, d, _in_proj_weight(w_in_t, l), IN_TILE_MODES, IN_TILE_SCALES,
                     cos, sin, gains, proj_tm, w_is_nk=True)
        qbuf = _proj(proj, 3, MLA_Q_RANK, _uq_weight(mla_w_uq[l]),
                     [MODE_PLAIN, MODE_PLAIN, MODE_ROPE_MLA], [(MLA_NOPE + MLA_ROPE) ** -0.5] * 3,
                     cos, sin, gains, proj_tm)
        kvbuf = _proj(proj, 4 * PROJ_TN // MLA_KV_RANK, MLA_KV_RANK, _ukv_weight(mla_w_ukv[l]),
                      [MODE_PLAIN] * 4, [1.0] * 4, cos, sin, gains, proj_tm)

        lambda_init = 0.8 - 0.6 * math.exp(-0.3 * l)
        lam_parts = jnp.stack([diff_lambda_q1[l], diff_lambda_k1[l],
                               diff_lambda_q2[l], diff_lambda_k2[l]]).astype(F32)
        oa = _diff_attention(proj, lam_parts, diff_subln_g[l], b, s, lambda_init)
        ob = _mla_attention(qbuf, kvbuf, proj, b, s)
        oc = _moba_attention(proj, b, s)

        x1, x1b, x1p = _merge(oa, ob, oc, proj, xf, w_branch[l].astype(BF16), w_out[l].astype(BF16),
                              ln_mix_g[l], ln_mix_b[l])

        e8, r8, w8, counts = _router(x1, router_w[l], router_bias[l])
        offsets, tile_expert, tile_rows, n_valid = _expert_tiling(counts, n_tiles)
        expert_ids = jnp.arange(N_EXPERTS, dtype=I32)[:, None, None]
        pos8 = jnp.sum(jnp.where(e8[None] == expert_ids, offsets[:, None, None], 0), axis=0) + r8
        xs = _dispatch(x1p, pos8, n_tiles * EXPERT_TM)
        ys = _experts(xs, tile_expert, tile_rows, n_valid,
                      expert_w_gate.reshape(depth * N_EXPERTS, d, EXPERT_FF),
                      expert_w_up.reshape(depth * N_EXPERTS, d, EXPERT_FF),
                      expert_w_down.reshape(depth * N_EXPERTS, EXPERT_FF, d), l)
        xf, xb = _combine(ys, pos8, w8.T, x1, x1b, shared_w_gate[l].astype(BF16),
                          shared_w_up[l].astype(BF16), shared_w_down[l].astype(BF16),
                          ln_ffn_g[l], ln_ffn_b[l])
    return xf.reshape(b, s, d)
```

```python
import functools
import math

import jax
import jax.numpy as jnp
from jax import lax
from jax.experimental import pallas as pl
from jax.experimental.pallas import tpu as pltpu

F32 = jnp.float32
BF16 = jnp.bfloat16
I32 = jnp.int32
U32 = jnp.uint32

D_MODEL = 2048
DEPTH = 2
ROPE_THETA = 500000.0
NORM_EPS = 1e-5

DIFF_HEADS = 4
DIFF_HALF_DIM = 64
DIFF_ROT = 16
MLA_HEADS = 8
MLA_Q_RANK = 512
MLA_KV_RANK = 256
MLA_NOPE = 128
MLA_ROPE = 64
MLA_V = 128
MOBA_HEADS = 4
MOBA_HEAD_DIM = 128
MOBA_BLOCK = 256
MOBA_TOPK = 3
MOBA_ROT = 32
A_WIDTH = 512
B_WIDTH = 1024
C_WIDTH = 512

N_EXPERTS = 64
N_GROUPS = 8
GROUP_SIZE = N_EXPERTS // N_GROUPS
TOPK_GROUPS = 4
TOPK = 8
EXPERT_FF = 512
ROUTED_SCALE = 2.5

DEEPNORM_ALPHA = (2 * DEPTH) ** 0.25

LANES = 128
NEG_BIG = -1e30

PROJ_TN = 512
MODE_PLAIN, MODE_ROPE_DIFF, MODE_ROPE_MLA, MODE_ROPE_MOBA, MODE_RMS, MODE_CKV, MODE_SIGMOID = range(7)
_ROPE_OF_MODE = {MODE_ROPE_DIFF: (0, DIFF_ROT // 2), MODE_ROPE_MLA: (1, MLA_ROPE // 2),
                 MODE_ROPE_MOBA: (2, MOBA_ROT // 2)}
_ROPE_PERIOD = (DIFF_HALF_DIM, MLA_ROPE, MOBA_HEAD_DIM)
_ROPE_ROT = (DIFF_ROT, MLA_ROPE, MOBA_ROT)

IN_TILE_MODES = ([MODE_ROPE_DIFF, MODE_ROPE_DIFF, MODE_PLAIN, MODE_RMS, MODE_CKV,
                  MODE_ROPE_MOBA, MODE_ROPE_MOBA, MODE_PLAIN] + [MODE_SIGMOID] * 12)
IN_TILE_SCALES = [DIFF_HALF_DIM ** -0.5] + [1.0] * 4 + [MOBA_HEAD_DIM ** -0.5] + [1.0] * 14
IN_COLS_PADDED = PROJ_TN * len(IN_TILE_MODES)

VMEM_LIMIT = 56 * 1024 * 1024

ATT_TQ = 256
PROJ_TM = 2048
MERGE_TM = 256
EXPERT_TM = 512
ROW_TM = 256
ROUTER_TM = 1024


def _cparams(sem):
    return pltpu.CompilerParams(dimension_semantics=sem, vmem_limit_bytes=VMEM_LIMIT)


def _rope_table_kernel(pos_ref, c_ref, cos_ref, sin_ref):
    pos = pos_ref[...].astype(F32)
    for p in range(3):
        ang = pos * c_ref[p:p + 1, :]
        cos_ref[p] = jnp.cos(ang)
        sin_ref[p] = jnp.sin(ang) * c_ref[3 + p:4 + p, :]


def _rope_tables(positions):
    t = positions.size
    lane = jnp.arange(LANES)
    rows = []
    signs = []
    for period, rot in zip(_ROPE_PERIOD, _ROPE_ROT):
        half = rot // 2
        inv_freq = ROPE_THETA ** (-jnp.arange(0, rot, 2, dtype=F32) / rot)
        cp = lane % period
        active = cp < rot
        rows.append(jnp.where(active, inv_freq[cp % half], 0.0))
        signs.append(jnp.where(active, jnp.where(cp < half, -1.0, 1.0), 0.0))
    consts = jnp.stack(rows + signs + [jnp.zeros((LANES,), F32)] * 2).astype(F32)
    tm = min(1024, t)
    cos, sin = pl.pallas_call(
        _rope_table_kernel,
        grid=(t // tm,),
        in_specs=[pl.BlockSpec((tm, 1), lambda i: (i, 0)),
                  pl.BlockSpec((8, LANES), lambda i: (0, 0))],
        out_specs=[pl.BlockSpec((3, tm, LANES), lambda i: (0, i, 0)),
                   pl.BlockSpec((3, tm, LANES), lambda i: (0, i, 0))],
        out_shape=[jax.ShapeDtypeStruct((3, t, LANES), F32)] * 2,
        compiler_params=_cparams(("parallel",)),
        name="rope_tables",
    )(positions.reshape(t, 1), consts)
    return cos, sin


def _proj_kernel(mode_ref, scale_ref, x_ref, w_ref, cos_ref, sin_ref, g_ref, o_ref, *, w_is_nk):
    j = pl.program_id(1)
    mode = mode_ref[j]
    if w_is_nk:
        y = lax.dot_general(x_ref[...], w_ref[...], (((1,), (1,)), ((), ())), preferred_element_type=F32)
    else:
        y = jnp.dot(x_ref[...], w_ref[...], preferred_element_type=F32)
    y = y * scale_ref[j]
    tm, tn = y.shape
    lane = lax.broadcasted_iota(I32, (tm, LANES), 1)

    def rope_chunk(yc, table, half):
        first = (lane % _ROPE_PERIOD[table]) < half
        swapped = jnp.where(first, pltpu.roll(yc, LANES - half, 1), pltpu.roll(yc, half, 1))
        return yc * cos_ref[table] + swapped * sin_ref[table]

    @pl.when(mode == MODE_PLAIN)
    def _():
        o_ref[...] = y.astype(o_ref.dtype)

    for rope_mode, (table, half) in _ROPE_OF_MODE.items():
        @pl.when(mode == rope_mode)
        def _(table=table, half=half):
            for c in range(tn // LANES):
                sl = slice(c * LANES, (c + 1) * LANES)
                o_ref[:, sl] = rope_chunk(y[:, sl], table, half).astype(o_ref.dtype)

    @pl.when(mode == MODE_RMS)
    def _():
        r = lax.rsqrt(jnp.mean(y * y, axis=-1, keepdims=True) + NORM_EPS)
        o_ref[...] = (y * r * g_ref[0:1, :]).astype(o_ref.dtype)

    @pl.when(mode == MODE_CKV)
    def _():
        ckv = y[:, :MLA_KV_RANK]
        r = lax.rsqrt(jnp.mean(ckv * ckv, axis=-1, keepdims=True) + NORM_EPS)
        o_ref[:, :MLA_KV_RANK] = (ckv * r * g_ref[1:2, :MLA_KV_RANK]).astype(o_ref.dtype)
        sl = slice(MLA_KV_RANK, MLA_KV_RANK + LANES)
        table, half = _ROPE_OF_MODE[MODE_ROPE_MLA]
        o_ref[:, sl] = rope_chunk(y[:, sl], table, half).astype(o_ref.dtype)
        o_ref[:, MLA_KV_RANK + LANES:] = jnp.zeros((tm, tn - MLA_KV_RANK - LANES), o_ref.dtype)

    @pl.when(mode == MODE_SIGMOID)
    def _():
        o_ref[...] = (0.5 * jnp.tanh(0.5 * y) + 0.5).astype(o_ref.dtype)


def _proj(x, x_col_block, k_dim, w, modes, scales, cos, sin, gains, tm, w_is_nk=False):
    t = x.shape[0]
    n = w.shape[0] if w_is_nk else w.shape[1]
    assert n % PROJ_TN == 0 and len(modes) == len(scales) == n // PROJ_TN
    assert w.shape == ((n, k_dim) if w_is_nk else (k_dim, n))
    w_spec = (pl.BlockSpec((PROJ_TN, k_dim), lambda i, j, m: (j, 0)) if w_is_nk
              else pl.BlockSpec((k_dim, PROJ_TN), lambda i, j, m: (0, j)))
    grid_spec = pltpu.PrefetchScalarGridSpec(
        num_scalar_prefetch=1,
        grid=(t // tm, n // PROJ_TN),
        in_specs=[pl.BlockSpec(memory_space=pltpu.SMEM),
                  pl.BlockSpec((tm, k_dim), lambda i, j, m: (i, x_col_block)),
                  w_spec,
                  pl.BlockSpec((3, tm, LANES), lambda i, j, m: (0, i, 0)),
                  pl.BlockSpec((3, tm, LANES), lambda i, j, m: (0, i, 0)),
                  pl.BlockSpec((8, PROJ_TN), lambda i, j, m: (0, 0))],
        out_specs=pl.BlockSpec((tm, PROJ_TN), lambda i, j, m: (i, j)),
    )
    return pl.pallas_call(
        functools.partial(_proj_kernel, w_is_nk=w_is_nk),
        grid_spec=grid_spec,
        out_shape=jax.ShapeDtypeStruct((t, n), BF16),
        compiler_params=_cparams(("parallel", "arbitrary")),
        name="proj",
    )(jnp.asarray(modes, I32), jnp.asarray(scales, F32), x, w, cos, sin, gains)


_NT_DIMS = (((1,), (1,)), ((), ()))


def _causal_mask(tq):
    row = lax.broadcasted_iota(I32, (tq, tq), 0)
    col = lax.broadcasted_iota(I32, (tq, tq), 1)
    return col <= row


def _score_strip(s_ref, i, tq, block_scores, block_mask):
    causal = _causal_mask(tq)
    for j in range(i + 1):
        sc = block_scores(j)
        if j == i:
            sc = jnp.where(causal, sc, NEG_BIG)
        elif block_mask is not None:
            sc = jnp.where(block_mask(j), sc, NEG_BIG)
        s_ref[:, j * tq:(j + 1) * tq] = sc


def _softmax_times_v(s_ref, p_ref, v_ref, n):
    tq = s_ref.shape[0]
    nch = n // LANES
    mrun = s_ref[:, 0:LANES]
    for c in range(1, nch):
        mrun = jnp.maximum(mrun, s_ref[:, c * LANES:(c + 1) * LANES])
    m = jnp.broadcast_to(jnp.max(mrun, axis=-1, keepdims=True), (tq, LANES))
    lrun = jnp.zeros((tq, LANES), F32)
    for c in range(nch):
        sl = slice(c * LANES, (c + 1) * LANES)
        p = jnp.exp(s_ref[:, sl] - m)
        lrun = lrun + p
        p_ref[:, sl] = p.astype(BF16)
    l = jnp.sum(lrun, axis=-1, keepdims=True)
    o = jnp.dot(p_ref[:, :n], v_ref[0:n, :], preferred_element_type=F32)
    return o / l


def _diff_attn_kernel(lam_ref, g_ref, q_ref, k_ref, v_ref, o_ref, s1_ref, p1_ref, s2_ref, p2_ref,
                      *, tq, lambda_init):
    nq = q_ref.shape[0] // tq
    lp = lam_ref[...]
    lam = (jnp.exp(jnp.sum(lp[0:1] * lp[1:2], axis=-1, keepdims=True))
           - jnp.exp(jnp.sum(lp[2:3] * lp[3:4], axis=-1, keepdims=True)) + lambda_init)
    lane = lax.broadcasted_iota(I32, (tq, LANES), 1)
    for i in range(nq):
        q = q_ref[i * tq:(i + 1) * tq, :]
        q1 = jnp.where(lane < DIFF_HALF_DIM, q, jnp.zeros_like(q))
        q2 = jnp.where(lane >= DIFF_HALF_DIM, q, jnp.zeros_like(q))
        kblk = lambda j: k_ref[j * tq:(j + 1) * tq, :]
        b = i % 2
        _score_strip(s1_ref.at[b], i, tq,
                     lambda j: lax.dot_general(q1, kblk(j), _NT_DIMS, preferred_element_type=F32), None)
        _score_strip(s2_ref.at[b], i, tq,
                     lambda j: lax.dot_general(q2, kblk(j), _NT_DIMS, preferred_element_type=F32), None)
        n = (i + 1) * tq
        o = (_softmax_times_v(s1_ref.at[b], p1_ref.at[b], v_ref, n)
             - lam * _softmax_times_v(s2_ref.at[b], p2_ref.at[b], v_ref, n))
        r = lax.rsqrt(jnp.mean(o * o, axis=-1, keepdims=True) + NORM_EPS)
        o_ref[i * tq:(i + 1) * tq, :] = (o * r * g_ref[...] * (1.0 - lambda_init)).astype(o_ref.dtype)


def _diff_attention(proj, lam_parts, subln_g, b, s, lambda_init):
    t = b * s
    tq = min(ATT_TQ, s)
    kern = functools.partial(_diff_attn_kernel, tq=tq, lambda_init=lambda_init)
    return pl.pallas_call(
        kern,
        grid=(b, DIFF_HEADS),
        in_specs=[pl.BlockSpec((4, DIFF_HALF_DIM), lambda bi, h: (0, 0)),
                  pl.BlockSpec((1, LANES), lambda bi, h: (0, 0)),
                  pl.BlockSpec((s, LANES), lambda bi, h: (bi, h)),
                  pl.BlockSpec((s, LANES), lambda bi, h: (bi, DIFF_HEADS + h)),
                  pl.BlockSpec((s, LANES), lambda bi, h: (bi, 2 * DIFF_HEADS + h))],
        out_specs=pl.BlockSpec((s, LANES), lambda bi, h: (bi, h)),
        out_shape=jax.ShapeDtypeStruct((t, A_WIDTH), BF16),
        scratch_shapes=[pltpu.VMEM((2, tq, s), F32), pltpu.VMEM((2, tq, s), BF16),
                        pltpu.VMEM((2, tq, s), F32), pltpu.VMEM((2, tq, s), BF16)],
        compiler_params=_cparams(("parallel", "parallel")),
        name="diff_attention",
    )(lam_parts, subln_g.reshape(1, LANES), proj, proj, proj)


def _mla_attn_kernel(qn_ref, qr_ref, kn_ref, kpe_ref, v_ref, o_ref, s_ref, p_ref, *, tq):
    h = pl.program_id(1)
    nq = qn_ref.shape[0] // tq
    lane = lax.broadcasted_iota(I32, (tq, LANES), 1)
    mine = (lane >= MLA_ROPE) == (h % 2 == 1)
    for i in range(nq):
        qn = qn_ref[i * tq:(i + 1) * tq, :]
        qr = qr_ref[i * tq:(i + 1) * tq, :]
        qr = jnp.where(mine, qr, jnp.zeros_like(qr))

        qcat = jnp.concatenate([qn, qr], axis=1)

        def block_scores(j, qcat=qcat):
            rows = slice(j * tq, (j + 1) * tq)
            kcat = jnp.concatenate([kn_ref[rows, :], kpe_ref[rows, :]], axis=1)
            return lax.dot_general(qcat, kcat, _NT_DIMS, preferred_element_type=F32)

        _score_strip(s_ref.at[i % 2], i, tq, block_scores, None)
        o = _softmax_times_v(s_ref.at[i % 2], p_ref.at[i % 2], v_ref, (i + 1) * tq)
        o_ref[i * tq:(i + 1) * tq, :] = o.astype(o_ref.dtype)


def _mla_attention(qbuf, kvbuf, proj, b, s):
    t = b * s
    tq = min(ATT_TQ, s)
    kpe_block = (4 * PROJ_TN + MLA_KV_RANK) // LANES
    return pl.pallas_call(
        functools.partial(_mla_attn_kernel, tq=tq),
        grid=(b, MLA_HEADS),
        in_specs=[pl.BlockSpec((s, LANES), lambda bi, h: (bi, h)),
                  pl.BlockSpec((s, LANES), lambda bi, h: (bi, MLA_HEADS + h // 2)),
                  pl.BlockSpec((s, LANES), lambda bi, h: (bi, h)),
                  pl.BlockSpec((s, LANES), lambda bi, h: (bi, kpe_block)),
                  pl.BlockSpec((s, LANES), lambda bi, h: (bi, MLA_HEADS + h))],
        out_specs=pl.BlockSpec((s, LANES), lambda bi, h: (bi, h)),
        out_shape=jax.ShapeDtypeStruct((t, B_WIDTH), BF16),
        scratch_shapes=[pltpu.VMEM((2, tq, s), F32), pltpu.VMEM((2, tq, s), BF16)],
        compiler_params=_cparams(("parallel", "parallel")),
        name="mla_attention",
    )(qbuf, qbuf, kvbuf, proj, kvbuf)


def _moba_attn_kernel(q_ref, k_ref, v_ref, o_ref, s_ref, p_ref, kmean_ref):
    tq = MOBA_BLOCK
    nb = q_ref.shape[0] // tq
    kmean_ref[...] = jnp.zeros(kmean_ref.shape, F32)
    for j in range(nb):
        kb = k_ref[j * tq:(j + 1) * tq, :].astype(F32)
        kmean_ref[j:j + 1, :] = jnp.sum(kb, axis=0, keepdims=True) * (1.0 / tq)
    km = kmean_ref[...]
    km_hi = km.astype(BF16)
    km_lo = (km - km_hi.astype(F32)).astype(BF16)
    nbp = -(-nb // 8) * 8
    blk = lax.broadcasted_iota(I32, (nbp, tq), 0)
    for i in range(nb):
        q = q_ref[i * tq:(i + 1) * tq, :]
        gate = (lax.dot_general(km_hi, q, _NT_DIMS, preferred_element_type=F32)
                + lax.dot_general(km_lo, q, _NT_DIMS, preferred_element_type=F32))[:nbp, :]
        fully_past = blk < i
        g = jnp.where(fully_past, gate, -jnp.inf)
        cnt = jnp.zeros(g.shape, F32)
        for jp in range(i):
            row = g[jp:jp + 1, :]
            ahead = (row > g) | ((row == g) & (jp < blk))
            cnt = cnt + jnp.where(ahead, 1.0, 0.0)
        kept_t = jnp.where((cnt < MOBA_TOPK) & fully_past, 1.0, 0.0)
        kept_t = jnp.concatenate([kept_t, jnp.zeros((LANES - nbp, tq), F32)], axis=0)
        kept = kept_t.T > 0.5
        _score_strip(s_ref.at[i % 2], i, tq,
                     lambda j, q=q: lax.dot_general(q, k_ref[j * tq:(j + 1) * tq, :], _NT_DIMS,
                                                    preferred_element_type=F32),
                     lambda j, kept=kept: kept[:, j:j + 1])
        o = _softmax_times_v(s_ref.at[i % 2], p_ref.at[i % 2], v_ref, (i + 1) * tq)
        o_ref[i * tq:(i + 1) * tq, :] = o.astype(o_ref.dtype)


def _moba_attention(proj, b, s):
    assert s % MOBA_BLOCK == 0 and s // MOBA_BLOCK <= LANES
    t = b * s
    base = 5 * PROJ_TN // LANES
    return pl.pallas_call(
        _moba_attn_kernel,
        grid=(b, MOBA_HEADS),
        in_specs=[pl.BlockSpec((s, LANES), lambda bi, h: (bi, base + h)),
                  pl.BlockSpec((s, LANES), lambda bi, h: (bi, base + MOBA_HEADS + h)),
                  pl.BlockSpec((s, LANES), lambda bi, h: (bi, base + 2 * MOBA_HEADS + h))],
        out_specs=pl.BlockSpec((s, LANES), lambda bi, h: (bi, h)),
        out_shape=jax.ShapeDtypeStruct((t, C_WIDTH), BF16),
        scratch_shapes=[pltpu.VMEM((2, MOBA_BLOCK, s), F32), pltpu.VMEM((2, MOBA_BLOCK, s), BF16),
                        pltpu.VMEM((LANES, LANES), F32)],
        compiler_params=_cparams(("parallel", "parallel")),
        name="moba_attention",
    )(proj, proj, proj)


def _layer_norm_rows(z, g, b):
    mu = jnp.mean(z, axis=-1, keepdims=True)
    zc = z - mu
    var = jnp.mean(zc * zc, axis=-1, keepdims=True)
    return zc * lax.rsqrt(var + NORM_EPS) * g + b


def _pack_halves(z):
    n = z.shape[1] // 2
    lo = lax.bitcast_convert_type(z[:, :n].astype(BF16).astype(F32), U32)
    hi = lax.bitcast_convert_type(z[:, n:].astype(BF16).astype(F32), U32)
    return (lo >> 16) | hi


def _unpack_halves(u):
    lo = lax.bitcast_convert_type(u << 16, F32)
    hi = lax.bitcast_convert_type(u & jnp.uint32(0xFFFF0000), F32)
    return lo, hi


ROW_WORDS = D_MODEL // 2
ROW_SUBLANES = ROW_WORDS // LANES


def _store_row_tiles(ref, packed):
    tm = packed.shape[0]
    for a in range(ROW_SUBLANES):
        ref[pl.ds(a, tm, stride=ROW_SUBLANES), :] = packed[:, a * LANES:(a + 1) * LANES]


def _load_row_tile_chunk(ref, a, tm):
    return ref[pl.ds(a, tm, stride=ROW_SUBLANES), :]


def _merge_kernel(oa_ref, ob_ref, oc_ref, g0_ref, g1_ref, g2_ref, x_ref, wb_ref, wo_ref,
                  lng_ref, lnb_ref, xo_ref, xb_ref, xp_ref):
    ya = jnp.dot(oa_ref[...], wb_ref[:A_WIDTH, :], preferred_element_type=F32)
    yb = jnp.dot(ob_ref[...], wb_ref[A_WIDTH:A_WIDTH + B_WIDTH, :], preferred_element_type=F32)
    yc = jnp.dot(oc_ref[...], wb_ref[A_WIDTH + B_WIDTH:, :], preferred_element_type=F32)
    y = g0_ref[...].astype(F32) * ya + g1_ref[...].astype(F32) * yb + g2_ref[...].astype(F32) * yc
    mix = jnp.dot(y.astype(BF16), wo_ref[...], preferred_element_type=F32)
    z = _layer_norm_rows(DEEPNORM_ALPHA * x_ref[...] + mix, lng_ref[...], lnb_ref[...])
    xo_ref[...] = z
    xb_ref[...] = z.astype(BF16)
    _store_row_tiles(xp_ref, _pack_halves(z))


def _merge(oa, ob, oc, proj, x, w_branch, w_out, ln_g, ln_b):
    t = x.shape[0]
    tm = min(MERGE_TM, t)
    gate_base = 8 * PROJ_TN // D_MODEL
    row = lambda i: (i, 0)
    whole = lambda i: (0, 0)
    resident = dict(pipeline_mode=pl.Buffered(1))
    return pl.pallas_call(
        _merge_kernel,
        grid=(t // tm,),
        in_specs=[pl.BlockSpec((tm, A_WIDTH), row),
                  pl.BlockSpec((tm, B_WIDTH), row),
                  pl.BlockSpec((tm, C_WIDTH), row),
                  pl.BlockSpec((tm, D_MODEL), lambda i: (i, gate_base)),
                  pl.BlockSpec((tm, D_MODEL), lambda i: (i, gate_base + 1)),
                  pl.BlockSpec((tm, D_MODEL), lambda i: (i, gate_base + 2)),
                  pl.BlockSpec((tm, D_MODEL), row),
                  pl.BlockSpec((D_MODEL, D_MODEL), whole, **resident),
                  pl.BlockSpec((D_MODEL, D_MODEL), whole, **resident),
                  pl.BlockSpec((1, D_MODEL), whole),
                  pl.BlockSpec((1, D_MODEL), whole)],
        out_specs=[pl.BlockSpec((tm, D_MODEL), row),
                   pl.BlockSpec((tm, D_MODEL), row),
                   pl.BlockSpec((tm * ROW_SUBLANES, LANES), row)],
        out_shape=[jax.ShapeDtypeStruct((t, D_MODEL), F32),
                   jax.ShapeDtypeStruct((t, D_MODEL), BF16),
                   jax.ShapeDtypeStruct((t * ROW_SUBLANES, LANES), U32)],
        compiler_params=_cparams(("parallel",)),
        name="merge_outproj_ln",
    )(oa, ob, oc, proj, proj, proj, x, w_branch, w_out, ln_g.reshape(1, -1), ln_b.reshape(1, -1))


def _split_bf16(a):
    hi = a.astype(BF16)
    lo = (a - hi.astype(F32)).astype(BF16)
    return hi, lo


def _router_kernel(x_ref, wt_ref, bias_ref, upper_ref, lower_ref,
                   e8_ref, r8_ref, w8_ref, cnt_ref, carry_ref):
    i = pl.program_id(0)

    @pl.when(i == 0)
    def _():
        carry_ref[...] = jnp.zeros(carry_ref.shape, F32)

    xh, xl = _split_bf16(x_ref[...])
    wh, wl = _split_bf16(wt_ref[...])
    logits = (lax.dot_general(wh, xh, _NT_DIMS, preferred_element_type=F32)
              + lax.dot_general(wh, xl, _NT_DIMS, preferred_element_type=F32)
              + lax.dot_general(wl, xh, _NT_DIMS, preferred_element_type=F32))
    scores = 1.0 / (1.0 + jnp.exp(-logits))
    choice = scores + bias_ref[:, 0:1]
    tm = choice.shape[1]
    sub = lax.broadcasted_iota(I32, (GROUP_SIZE, tm), 0)

    group_rows = []
    for g in range(N_GROUPS):
        cg = choice[g * GROUP_SIZE:(g + 1) * GROUP_SIZE, :]
        m1 = jnp.max(cg, axis=0, keepdims=True)
        first = jnp.min(jnp.where(cg == m1, sub, GROUP_SIZE), axis=0, keepdims=True)
        m2 = jnp.max(jnp.where(sub == first, -jnp.inf, cg), axis=0, keepdims=True)
        group_rows.append(m1 + m2)
    gs = jnp.concatenate(group_rows, axis=0)
    ahead = jnp.zeros(gs.shape, F32)
    for gp in range(N_GROUPS):
        rowv = gs[gp:gp + 1, :]
        ahead = ahead + jnp.where((rowv > gs) | ((rowv == gs) & (gp < sub)), 1.0, 0.0)
    keep_group = ahead < TOPK_GROUPS
    masked = jnp.concatenate(
        [jnp.where(keep_group[g:g + 1, :], choice[g * GROUP_SIZE:(g + 1) * GROUP_SIZE, :], -jnp.inf)
         for g in range(N_GROUPS)], axis=0)

    eidx = lax.broadcasted_iota(I32, masked.shape, 0)
    ahead = jnp.zeros(masked.shape, F32)
    for ep in range(N_EXPERTS):
        rowv = masked[ep:ep + 1, :]
        ahead = ahead + jnp.where((rowv > masked) | ((rowv == masked) & (ep < eidx)), 1.0, 0.0)
    sel = ahead < TOPK
    picked = jnp.where(sel, scores, 0.0)
    gates = picked / jnp.sum(picked, axis=0, keepdims=True) * ROUTED_SCALE
    self32 = jnp.where(sel, 1.0, 0.0)
    selb = self32.astype(BF16)

    rank = jnp.dot(selb, upper_ref[...], preferred_element_type=F32) + carry_ref[:, 0:1]
    slot = jnp.dot(lower_ref[...], selb, preferred_element_type=F32)
    carry_ref[...] = carry_ref[...] + jnp.sum(self32, axis=1, keepdims=True)
    cnt_ref[...] = carry_ref[...].astype(I32)

    eidf = eidx.astype(F32)
    e_rows, r_rows, w_rows = [], [], []
    for k in range(TOPK):
        hit = sel & (slot == float(k))
        e_rows.append(jnp.sum(jnp.where(hit, eidf, 0.0), axis=0, keepdims=True))
        r_rows.append(jnp.sum(jnp.where(hit, rank, 0.0), axis=0, keepdims=True))
        w_rows.append(jnp.sum(jnp.where(hit, gates, 0.0), axis=0, keepdims=True))
    e8_ref[...] = jnp.concatenate(e_rows, axis=0).astype(I32)
    r8_ref[...] = jnp.concatenate(r_rows, axis=0).astype(I32)
    w8_ref[...] = jnp.concatenate(w_rows, axis=0)


def _router(x, router_w, router_bias):
    t = x.shape[0]
    tm = min(ROUTER_TM, t)
    upper = (jnp.arange(tm)[:, None] < jnp.arange(tm)[None, :]).astype(BF16)
    lower = (jnp.arange(N_EXPERTS)[None, :] < jnp.arange(N_EXPERTS)[:, None]).astype(BF16)
    bias = jnp.broadcast_to(router_bias.astype(F32)[:, None], (N_EXPERTS, LANES))
    whole = lambda i: (0, 0)
    col = lambda i: (0, i)
    return pl.pallas_call(
        _router_kernel,
        grid=(t // tm,),
        in_specs=[pl.BlockSpec((tm, D_MODEL), lambda i: (i, 0)),
                  pl.BlockSpec((N_EXPERTS, D_MODEL), whole),
                  pl.BlockSpec((N_EXPERTS, LANES), whole),
                  pl.BlockSpec((tm, tm), whole),
                  pl.BlockSpec((N_EXPERTS, N_EXPERTS), whole)],
        out_specs=[pl.BlockSpec((TOPK, tm), col),
                   pl.BlockSpec((TOPK, tm), col),
                   pl.BlockSpec((TOPK, tm), col),
                   pl.BlockSpec((N_EXPERTS, LANES), whole)],
        out_shape=[jax.ShapeDtypeStruct((TOPK, t), I32),
                   jax.ShapeDtypeStruct((TOPK, t), I32),
                   jax.ShapeDtypeStruct((TOPK, t), F32),
                   jax.ShapeDtypeStruct((N_EXPERTS, LANES), I32)],
        scratch_shapes=[pltpu.VMEM((N_EXPERTS, LANES), F32)],
        compiler_params=_cparams(("arbitrary",)),
        name="router",
    )(x, router_w.T, bias, upper, lower)


def _row_tile(ref, r):
    return ref.at[pl.ds(pl.multiple_of(r * ROW_SUBLANES, ROW_SUBLANES), ROW_SUBLANES)]


def _dispatch_kernel(pos_ref, x_ref, xs_ref, sem):
    tm = x_ref.shape[0] // ROW_SUBLANES

    def row_copy(t, k):
        return pltpu.make_async_copy(_row_tile(x_ref, t), _row_tile(xs_ref, pos_ref[t * TOPK + k]), sem)

    def start(t, c):
        for k in range(TOPK):
            row_copy(t, k).start(priority=k % 2)
        return c
    lax.fori_loop(0, tm, start, 0)

    def wait(t, c):
        for k in range(TOPK):
            row_copy(t, k).wait()
        return c
    lax.fori_loop(0, tm, wait, 0)


def _dispatch(xp, pos8, n_rows):
    t = xp.shape[0] // ROW_SUBLANES
    tm = min(ROW_TM, t)
    return pl.pallas_call(
        _dispatch_kernel,
        grid=(t // tm,),
        in_specs=[pl.BlockSpec((tm * TOPK,), lambda i: (i,), memory_space=pltpu.SMEM),
                  pl.BlockSpec((tm * ROW_SUBLANES, LANES), lambda i: (i, 0))],
        out_specs=pl.BlockSpec(memory_space=pl.ANY),
        scratch_shapes=[pltpu.SemaphoreType.DMA(())],
        out_shape=jax.ShapeDtypeStruct((n_rows * ROW_SUBLANES, LANES), U32),
        compiler_params=_cparams(("arbitrary",)),
        name="dispatch",
    )(pos8, xp)


def _expert_kernel(te_ref, rows_ref, nv_ref, xs_ref, wg_ref, wu_ref, wd_ref, ys_ref,
                   wgb_ref, wub_ref, wdb_ref):
    i = pl.program_id(0)

    @pl.when(i < nv_ref[0])
    def _():
        @pl.when((i == 0) | (te_ref[i] != te_ref[jnp.maximum(i - 1, 0)]))
        def _():
            wgb_ref[...] = wg_ref[0].astype(BF16)
            wub_ref[...] = wu_ref[0].astype(BF16)
            wdb_ref[...] = wd_ref[0].astype(BF16)

        half = D_MODEL // 2
        tm = EXPERT_TM
        u = jnp.concatenate([_load_row_tile_chunk(xs_ref, a, tm) for a in range(ROW_SUBLANES)], axis=1)
        live = lax.broadcasted_iota(I32, (tm, 1), 0) < rows_ref[i]
        lo, hi = _unpack_halves(jnp.where(live, u, jnp.zeros_like(u)))
        xl = lo.astype(BF16)
        xh = hi.astype(BF16)
        gate = (jnp.dot(xl, wgb_ref[:half, :], preferred_element_type=F32)
                + jnp.dot(xh, wgb_ref[half:, :], preferred_element_type=F32))
        up = (jnp.dot(xl, wub_ref[:half, :], preferred_element_type=F32)
              + jnp.dot(xh, wub_ref[half:, :], preferred_element_type=F32))
        hid = (gate / (1.0 + jnp.exp(-gate))) * up
        y = jnp.dot(hid.astype(BF16), wdb_ref[...], preferred_element_type=F32)
        _store_row_tiles(ys_ref, _pack_halves(y))


def _experts(xs, tile_expert, tile_rows, n_valid, w_gate, w_up, w_down, layer):
    n_rows = xs.shape[0] // ROW_SUBLANES
    n_tiles = n_rows // EXPERT_TM
    row = lambda i, te, tr, nv: (jnp.minimum(i, nv[0] - 1), 0)
    wsel = lambda i, te, tr, nv: (layer * N_EXPERTS + te[i], 0, 0)
    grid_spec = pltpu.PrefetchScalarGridSpec(
        num_scalar_prefetch=3,
        grid=(n_tiles,),
        in_specs=[pl.BlockSpec((EXPERT_TM * ROW_SUBLANES, LANES), row),
                  pl.BlockSpec((1, D_MODEL, EXPERT_FF), wsel),
                  pl.BlockSpec((1, D_MODEL, EXPERT_FF), wsel),
                  pl.BlockSpec((1, EXPERT_FF, D_MODEL), wsel)],
        out_specs=pl.BlockSpec((EXPERT_TM * ROW_SUBLANES, LANES), row),
        scratch_shapes=[pltpu.VMEM((D_MODEL, EXPERT_FF), BF16), pltpu.VMEM((D_MODEL, EXPERT_FF), BF16),
                        pltpu.VMEM((EXPERT_FF, D_MODEL), BF16)],
    )
    return pl.pallas_call(
        _expert_kernel,
        grid_spec=grid_spec,
        out_shape=jax.ShapeDtypeStruct((n_rows * ROW_SUBLANES, LANES), U32),
        compiler_params=_cparams(("arbitrary",)),
        name="experts",
    )(tile_expert, tile_rows, n_valid, xs, w_gate, w_up, w_down)


def _combine_kernel(pos_ref, pos_next_ref, w8_ref, ys_ref, x_ref, xb_ref, wsg_ref, wsu_ref, wsd_ref,
                    lng_ref, lnb_ref, xo_ref, xob_ref, buf_ref, sem):
    i = pl.program_id(0)
    tm = x_ref.shape[0]
    slot = i % 2

    def row_copy(p_ref, s, t, k):
        return pltpu.make_async_copy(_row_tile(ys_ref, p_ref[t * TOPK + k]), _row_tile(buf_ref.at[s, k], t),
                                     sem.at[s])

    def start_all(p_ref, s):
        def start(t, c):
            for k in range(TOPK):
                row_copy(p_ref, s, t, k).start(priority=k % 2)
            return c
        lax.fori_loop(0, tm, start, 0)

    @pl.when(i == 0)
    def _():
        start_all(pos_ref, 0)

    @pl.when(i + 1 < pl.num_programs(0))
    def _():
        start_all(pos_next_ref, 1 - slot)

    xb = xb_ref[...]
    sg = jnp.dot(xb, wsg_ref[...], preferred_element_type=F32)
    su = jnp.dot(xb, wsu_ref[...], preferred_element_type=F32)
    hid = (sg / (1.0 + jnp.exp(-sg))) * su
    shared = jnp.dot(hid.astype(BF16), wsd_ref[...], preferred_element_type=F32)

    def wait(t, c):
        for k in range(TOPK):
            row_copy(pos_ref, slot, t, k).wait()
        return c
    lax.fori_loop(0, tm, wait, 0)

    w8 = w8_ref[...]
    wk = [jnp.broadcast_to(w8[:, k:k + 1], (tm, LANES)) for k in range(TOPK)]
    lo_parts, hi_parts = [], []
    for a in range(ROW_SUBLANES):
        acc_lo = jnp.zeros((tm, LANES), F32)
        acc_hi = jnp.zeros((tm, LANES), F32)
        for k in range(TOPK):
            lo, hi = _unpack_halves(_load_row_tile_chunk(buf_ref.at[slot, k], a, tm))
            acc_lo = acc_lo + wk[k] * lo
            acc_hi = acc_hi + wk[k] * hi
        lo_parts.append(acc_lo)
        hi_parts.append(acc_hi)
    routed = jnp.concatenate(lo_parts + hi_parts, axis=1)
    z = _layer_norm_rows(DEEPNORM_ALPHA * x_ref[...] + (routed + shared), lng_ref[...], lnb_ref[...])
    xo_ref[...] = z
    xob_ref[...] = z.astype(BF16)


def _combine(ys, pos8, w8t, x, xb, ws_gate, ws_up, ws_down, ln_g, ln_b):
    t = x.shape[0]
    tm = min(ROW_TM, t)
    row = lambda i: (i, 0)
    whole = lambda i: (0, 0)
    n_steps = t // tm
    grid_spec = pl.GridSpec(
        grid=(n_steps,),
        in_specs=[pl.BlockSpec((tm * TOPK,), lambda i: (i,), memory_space=pltpu.SMEM),
                  pl.BlockSpec((tm * TOPK,), lambda i: (jnp.minimum(i + 1, n_steps - 1),),
                               memory_space=pltpu.SMEM),
                  pl.BlockSpec((tm, TOPK), row),
                  pl.BlockSpec(memory_space=pl.ANY),
                  pl.BlockSpec((tm, D_MODEL), row),
                  pl.BlockSpec((tm, D_MODEL), row),
                  pl.BlockSpec((D_MODEL, EXPERT_FF), whole),
                  pl.BlockSpec((D_MODEL, EXPERT_FF), whole),
                  pl.BlockSpec((EXPERT_FF, D_MODEL), whole),
                  pl.BlockSpec((1, D_MODEL), whole),
                  pl.BlockSpec((1, D_MODEL), whole)],
        out_specs=[pl.BlockSpec((tm, D_MODEL), row),
                   pl.BlockSpec((tm, D_MODEL), row)],
        scratch_shapes=[pltpu.VMEM((2, TOPK, tm * ROW_SUBLANES, LANES), U32),
                        pltpu.SemaphoreType.DMA((2,))],
    )
    return pl.pallas_call(
        _combine_kernel,
        grid_spec=grid_spec,
        out_shape=[jax.ShapeDtypeStruct((t, D_MODEL), F32),
                   jax.ShapeDtypeStruct((t, D_MODEL), BF16)],
        compiler_params=_cparams(("arbitrary",)),
        name="combine_shared_ln",
    )(pos8, pos8, w8t, ys, x, xb, ws_gate, ws_up, ws_down, ln_g.reshape(1, -1), ln_b.reshape(1, -1))


_CKV_TILE = 4
_TAIL_SHIFT = PROJ_TN - MLA_KV_RANK - MLA_ROPE


def _in_proj_weight_kernel(prev_ref, cur_ref, o_ref):
    j = pl.program_id(1)

    @pl.when(j < _CKV_TILE)
    def _():
        o_ref[...] = cur_ref[0].astype(BF16)

    @pl.when(j == _CKV_TILE)
    def _():
        w = cur_ref[0]
        kpe = w[MLA_KV_RANK:MLA_KV_RANK + MLA_ROPE, :]
        pad = jnp.zeros((_TAIL_SHIFT - MLA_ROPE, w.shape[1]), F32)
        o_ref[...] = jnp.concatenate([w[:MLA_KV_RANK, :], kpe, kpe, pad], axis=0).astype(BF16)

    @pl.when(j > _CKV_TILE)
    def _():
        keep = PROJ_TN - _TAIL_SHIFT
        o_ref[...] = jnp.concatenate([prev_ref[0][keep:, :], cur_ref[0][:keep, :]], axis=0).astype(BF16)


def _in_proj_weight(w_in_t_all, layer):
    d = w_in_t_all.shape[2]
    return pl.pallas_call(
        _in_proj_weight_kernel,
        grid=(1, IN_COLS_PADDED // PROJ_TN),
        in_specs=[pl.BlockSpec((1, PROJ_TN, d), lambda i, j: (layer, jnp.maximum(j - 1, 0), 0)),
                  pl.BlockSpec((1, PROJ_TN, d), lambda i, j: (layer, j, 0))],
        out_specs=pl.BlockSpec((PROJ_TN, d), lambda i, j: (j, 0)),
        out_shape=jax.ShapeDtypeStruct((IN_COLS_PADDED, d), BF16),
        compiler_params=_cparams(("parallel", "parallel")),
        name="in_proj_weight",
    )(w_in_t_all, w_in_t_all)


def _uq_weight(w_uq):
    w = w_uq.reshape(MLA_Q_RANK, MLA_HEADS, MLA_NOPE + MLA_ROPE)
    return jnp.concatenate([w[:, :, :MLA_NOPE].reshape(MLA_Q_RANK, -1),
                            w[:, :, MLA_NOPE:].reshape(MLA_Q_RANK, -1)], axis=1).astype(BF16)


def _ukv_weight(w_ukv):
    w = w_ukv.reshape(MLA_KV_RANK, MLA_HEADS, MLA_NOPE + MLA_V)
    return jnp.concatenate([w[:, :, :MLA_NOPE].reshape(MLA_KV_RANK, -1),
                            w[:, :, MLA_NOPE:].reshape(MLA_KV_RANK, -1)], axis=1).astype(BF16)


def _expert_tiling_kernel(cnt_ref, meta_ref, off_ref):
    ntp = meta_ref.shape[1]
    cnt = cnt_ref[...].astype(F32)
    tiles = jnp.floor((cnt + (EXPERT_TM - 1)) * (1.0 / EXPERT_TM))
    row = lax.broadcasted_iota(I32, (N_EXPERTS, N_EXPERTS), 0)
    col = lax.broadcasted_iota(I32, (N_EXPERTS, N_EXPERTS), 1)
    incl = jnp.where(col <= row, 1.0, 0.0).astype(BF16)
    ends = jnp.dot(incl, tiles.astype(BF16), preferred_element_type=F32)
    first = ends - tiles
    off_ref[...] = (first * EXPERT_TM).astype(I32)

    def widen(a):
        return jnp.concatenate([a] * (ntp // LANES), axis=1)
    tile_id = lax.broadcasted_iota(I32, (N_EXPERTS, ntp), 1).astype(F32)
    ends_w, first_w, cnt_w = widen(ends), widen(first), widen(cnt)
    tile_expert = jnp.minimum(jnp.sum(jnp.where(ends_w <= tile_id, 1.0, 0.0), axis=0, keepdims=True),
                              N_EXPERTS - 1.0)
    mine = (first_w <= tile_id) & (tile_id < ends_w)
    rows = jnp.clip(cnt_w - (tile_id - first_w) * EXPERT_TM, 0.0, float(EXPERT_TM))
    tile_rows = jnp.sum(jnp.where(mine, rows, 0.0), axis=0, keepdims=True)
    n_valid = widen(ends[N_EXPERTS - 1:N_EXPERTS, :])
    meta_ref[...] = jnp.concatenate(
        [tile_expert, tile_rows, n_valid, jnp.zeros((5, ntp), F32)], axis=0).astype(I32)


def _expert_tiling(counts, n_tiles):
    ntp = -(-n_tiles // LANES) * LANES
    meta, off = pl.pallas_call(
        _expert_tiling_kernel,
        out_shape=[jax.ShapeDtypeStruct((8, ntp), I32), jax.ShapeDtypeStruct((N_EXPERTS, LANES), I32)],
        name="expert_tiling",
    )(counts)
    return off[:, 0], meta[0, :n_tiles], meta[1, :n_tiles], meta[2, :1]


def kernel(x, positions, w_in, diff_lambda_q1, diff_lambda_k1, diff_lambda_q2, diff_lambda_k2, diff_subln_g, mla_q_norm_g, mla_w_uq, mla_kv_norm_g, mla_w_ukv, w_branch, w_out, ln_mix_g, ln_mix_b, router_w, router_bias, expert_w_gate, expert_w_up, expert_w_down, shared_w_gate, shared_w_up, shared_w_down, ln_ffn_g, ln_ffn_b):
    b, s, d = x.shape
    t = b * s
    depth = w_in.shape[0]
    cos, sin = _rope_tables(positions)
    xf = x.reshape(t, d)
    xb = xf.astype(BF16)
    w_in_t = jnp.swapaxes(w_in, 1, 2)
    n_tiles = t * TOPK // EXPERT_TM + N_EXPERTS
    proj_tm = min(PROJ_TM, t)

    for l in range(depth):
        gains = jnp.zeros((8, PROJ_TN), F32)
        gains = gains.at[0, :].set(mla_q_norm_g[l]).at[1, :MLA_KV_RANK].set(mla_kv_norm_g[l])
        proj = _proj(xb, 0, d, _in_proj_weight(w_in_t, l), IN_TILE_MODES, IN_TILE_SCALES,
                     cos, sin, gains, proj_tm, w_is_nk=True)
        qbuf = _proj(proj, 3, MLA_Q_RANK, _uq_weight(mla_w_uq[l]),
                     [MODE_PLAIN, MODE_PLAIN, MODE_ROPE_MLA], [(MLA_NOPE + MLA_ROPE) ** -0.5] * 3,
                     cos, sin, gains, proj_tm)
        kvbuf = _proj(proj, 4 * PROJ_TN // MLA_KV_RANK, MLA_KV_RANK, _ukv_weight(mla_w_ukv[l]),
                      [MODE_PLAIN] * 4, [1.0] * 4, cos, sin, gains, proj_tm)

        lambda_init = 0.8 - 0.6 * math.exp(-0.3 * l)
        lam_parts = jnp.stack([diff_lambda_q1[l], diff_lambda_k1[l],
                               diff_lambda_q2[l], diff_lambda_k2[l]]).astype(F32)
        oa = _diff_attention(proj, lam_parts, diff_subln_g[l], b, s, lambda_init)
        ob = _mla_attention(qbuf, kvbuf, proj, b, s)
        oc = _moba_attention(proj, b, s)

        x1, x1b, x1p = _merge(oa, ob, oc, proj, xf, w_branch[l].astype(BF16), w_out[l].astype(BF16),
                              ln_mix_g[l], ln_mix_b[l])

        e8, r8, w8, counts = _router(x1, router_w[l], router_bias[l])
        offsets, tile_expert, tile_rows, n_valid = _expert_tiling(counts, n_tiles)
        expert_ids = jnp.arange(N_EXPERTS, dtype=I32)[:, None, None]
        pos8 = jnp.sum(jnp.where(e8[None] == expert_ids, offsets[:, None, None], 0), axis=0) + r8
        pos8 = pos8.T.reshape(-1)
        xs = _dispatch(x1p, pos8, n_tiles * EXPERT_TM)
        ys = _experts(xs, tile_expert, tile_rows, n_valid,
                      expert_w_gate.reshape(depth * N_EXPERTS, d, EXPERT_FF),
                      expert_w_up.reshape(depth * N_EXPERTS, d, EXPERT_FF),
                      expert_w_down.reshape(depth * N_EXPERTS, EXPERT_FF, d), l)
        xf, xb = _combine(ys, pos8, w8.T, x1, x1b, shared_w_gate[l].astype(BF16),
                          shared_w_up[l].astype(BF16), shared_w_down[l].astype(BF16),
                          ln_ffn_g[l], ln_ffn_b[l])
    return xf.reshape(b, s, d)
```

```python
import functools
import math

import jax
import jax.numpy as jnp
from jax import lax
from jax.experimental import pallas as pl
from jax.experimental.pallas import tpu as pltpu

F32 = jnp.float32
BF16 = jnp.bfloat16
I32 = jnp.int32
U32 = jnp.uint32

D_MODEL = 2048
DEPTH = 2
ROPE_THETA = 500000.0
NORM_EPS = 1e-5

DIFF_HEADS = 4
DIFF_HALF_DIM = 64
DIFF_ROT = 16
MLA_HEADS = 8
MLA_Q_RANK = 512
MLA_KV_RANK = 256
MLA_NOPE = 128
MLA_ROPE = 64
MLA_V = 128
MOBA_HEADS = 4
MOBA_HEAD_DIM = 128
MOBA_BLOCK = 256
MOBA_TOPK = 3
MOBA_ROT = 32
A_WIDTH = 512
B_WIDTH = 1024
C_WIDTH = 512

N_EXPERTS = 64
N_GROUPS = 8
GROUP_SIZE = N_EXPERTS // N_GROUPS
TOPK_GROUPS = 4
TOPK = 8
EXPERT_FF = 512
ROUTED_SCALE = 2.5

DEEPNORM_ALPHA = (2 * DEPTH) ** 0.25

LANES = 128
NEG_BIG = -1e30

PROJ_TN = 512
MODE_PLAIN, MODE_ROPE_DIFF, MODE_ROPE_MLA, MODE_ROPE_MOBA, MODE_RMS, MODE_CKV, MODE_SIGMOID = range(7)
_ROPE_OF_MODE = {MODE_ROPE_DIFF: (0, DIFF_ROT // 2), MODE_ROPE_MLA: (1, MLA_ROPE // 2),
                 MODE_ROPE_MOBA: (2, MOBA_ROT // 2)}
_ROPE_PERIOD = (DIFF_HALF_DIM, MLA_ROPE, MOBA_HEAD_DIM)
_ROPE_ROT = (DIFF_ROT, MLA_ROPE, MOBA_ROT)

IN_TILE_MODES = ([MODE_ROPE_DIFF, MODE_ROPE_DIFF, MODE_PLAIN, MODE_RMS, MODE_CKV,
                  MODE_ROPE_MOBA, MODE_ROPE_MOBA, MODE_PLAIN] + [MODE_SIGMOID] * 12)
IN_TILE_SCALES = [DIFF_HALF_DIM ** -0.5] + [1.0] * 4 + [MOBA_HEAD_DIM ** -0.5] + [1.0] * 14
IN_COLS_PADDED = PROJ_TN * len(IN_TILE_MODES)

VMEM_LIMIT = 56 * 1024 * 1024

ATT_TQ = 256
PROJ_TM = 2048
MERGE_TM = 256
EXPERT_TM = 512
ROW_TM = 256
ROUTER_TM = 1024


def _cparams(sem):
    return pltpu.CompilerParams(dimension_semantics=sem, vmem_limit_bytes=VMEM_LIMIT)


def _rope_table_kernel(pos_ref, c_ref, cos_ref, sin_ref):
    pos = pos_ref[...].astype(F32)
    for p in range(3):
        ang = pos * c_ref[p:p + 1, :]
        cos_ref[p] = jnp.cos(ang)
        sin_ref[p] = jnp.sin(ang) * c_ref[3 + p:4 + p, :]


def _rope_tables(positions):
    t = positions.size
    lane = jnp.arange(LANES)
    rows = []
    signs = []
    for period, rot in zip(_ROPE_PERIOD, _ROPE_ROT):
        half = rot // 2
        inv_freq = ROPE_THETA ** (-jnp.arange(0, rot, 2, dtype=F32) / rot)
        cp = lane % period
        active = cp < rot
        rows.append(jnp.where(active, inv_freq[cp % half], 0.0))
        signs.append(jnp.where(active, jnp.where(cp < half, -1.0, 1.0), 0.0))
    consts = jnp.stack(rows + signs + [jnp.zeros((LANES,), F32)] * 2).astype(F32)
    tm = min(1024, t)
    cos, sin = pl.pallas_call(
        _rope_table_kernel,
        grid=(t // tm,),
        in_specs=[pl.BlockSpec((tm, 1), lambda i: (i, 0)),
                  pl.BlockSpec((8, LANES), lambda i: (0, 0))],
        out_specs=[pl.BlockSpec((3, tm, LANES), lambda i: (0, i, 0)),
                   pl.BlockSpec((3, tm, LANES), lambda i: (0, i, 0))],
        out_shape=[jax.ShapeDtypeStruct((3, t, LANES), F32)] * 2,
        compiler_params=_cparams(("parallel",)),
        name="rope_tables",
    )(positions.reshape(t, 1), consts)
    return cos, sin


def _proj_kernel(mode_ref, scale_ref, x_ref, w_ref, cos_ref, sin_ref, g_ref, o_ref, *, w_is_nk):
    j = pl.program_id(1)
    mode = mode_ref[j]
    if w_is_nk:
        y = lax.dot_general(x_ref[...], w_ref[...], (((1,), (1,)), ((), ())), preferred_element_type=F32)
    else:
        y = jnp.dot(x_ref[...], w_ref[...], preferred_element_type=F32)
    y = y * scale_ref[j]
    tm, tn = y.shape
    lane = lax.broadcasted_iota(I32, (tm, LANES), 1)

    def rope_chunk(yc, table, half):
        first = (lane % _ROPE_PERIOD[table]) < half
        swapped = jnp.where(first, pltpu.roll(yc, LANES - half, 1), pltpu.roll(yc, half, 1))
        return yc * cos_ref[table] + swapped * sin_ref[table]

    @pl.when(mode == MODE_PLAIN)
    def _():
        o_ref[...] = y.astype(o_ref.dtype)

    for rope_mode, (table, half) in _ROPE_OF_MODE.items():
        @pl.when(mode == rope_mode)
        def _(table=table, half=half):
            for c in range(tn // LANES):
                sl = slice(c * LANES, (c + 1) * LANES)
                o_ref[:, sl] = rope_chunk(y[:, sl], table, half).astype(o_ref.dtype)

    @pl.when(mode == MODE_RMS)
    def _():
        r = lax.rsqrt(jnp.mean(y * y, axis=-1, keepdims=True) + NORM_EPS)
        o_ref[...] = (y * r * g_ref[0:1, :]).astype(o_ref.dtype)

    @pl.when(mode == MODE_CKV)
    def _():
        ckv = y[:, :MLA_KV_RANK]
        r = lax.rsqrt(jnp.mean(ckv * ckv, axis=-1, keepdims=True) + NORM_EPS)
        o_ref[:, :MLA_KV_RANK] = (ckv * r * g_ref[1:2, :MLA_KV_RANK]).astype(o_ref.dtype)
        sl = slice(MLA_KV_RANK, MLA_KV_RANK + LANES)
        table, half = _ROPE_OF_MODE[MODE_ROPE_MLA]
        o_ref[:, sl] = rope_chunk(y[:, sl], table, half).astype(o_ref.dtype)
        o_ref[:, MLA_KV_RANK + LANES:] = jnp.zeros((tm, tn - MLA_KV_RANK - LANES), o_ref.dtype)

    @pl.when(mode == MODE_SIGMOID)
    def _():
        o_ref[...] = (0.5 * jnp.tanh(0.5 * y) + 0.5).astype(o_ref.dtype)


def _proj(x, x_col_block, k_dim, w, modes, scales, cos, sin, gains, tm, w_is_nk=False):
    t = x.shape[0]
    n = w.shape[0] if w_is_nk else w.shape[1]
    assert n % PROJ_TN == 0 and len(modes) == len(scales) == n // PROJ_TN
    assert w.shape == ((n, k_dim) if w_is_nk else (k_dim, n))
    w_spec = (pl.BlockSpec((PROJ_TN, k_dim), lambda i, j, m: (j, 0)) if w_is_nk
              else pl.BlockSpec((k_dim, PROJ_TN), lambda i, j, m: (0, j)))
    grid_spec = pltpu.PrefetchScalarGridSpec(
        num_scalar_prefetch=1,
        grid=(t // tm, n // PROJ_TN),
        in_specs=[pl.BlockSpec(memory_space=pltpu.SMEM),
                  pl.BlockSpec((tm, k_dim), lambda i, j, m: (i, x_col_block)),
                  w_spec,
                  pl.BlockSpec((3, tm, LANES), lambda i, j, m: (0, i, 0)),
                  pl.BlockSpec((3, tm, LANES), lambda i, j, m: (0, i, 0)),
                  pl.BlockSpec((8, PROJ_TN), lambda i, j, m: (0, 0))],
        out_specs=pl.BlockSpec((tm, PROJ_TN), lambda i, j, m: (i, j)),
    )
    return pl.pallas_call(
        functools.partial(_proj_kernel, w_is_nk=w_is_nk),
        grid_spec=grid_spec,
        out_shape=jax.ShapeDtypeStruct((t, n), BF16),
        compiler_params=_cparams(("parallel", "arbitrary")),
        name="proj",
    )(jnp.asarray(modes, I32), jnp.asarray(scales, F32), x, w, cos, sin, gains)


_NT_DIMS = (((1,), (1,)), ((), ()))


def _causal_mask(tq):
    row = lax.broadcasted_iota(I32, (tq, tq), 0)
    col = lax.broadcasted_iota(I32, (tq, tq), 1)
    return col <= row


def _score_strip(s_ref, i, tq, block_scores, block_mask):
    causal = _causal_mask(tq)
    for j in range(i + 1):
        sc = block_scores(j)
        if j == i:
            sc = jnp.where(causal, sc, NEG_BIG)
        elif block_mask is not None:
            sc = jnp.where(block_mask(j), sc, NEG_BIG)
        s_ref[:, j * tq:(j + 1) * tq] = sc


def _softmax_times_v(s_ref, p_ref, v_ref, n):
    tq = s_ref.shape[0]
    nch = n // LANES
    mrun = s_ref[:, 0:LANES]
    for c in range(1, nch):
        mrun = jnp.maximum(mrun, s_ref[:, c * LANES:(c + 1) * LANES])
    m = jnp.broadcast_to(jnp.max(mrun, axis=-1, keepdims=True), (tq, LANES))
    lrun = jnp.zeros((tq, LANES), F32)
    for c in range(nch):
        sl = slice(c * LANES, (c + 1) * LANES)
        p = jnp.exp(s_ref[:, sl] - m)
        lrun = lrun + p
        p_ref[:, sl] = p.astype(BF16)
    l = jnp.sum(lrun, axis=-1, keepdims=True)
    o = jnp.dot(p_ref[:, :n], v_ref[0:n, :], preferred_element_type=F32)
    return o / l


def _diff_attn_kernel(lam_ref, g_ref, q_ref, k_ref, v_ref, o_ref, s1_ref, p1_ref, s2_ref, p2_ref,
                      *, tq, lambda_init):
    nq = q_ref.shape[0] // tq
    lp = lam_ref[...]
    lam = (jnp.exp(jnp.sum(lp[0:1] * lp[1:2], axis=-1, keepdims=True))
           - jnp.exp(jnp.sum(lp[2:3] * lp[3:4], axis=-1, keepdims=True)) + lambda_init)
    lane = lax.broadcasted_iota(I32, (tq, LANES), 1)
    for i in range(nq):
        q = q_ref[i * tq:(i + 1) * tq, :]
        q1 = jnp.where(lane < DIFF_HALF_DIM, q, jnp.zeros_like(q))
        q2 = jnp.where(lane >= DIFF_HALF_DIM, q, jnp.zeros_like(q))
        kblk = lambda j: k_ref[j * tq:(j + 1) * tq, :]
        b = i % 2
        _score_strip(s1_ref.at[b], i, tq,
                     lambda j: lax.dot_general(q1, kblk(j), _NT_DIMS, preferred_element_type=F32), None)
        _score_strip(s2_ref.at[b], i, tq,
                     lambda j: lax.dot_general(q2, kblk(j), _NT_DIMS, preferred_element_type=F32), None)
        n = (i + 1) * tq
        o = (_softmax_times_v(s1_ref.at[b], p1_ref.at[b], v_ref, n)
             - lam * _softmax_times_v(s2_ref.at[b], p2_ref.at[b], v_ref, n))
        r = lax.rsqrt(jnp.mean(o * o, axis=-1, keepdims=True) + NORM_EPS)
        o_ref[i * tq:(i + 1) * tq, :] = (o * r * g_ref[...] * (1.0 - lambda_init)).astype(o_ref.dtype)


def _diff_attention(proj, lam_parts, subln_g, b, s, lambda_init):
    t = b * s
    tq = min(ATT_TQ, s)
    kern = functools.partial(_diff_attn_kernel, tq=tq, lambda_init=lambda_init)
    return pl.pallas_call(
        kern,
        grid=(b, DIFF_HEADS),
        in_specs=[pl.BlockSpec((4, DIFF_HALF_DIM), lambda bi, h: (0, 0)),
                  pl.BlockSpec((1, LANES), lambda bi, h: (0, 0)),
                  pl.BlockSpec((s, LANES), lambda bi, h: (bi, h)),
                  pl.BlockSpec((s, LANES), lambda bi, h: (bi, DIFF_HEADS + h)),
                  pl.BlockSpec((s, LANES), lambda bi, h: (bi, 2 * DIFF_HEADS + h))],
        out_specs=pl.BlockSpec((s, LANES), lambda bi, h: (bi, h)),
        out_shape=jax.ShapeDtypeStruct((t, A_WIDTH), BF16),
        scratch_shapes=[pltpu.VMEM((2, tq, s), F32), pltpu.VMEM((2, tq, s), BF16),
                        pltpu.VMEM((2, tq, s), F32), pltpu.VMEM((2, tq, s), BF16)],
        compiler_params=_cparams(("parallel", "parallel")),
        name="diff_attention",
    )(lam_parts, subln_g.reshape(1, LANES), proj, proj, proj)


def _mla_attn_kernel(qn_ref, qr_ref, kn_ref, kpe_ref, v_ref, o_ref, s_ref, p_ref, *, tq):
    h = pl.program_id(1)
    nq = qn_ref.shape[0] // tq
    lane = lax.broadcasted_iota(I32, (tq, LANES), 1)
    mine = (lane >= MLA_ROPE) == (h % 2 == 1)
    for i in range(nq):
        qn = qn_ref[i * tq:(i + 1) * tq, :]
        qr = qr_ref[i * tq:(i + 1) * tq, :]
        qr = jnp.where(mine, qr, jnp.zeros_like(qr))

        qcat = jnp.concatenate([qn, qr], axis=1)

        def block_scores(j, qcat=qcat):
            rows = slice(j * tq, (j + 1) * tq)
            kcat = jnp.concatenate([kn_ref[rows, :], kpe_ref[rows, :]], axis=1)
            return lax.dot_general(qcat, kcat, _NT_DIMS, preferred_element_type=F32)

        _score_strip(s_ref.at[i % 2], i, tq, block_scores, None)
        o = _softmax_times_v(s_ref.at[i % 2], p_ref.at[i % 2], v_ref, (i + 1) * tq)
        o_ref[i * tq:(i + 1) * tq, :] = o.astype(o_ref.dtype)


def _mla_attention(qbuf, kvbuf, proj, b, s):
    t = b * s
    tq = min(ATT_TQ, s)
    kpe_block = (4 * PROJ_TN + MLA_KV_RANK) // LANES
    return pl.pallas_call(
        functools.partial(_mla_attn_kernel, tq=tq),
        grid=(b, MLA_HEADS),
        in_specs=[pl.BlockSpec((s, LANES), lambda bi, h: (bi, h)),
                  pl.BlockSpec((s, LANES), lambda bi, h: (bi, MLA_HEADS + h // 2)),
                  pl.BlockSpec((s, LANES), lambda bi, h: (bi, h)),
                  pl.BlockSpec((s, LANES), lambda bi, h: (bi, kpe_block)),
                  pl.BlockSpec((s, LANES), lambda bi, h: (bi, MLA_HEADS + h))],
        out_specs=pl.BlockSpec((s, LANES), lambda bi, h: (bi, h)),
        out_shape=jax.ShapeDtypeStruct((t, B_WIDTH), BF16),
        scratch_shapes=[pltpu.VMEM((2, tq, s), F32), pltpu.VMEM((2, tq, s), BF16)],
        compiler_params=_cparams(("parallel", "parallel")),
        name="mla_attention",
    )(qbuf, qbuf, kvbuf, proj, kvbuf)


def _moba_attn_kernel(q_ref, k_ref, v_ref, o_ref, s_ref, p_ref, kmean_ref):
    tq = MOBA_BLOCK
    nb = q_ref.shape[0] // tq
    kmean_ref[...] = jnp.zeros(kmean_ref.shape, F32)
    for j in range(nb):
        kb = k_ref[j * tq:(j + 1) * tq, :].astype(F32)
        kmean_ref[j:j + 1, :] = jnp.sum(kb, axis=0, keepdims=True) * (1.0 / tq)
    km = kmean_ref[...]
    km_hi = km.astype(BF16)
    km_lo = (km - km_hi.astype(F32)).astype(BF16)
    nbp = -(-nb // 8) * 8
    blk = lax.broadcasted_iota(I32, (nbp, tq), 0)
    for i in range(nb):
        q = q_ref[i * tq:(i + 1) * tq, :]
        gate = (lax.dot_general(km_hi, q, _NT_DIMS, preferred_element_type=F32)
                + lax.dot_general(km_lo, q, _NT_DIMS, preferred_element_type=F32))[:nbp, :]
        fully_past = blk < i
        g = jnp.where(fully_past, gate, -jnp.inf)
        cnt = jnp.zeros(g.shape, F32)
        for jp in range(i):
            row = g[jp:jp + 1, :]
            ahead = (row > g) | ((row == g) & (jp < blk))
            cnt = cnt + jnp.where(ahead, 1.0, 0.0)
        kept_t = jnp.where((cnt < MOBA_TOPK) & fully_past, 1.0, 0.0)
        kept_t = jnp.concatenate([kept_t, jnp.zeros((LANES - nbp, tq), F32)], axis=0)
        kept = kept_t.T > 0.5
        _score_strip(s_ref.at[i % 2], i, tq,
                     lambda j, q=q: lax.dot_general(q, k_ref[j * tq:(j + 1) * tq, :], _NT_DIMS,
                                                    preferred_element_type=F32),
                     lambda j, kept=kept: kept[:, j:j + 1])
        o = _softmax_times_v(s_ref.at[i % 2], p_ref.at[i % 2], v_ref, (i + 1) * tq)
        o_ref[i * tq:(i + 1) * tq, :] = o.astype(o_ref.dtype)


def _moba_attention(proj, b, s):
    assert s % MOBA_BLOCK == 0 and s // MOBA_BLOCK <= LANES
    t = b * s
    base = 5 * PROJ_TN // LANES
    return pl.pallas_call(
        _moba_attn_kernel,
        grid=(b, MOBA_HEADS),
        in_specs=[pl.BlockSpec((s, LANES), lambda bi, h: (bi, base + h)),
                  pl.BlockSpec((s, LANES), lambda bi, h: (bi, base + MOBA_HEADS + h)),
                  pl.BlockSpec((s, LANES), lambda bi, h: (bi, base + 2 * MOBA_HEADS + h))],
        out_specs=pl.BlockSpec((s, LANES), lambda bi, h: (bi, h)),
        out_shape=jax.ShapeDtypeStruct((t, C_WIDTH), BF16),
        scratch_shapes=[pltpu.VMEM((2, MOBA_BLOCK, s), F32), pltpu.VMEM((2, MOBA_BLOCK, s), BF16),
                        pltpu.VMEM((LANES, LANES), F32)],
        compiler_params=_cparams(("parallel", "parallel")),
        name="moba_attention",
    )(proj, proj, proj)


def _layer_norm_rows(z, g, b):
    mu = jnp.mean(z, axis=-1, keepdims=True)
    zc = z - mu
    var = jnp.mean(zc * zc, axis=-1, keepdims=True)
    return zc * lax.rsqrt(var + NORM_EPS) * g + b


def _pack_halves(z):
    n = z.shape[1] // 2
    lo = lax.bitcast_convert_type(z[:, :n].astype(BF16).astype(F32), U32)
    hi = lax.bitcast_convert_type(z[:, n:].astype(BF16).astype(F32), U32)
    return (lo >> 16) | hi


def _unpack_halves(u):
    lo = lax.bitcast_convert_type(u << 16, F32)
    hi = lax.bitcast_convert_type(u & jnp.uint32(0xFFFF0000), F32)
    return lo, hi


ROW_WORDS = D_MODEL // 2
ROW_SUBLANES = ROW_WORDS // LANES


def _store_row_tiles(ref, packed):
    tm = packed.shape[0]
    for a in range(ROW_SUBLANES):
        ref[pl.ds(a, tm, stride=ROW_SUBLANES), :] = packed[:, a * LANES:(a + 1) * LANES]


def _load_row_tile_chunk(ref, a, tm):
    return ref[pl.ds(a, tm, stride=ROW_SUBLANES), :]


def _merge_kernel(oa_ref, ob_ref, oc_ref, g0_ref, g1_ref, g2_ref, x_ref, wb_ref, wo_ref,
                  lng_ref, lnb_ref, xo_ref, xb_ref, xp_ref):
    ya = jnp.dot(oa_ref[...], wb_ref[:A_WIDTH, :], preferred_element_type=F32)
    yb = jnp.dot(ob_ref[...], wb_ref[A_WIDTH:A_WIDTH + B_WIDTH, :], preferred_element_type=F32)
    yc = jnp.dot(oc_ref[...], wb_ref[A_WIDTH + B_WIDTH:, :], preferred_element_type=F32)
    y = g0_ref[...].astype(F32) * ya + g1_ref[...].astype(F32) * yb + g2_ref[...].astype(F32) * yc
    mix = jnp.dot(y.astype(BF16), wo_ref[...], preferred_element_type=F32)
    z = _layer_norm_rows(DEEPNORM_ALPHA * x_ref[...] + mix, lng_ref[...], lnb_ref[...])
    xo_ref[...] = z
    xb_ref[...] = z.astype(BF16)
    _store_row_tiles(xp_ref, _pack_halves(z))


def _merge(oa, ob, oc, proj, x, w_branch, w_out, ln_g, ln_b):
    t = x.shape[0]
    tm = min(MERGE_TM, t)
    gate_base = 8 * PROJ_TN // D_MODEL
    row = lambda i: (i, 0)
    whole = lambda i: (0, 0)
    resident = dict(pipeline_mode=pl.Buffered(1))
    return pl.pallas_call(
        _merge_kernel,
        grid=(t // tm,),
        in_specs=[pl.BlockSpec((tm, A_WIDTH), row),
                  pl.BlockSpec((tm, B_WIDTH), row),
                  pl.BlockSpec((tm, C_WIDTH), row),
                  pl.BlockSpec((tm, D_MODEL), lambda i: (i, gate_base)),
                  pl.BlockSpec((tm, D_MODEL), lambda i: (i, gate_base + 1)),
                  pl.BlockSpec((tm, D_MODEL), lambda i: (i, gate_base + 2)),
                  pl.BlockSpec((tm, D_MODEL), row),
                  pl.BlockSpec((D_MODEL, D_MODEL), whole, **resident),
                  pl.BlockSpec((D_MODEL, D_MODEL), whole, **resident),
                  pl.BlockSpec((1, D_MODEL), whole),
                  pl.BlockSpec((1, D_MODEL), whole)],
        out_specs=[pl.BlockSpec((tm, D_MODEL), row),
                   pl.BlockSpec((tm, D_MODEL), row),
                   pl.BlockSpec((tm * ROW_SUBLANES, LANES), row)],
        out_shape=[jax.ShapeDtypeStruct((t, D_MODEL), F32),
                   jax.ShapeDtypeStruct((t, D_MODEL), BF16),
                   jax.ShapeDtypeStruct((t * ROW_SUBLANES, LANES), U32)],
        compiler_params=_cparams(("parallel",)),
        name="merge_outproj_ln",
    )(oa, ob, oc, proj, proj, proj, x, w_branch, w_out, ln_g.reshape(1, -1), ln_b.reshape(1, -1))


def _split_bf16(a):
    hi = a.astype(BF16)
    lo = (a - hi.astype(F32)).astype(BF16)
    return hi, lo


def _router_kernel(x_ref, wt_ref, bias_ref, upper_ref, lower_ref,
                   e8_ref, r8_ref, w8_ref, cnt_ref, carry_ref):
    i = pl.program_id(0)

    @pl.when(i == 0)
    def _():
        carry_ref[...] = jnp.zeros(carry_ref.shape, F32)

    xh, xl = _split_bf16(x_ref[...])
    wh, wl = _split_bf16(wt_ref[...])
    logits = (lax.dot_general(wh, xh, _NT_DIMS, preferred_element_type=F32)
              + lax.dot_general(wh, xl, _NT_DIMS, preferred_element_type=F32)
              + lax.dot_general(wl, xh, _NT_DIMS, preferred_element_type=F32))
    scores = 1.0 / (1.0 + jnp.exp(-logits))
    choice = scores + bias_ref[:, 0:1]
    tm = choice.shape[1]
    sub = lax.broadcasted_iota(I32, (GROUP_SIZE, tm), 0)

    group_rows = []
    for g in range(N_GROUPS):
        cg = choice[g * GROUP_SIZE:(g + 1) * GROUP_SIZE, :]
        m1 = jnp.max(cg, axis=0, keepdims=True)
        first = jnp.min(jnp.where(cg == m1, sub, GROUP_SIZE), axis=0, keepdims=True)
        m2 = jnp.max(jnp.where(sub == first, -jnp.inf, cg), axis=0, keepdims=True)
        group_rows.append(m1 + m2)
    gs = jnp.concatenate(group_rows, axis=0)
    ahead = jnp.zeros(gs.shape, F32)
    for gp in range(N_GROUPS):
        rowv = gs[gp:gp + 1, :]
        ahead = ahead + jnp.where((rowv > gs) | ((rowv == gs) & (gp < sub)), 1.0, 0.0)
    keep_group = ahead < TOPK_GROUPS
    masked = jnp.concatenate(
        [jnp.where(keep_group[g:g + 1, :], choice[g * GROUP_SIZE:(g + 1) * GROUP_SIZE, :], -jnp.inf)
         for g in range(N_GROUPS)], axis=0)

    eidx = lax.broadcasted_iota(I32, masked.shape, 0)
    remaining = masked
    self32 = jnp.zeros(masked.shape, F32)
    for _ in range(TOPK):
        top = jnp.max(remaining, axis=0, keepdims=True)
        first = jnp.min(jnp.where(remaining == top, eidx, N_EXPERTS), axis=0, keepdims=True)
        hit = eidx == first
        self32 = jnp.where(hit, 1.0, self32)
        remaining = jnp.where(hit, -jnp.inf, remaining)
    sel = self32 > 0.5
    picked = jnp.where(sel, scores, 0.0)
    gates = picked / jnp.sum(picked, axis=0, keepdims=True) * ROUTED_SCALE
    selb = self32.astype(BF16)

    rank = jnp.dot(selb, upper_ref[...], preferred_element_type=F32) + carry_ref[:, 0:1]
    slot = jnp.dot(lower_ref[...], selb, preferred_element_type=F32)
    carry_ref[...] = carry_ref[...] + jnp.sum(self32, axis=1, keepdims=True)
    cnt_ref[...] = carry_ref[...].astype(I32)

    eidf = eidx.astype(F32)
    e_rows, r_rows, w_rows = [], [], []
    for k in range(TOPK):
        hit = sel & (slot == float(k))
        e_rows.append(jnp.sum(jnp.where(hit, eidf, 0.0), axis=0, keepdims=True))
        r_rows.append(jnp.sum(jnp.where(hit, rank, 0.0), axis=0, keepdims=True))
        w_rows.append(jnp.sum(jnp.where(hit, gates, 0.0), axis=0, keepdims=True))
    e8_ref[...] = jnp.concatenate(e_rows, axis=0).astype(I32)
    r8_ref[...] = jnp.concatenate(r_rows, axis=0).astype(I32)
    w8_ref[...] = jnp.concatenate(w_rows, axis=0)


def _router(x, router_w, router_bias):
    t = x.shape[0]
    tm = min(ROUTER_TM, t)
    upper = (jnp.arange(tm)[:, None] < jnp.arange(tm)[None, :]).astype(BF16)
    lower = (jnp.arange(N_EXPERTS)[None, :] < jnp.arange(N_EXPERTS)[:, None]).astype(BF16)
    bias = jnp.broadcast_to(router_bias.astype(F32)[:, None], (N_EXPERTS, LANES))
    whole = lambda i: (0, 0)
    col = lambda i: (0, i)
    return pl.pallas_call(
        _router_kernel,
        grid=(t // tm,),
        in_specs=[pl.BlockSpec((tm, D_MODEL), lambda i: (i, 0)),
                  pl.BlockSpec((N_EXPERTS, D_MODEL), whole),
                  pl.BlockSpec((N_EXPERTS, LANES), whole),
                  pl.BlockSpec((tm, tm), whole),
                  pl.BlockSpec((N_EXPERTS, N_EXPERTS), whole)],
        out_specs=[pl.BlockSpec((TOPK, tm), col),
                   pl.BlockSpec((TOPK, tm), col),
                   pl.BlockSpec((TOPK, tm), col),
                   pl.BlockSpec((N_EXPERTS, LANES), whole)],
        out_shape=[jax.ShapeDtypeStruct((TOPK, t), I32),
                   jax.ShapeDtypeStruct((TOPK, t), I32),
                   jax.ShapeDtypeStruct((TOPK, t), F32),
                   jax.ShapeDtypeStruct((N_EXPERTS, LANES), I32)],
        scratch_shapes=[pltpu.VMEM((N_EXPERTS, LANES), F32)],
        compiler_params=_cparams(("arbitrary",)),
        name="router",
    )(x, router_w.T, bias, upper, lower)


def _row_tile(ref, r):
    return ref.at[pl.ds(pl.multiple_of(r * ROW_SUBLANES, ROW_SUBLANES), ROW_SUBLANES)]


def _dispatch_kernel(pos_ref, x_ref, xs_ref, sem):
    tm = x_ref.shape[0] // ROW_SUBLANES

    def row_copy(t, k):
        return pltpu.make_async_copy(_row_tile(x_ref, t), _row_tile(xs_ref, pos_ref[t * TOPK + k]), sem)

    def start(t, c):
        for k in range(TOPK):
            row_copy(t, k).start(priority=k % 2)
        return c
    lax.fori_loop(0, tm, start, 0)

    def wait(t, c):
        for k in range(TOPK):
            row_copy(t, k).wait()
        return c
    lax.fori_loop(0, tm, wait, 0)


def _dispatch(xp, pos8, n_rows):
    t = xp.shape[0] // ROW_SUBLANES
    tm = min(ROW_TM, t)
    return pl.pallas_call(
        _dispatch_kernel,
        grid=(t // tm,),
        in_specs=[pl.BlockSpec((tm * TOPK,), lambda i: (i,), memory_space=pltpu.SMEM),
                  pl.BlockSpec((tm * ROW_SUBLANES, LANES), lambda i: (i, 0))],
        out_specs=pl.BlockSpec(memory_space=pl.ANY),
        scratch_shapes=[pltpu.SemaphoreType.DMA(())],
        out_shape=jax.ShapeDtypeStruct((n_rows * ROW_SUBLANES, LANES), U32),
        compiler_params=_cparams(("arbitrary",)),
        name="dispatch",
    )(pos8, xp)


def _expert_kernel(te_ref, rows_ref, nv_ref, xs_ref, wg_ref, wu_ref, wd_ref, ys_ref,
                   wgb_ref, wub_ref, wdb_ref):
    i = pl.program_id(0)

    @pl.when(i < nv_ref[0])
    def _():
        @pl.when((i == 0) | (te_ref[i] != te_ref[jnp.maximum(i - 1, 0)]))
        def _():
            wgb_ref[...] = wg_ref[0].astype(BF16)
            wub_ref[...] = wu_ref[0].astype(BF16)
            wdb_ref[...] = wd_ref[0].astype(BF16)

        half = D_MODEL // 2

        def ffn(tm):
            u = jnp.concatenate([_load_row_tile_chunk(xs_ref, a, tm) for a in range(ROW_SUBLANES)], axis=1)
            live = lax.broadcasted_iota(I32, (tm, 1), 0) < rows_ref[i]
            lo, hi = _unpack_halves(jnp.where(live, u, jnp.zeros_like(u)))
            xl = lo.astype(BF16)
            xh = hi.astype(BF16)
            gate = (jnp.dot(xl, wgb_ref[:half, :], preferred_element_type=F32)
                    + jnp.dot(xh, wgb_ref[half:, :], preferred_element_type=F32))
            up = (jnp.dot(xl, wub_ref[:half, :], preferred_element_type=F32)
                  + jnp.dot(xh, wub_ref[half:, :], preferred_element_type=F32))
            hid = (gate / (1.0 + jnp.exp(-gate))) * up
            y = jnp.dot(hid.astype(BF16), wdb_ref[...], preferred_element_type=F32)
            _store_row_tiles(ys_ref, _pack_halves(y))

        @pl.when(rows_ref[i] > EXPERT_TM // 2)
        def _():
            ffn(EXPERT_TM)

        @pl.when(rows_ref[i] <= EXPERT_TM // 2)
        def _():
            ffn(EXPERT_TM // 2)


def _experts(xs, tile_expert, tile_rows, n_valid, w_gate, w_up, w_down, layer):
    n_rows = xs.shape[0] // ROW_SUBLANES
    n_tiles = n_rows // EXPERT_TM
    row = lambda i, te, tr, nv: (jnp.minimum(i, nv[0] - 1), 0)
    wsel = lambda i, te, tr, nv: (layer * N_EXPERTS + te[i], 0, 0)
    grid_spec = pltpu.PrefetchScalarGridSpec(
        num_scalar_prefetch=3,
        grid=(n_tiles,),
        in_specs=[pl.BlockSpec((EXPERT_TM * ROW_SUBLANES, LANES), row),
                  pl.BlockSpec((1, D_MODEL, EXPERT_FF), wsel),
                  pl.BlockSpec((1, D_MODEL, EXPERT_FF), wsel),
                  pl.BlockSpec((1, EXPERT_FF, D_MODEL), wsel)],
        out_specs=pl.BlockSpec((EXPERT_TM * ROW_SUBLANES, LANES), row),
        scratch_shapes=[pltpu.VMEM((D_MODEL, EXPERT_FF), BF16), pltpu.VMEM((D_MODEL, EXPERT_FF), BF16),
                        pltpu.VMEM((EXPERT_FF, D_MODEL), BF16)],
    )
    return pl.pallas_call(
        _expert_kernel,
        grid_spec=grid_spec,
        out_shape=jax.ShapeDtypeStruct((n_rows * ROW_SUBLANES, LANES), U32),
        compiler_params=_cparams(("arbitrary",)),
        name="experts",
    )(tile_expert, tile_rows, n_valid, xs, w_gate, w_up, w_down)


def _combine_kernel(pos_ref, pos_next_ref, w8_ref, ys_ref, x_ref, xb_ref, wsg_ref, wsu_ref, wsd_ref,
                    lng_ref, lnb_ref, xo_ref, xob_ref, buf_ref, sem):
    i = pl.program_id(0)
    tm = x_ref.shape[0]
    slot = i % 2

    def row_copy(p_ref, s, t, k):
        return pltpu.make_async_copy(_row_tile(ys_ref, p_ref[t * TOPK + k]), _row_tile(buf_ref.at[s, k], t),
                                     sem.at[s])

    def start_all(p_ref, s):
        def start(t, c):
            for k in range(TOPK):
                row_copy(p_ref, s, t, k).start(priority=k % 2)
            return c
        lax.fori_loop(0, tm, start, 0)

    @pl.when(i == 0)
    def _():
        start_all(pos_ref, 0)

    @pl.when(i + 1 < pl.num_programs(0))
    def _():
        start_all(pos_next_ref, 1 - slot)

    xb = xb_ref[...]
    sg = jnp.dot(xb, wsg_ref[...], preferred_element_type=F32)
    su = jnp.dot(xb, wsu_ref[...], preferred_element_type=F32)
    hid = (sg / (1.0 + jnp.exp(-sg))) * su
    shared = jnp.dot(hid.astype(BF16), wsd_ref[...], preferred_element_type=F32)

    def wait(t, c):
        for k in range(TOPK):
            row_copy(pos_ref, slot, t, k).wait()
        return c
    lax.fori_loop(0, tm, wait, 0)

    w8 = w8_ref[...]
    wk = [jnp.broadcast_to(w8[:, k:k + 1], (tm, LANES)) for k in range(TOPK)]
    lo_parts, hi_parts = [], []
    for a in range(ROW_SUBLANES):
        acc_lo = jnp.zeros((tm, LANES), F32)
        acc_hi = jnp.zeros((tm, LANES), F32)
        for k in range(TOPK):
            lo, hi = _unpack_halves(_load_row_tile_chunk(buf_ref.at[slot, k], a, tm))
            acc_lo = acc_lo + wk[k] * lo
            acc_hi = acc_hi + wk[k] * hi
        lo_parts.append(acc_lo)
        hi_parts.append(acc_hi)
    routed = jnp.concatenate(lo_parts + hi_parts, axis=1)
    z = _layer_norm_rows(DEEPNORM_ALPHA * x_ref[...] + (routed + shared), lng_ref[...], lnb_ref[...])
    xo_ref[...] = z
    xob_ref[...] = z.astype(BF16)


def _combine(ys, pos8, w8t, x, xb, ws_gate, ws_up, ws_down, ln_g, ln_b):
    t = x.shape[0]
    tm = min(ROW_TM, t)
    row = lambda i: (i, 0)
    whole = lambda i: (0, 0)
    n_steps = t // tm
    grid_spec = pl.GridSpec(
        grid=(n_steps,),
        in_specs=[pl.BlockSpec((tm * TOPK,), lambda i: (i,), memory_space=pltpu.SMEM),
                  pl.BlockSpec((tm * TOPK,), lambda i: (jnp.minimum(i + 1, n_steps - 1),),
                               memory_space=pltpu.SMEM),
                  pl.BlockSpec((tm, TOPK), row),
                  pl.BlockSpec(memory_space=pl.ANY),
                  pl.BlockSpec((tm, D_MODEL), row),
                  pl.BlockSpec((tm, D_MODEL), row),
                  pl.BlockSpec((D_MODEL, EXPERT_FF), whole),
                  pl.BlockSpec((D_MODEL, EXPERT_FF), whole),
                  pl.BlockSpec((EXPERT_FF, D_MODEL), whole),
                  pl.BlockSpec((1, D_MODEL), whole),
                  pl.BlockSpec((1, D_MODEL), whole)],
        out_specs=[pl.BlockSpec((tm, D_MODEL), row),
                   pl.BlockSpec((tm, D_MODEL), row)],
        scratch_shapes=[pltpu.VMEM((2, TOPK, tm * ROW_SUBLANES, LANES), U32),
                        pltpu.SemaphoreType.DMA((2,))],
    )
    return pl.pallas_call(
        _combine_kernel,
        grid_spec=grid_spec,
        out_shape=[jax.ShapeDtypeStruct((t, D_MODEL), F32),
                   jax.ShapeDtypeStruct((t, D_MODEL), BF16)],
        compiler_params=_cparams(("arbitrary",)),
        name="combine_shared_ln",
    )(pos8, pos8, w8t, ys, x, xb, ws_gate, ws_up, ws_down, ln_g.reshape(1, -1), ln_b.reshape(1, -1))


_CKV_TILE = 4
_TAIL_SHIFT = PROJ_TN - MLA_KV_RANK - MLA_ROPE


def _in_proj_weight_kernel(prev_ref, cur_ref, o_ref):
    j = pl.program_id(1)

    @pl.when(j < _CKV_TILE)
    def _():
        o_ref[...] = cur_ref[0].astype(BF16)

    @pl.when(j == _CKV_TILE)
    def _():
        w = cur_ref[0]
        kpe = w[MLA_KV_RANK:MLA_KV_RANK + MLA_ROPE, :]
        pad = jnp.zeros((_TAIL_SHIFT - MLA_ROPE, w.shape[1]), F32)
        o_ref[...] = jnp.concatenate([w[:MLA_KV_RANK, :], kpe, kpe, pad], axis=0).astype(BF16)

    @pl.when(j > _CKV_TILE)
    def _():
        keep = PROJ_TN - _TAIL_SHIFT
        o_ref[...] = jnp.concatenate([prev_ref[0][keep:, :], cur_ref[0][:keep, :]], axis=0).astype(BF16)


def _in_proj_weight(w_in_t_all, layer):
    d = w_in_t_all.shape[2]
    return pl.pallas_call(
        _in_proj_weight_kernel,
        grid=(1, IN_COLS_PADDED // PROJ_TN),
        in_specs=[pl.BlockSpec((1, PROJ_TN, d), lambda i, j: (layer, jnp.maximum(j - 1, 0), 0)),
                  pl.BlockSpec((1, PROJ_TN, d), lambda i, j: (layer, j, 0))],
        out_specs=pl.BlockSpec((PROJ_TN, d), lambda i, j: (j, 0)),
        out_shape=jax.ShapeDtypeStruct((IN_COLS_PADDED, d), BF16),
        compiler_params=_cparams(("parallel", "parallel")),
        name="in_proj_weight",
    )(w_in_t_all, w_in_t_all)


def _uq_weight(w_uq):
    w = w_uq.reshape(MLA_Q_RANK, MLA_HEADS, MLA_NOPE + MLA_ROPE)
    return jnp.concatenate([w[:, :, :MLA_NOPE].reshape(MLA_Q_RANK, -1),
                            w[:, :, MLA_NOPE:].reshape(MLA_Q_RANK, -1)], axis=1).astype(BF16)


def _ukv_weight(w_ukv):
    w = w_ukv.reshape(MLA_KV_RANK, MLA_HEADS, MLA_NOPE + MLA_V)
    return jnp.concatenate([w[:, :, :MLA_NOPE].reshape(MLA_KV_RANK, -1),
                            w[:, :, MLA_NOPE:].reshape(MLA_KV_RANK, -1)], axis=1).astype(BF16)


def _expert_tiling_kernel(cnt_ref, meta_ref, off_ref):
    ntp = meta_ref.shape[1]
    cnt = cnt_ref[...].astype(F32)
    tiles = jnp.floor((cnt + (EXPERT_TM - 1)) * (1.0 / EXPERT_TM))
    row = lax.broadcasted_iota(I32, (N_EXPERTS, N_EXPERTS), 0)
    col = lax.broadcasted_iota(I32, (N_EXPERTS, N_EXPERTS), 1)
    incl = jnp.where(col <= row, 1.0, 0.0).astype(BF16)
    ends = jnp.dot(incl, tiles.astype(BF16), preferred_element_type=F32)
    first = ends - tiles
    off_ref[...] = (first * EXPERT_TM).astype(I32)

    def widen(a):
        return jnp.concatenate([a] * (ntp // LANES), axis=1)
    tile_id = lax.broadcasted_iota(I32, (N_EXPERTS, ntp), 1).astype(F32)
    ends_w, first_w, cnt_w = widen(ends), widen(first), widen(cnt)
    tile_expert = jnp.minimum(jnp.sum(jnp.where(ends_w <= tile_id, 1.0, 0.0), axis=0, keepdims=True),
                              N_EXPERTS - 1.0)
    mine = (first_w <= tile_id) & (tile_id < ends_w)
    rows = jnp.clip(cnt_w - (tile_id - first_w) * EXPERT_TM, 0.0, float(EXPERT_TM))
    tile_rows = jnp.sum(jnp.where(mine, rows, 0.0), axis=0, keepdims=True)
    n_valid = widen(ends[N_EXPERTS - 1:N_EXPERTS, :])
    meta_ref[...] = jnp.concatenate(
        [tile_expert, tile_rows, n_valid, jnp.zeros((5, ntp), F32)], axis=0).astype(I32)


def _expert_tiling(counts, n_tiles):
    ntp = -(-n_tiles // LANES) * LANES
    meta, off = pl.pallas_call(
        _expert_tiling_kernel,
        out_shape=[jax.ShapeDtypeStruct((8, ntp), I32), jax.ShapeDtypeStruct((N_EXPERTS, LANES), I32)],
        name="expert_tiling",
    )(counts)
    return off[:, 0], meta[0, :n_tiles], meta[1, :n_tiles], meta[2, :1]


def kernel(x, positions, w_in, diff_lambda_q1, diff_lambda_k1, diff_lambda_q2, diff_lambda_k2, diff_subln_g, mla_q_norm_g, mla_w_uq, mla_kv_norm_g, mla_w_ukv, w_branch, w_out, ln_mix_g, ln_mix_b, router_w, router_bias, expert_w_gate, expert_w_up, expert_w_down, shared_w_gate, shared_w_up, shared_w_down, ln_ffn_g, ln_ffn_b):
    b, s, d = x.shape
    t = b * s
    depth = w_in.shape[0]
    cos, sin = _rope_tables(positions)
    xf = x.reshape(t, d)
    xb = xf.astype(BF16)
    w_in_t = jnp.swapaxes(w_in, 1, 2)
    n_tiles = t * TOPK // EXPERT_TM + N_EXPERTS
    proj_tm = min(PROJ_TM, t)

    for l in range(depth):
        gains = jnp.zeros((8, PROJ_TN), F32)
        gains = gains.at[0, :].set(mla_q_norm_g[l]).at[1, :MLA_KV_RANK].set(mla_kv_norm_g[l])
        proj = _proj(xb, 0, d, _in_proj_weight(w_in_t, l), IN_TILE_MODES, IN_TILE_SCALES,
                     cos, sin, gains, proj_tm, w_is_nk=True)
        qbuf = _proj(proj, 3, MLA_Q_RANK, _uq_weight(mla_w_uq[l]),
                     [MODE_PLAIN, MODE_PLAIN, MODE_ROPE_MLA], [(MLA_NOPE + MLA_ROPE) ** -0.5] * 3,
                     cos, sin, gains, proj_tm)
        kvbuf = _proj(proj, 4 * PROJ_TN // MLA_KV_RANK, MLA_KV_RANK, _ukv_weight(mla_w_ukv[l]),
                      [MODE_PLAIN] * 4, [1.0] * 4, cos, sin, gains, proj_tm)

        lambda_init = 0.8 - 0.6 * math.exp(-0.3 * l)
        lam_parts = jnp.stack([diff_lambda_q1[l], diff_lambda_k1[l],
                               diff_lambda_q2[l], diff_lambda_k2[l]]).astype(F32)
        oa = _diff_attention(proj, lam_parts, diff_subln_g[l], b, s, lambda_init)
        ob = _mla_attention(qbuf, kvbuf, proj, b, s)
        oc = _moba_attention(proj, b, s)

        x1, x1b, x1p = _merge(oa, ob, oc, proj, xf, w_branch[l].astype(BF16), w_out[l].astype(BF16),
                              ln_mix_g[l], ln_mix_b[l])

        e8, r8, w8, counts = _router(x1, router_w[l], router_bias[l])
        offsets, tile_expert, tile_rows, n_valid = _expert_tiling(counts, n_tiles)
        expert_ids = jnp.arange(N_EXPERTS, dtype=I32)[:, None, None]
        pos8 = jnp.sum(jnp.where(e8[None] == expert_ids, offsets[:, None, None], 0), axis=0) + r8
        pos8 = pos8.T.reshape(-1)
        xs = _dispatch(x1p, pos8, n_tiles * EXPERT_TM)
        ys = _experts(xs, tile_expert, tile_rows, n_valid,
                      expert_w_gate.reshape(depth * N_EXPERTS, d, EXPERT_FF),
                      expert_w_up.reshape(depth * N_EXPERTS, d, EXPERT_FF),
                      expert_w_down.reshape(depth * N_EXPERTS, EXPERT_FF, d), l)
        xf, xb = _combine(ys, pos8, w8.T, x1, x1b, shared_w_gate[l].astype(BF16),
                          shared_w_up[l].astype(BF16), shared_w_down[l].astype(BF16),
                          ln_ffn_g[l], ln_ffn_b[l])
    return xf.reshape(b, s, d)
```

```python
import functools
import math

import jax
import jax.numpy as jnp
from jax import lax
from jax.experimental import pallas as pl
from jax.experimental.pallas import tpu as pltpu

F32 = jnp.float32
BF16 = jnp.bfloat16
I32 = jnp.int32
U32 = jnp.uint32

D_MODEL = 2048
DEPTH = 2
ROPE_THETA = 500000.0
NORM_EPS = 1e-5

DIFF_HEADS = 4
DIFF_HALF_DIM = 64
DIFF_ROT = 16
MLA_HEADS = 8
MLA_Q_RANK = 512
MLA_KV_RANK = 256
MLA_NOPE = 128
MLA_ROPE = 64
MLA_V = 128
MOBA_HEADS = 4
MOBA_HEAD_DIM = 128
MOBA_BLOCK = 256
MOBA_TOPK = 3
MOBA_ROT = 32
A_WIDTH = 512
B_WIDTH = 1024
C_WIDTH = 512

N_EXPERTS = 64
N_GROUPS = 8
GROUP_SIZE = N_EXPERTS // N_GROUPS
TOPK_GROUPS = 4
TOPK = 8
EXPERT_FF = 512
ROUTED_SCALE = 2.5

DEEPNORM_ALPHA = (2 * DEPTH) ** 0.25

LANES = 128
NEG_BIG = -1e30

PROJ_TN = 512
MODE_PLAIN, MODE_ROPE_DIFF, MODE_ROPE_MLA, MODE_ROPE_MOBA, MODE_RMS, MODE_CKV, MODE_SIGMOID = range(7)
_ROPE_OF_MODE = {MODE_ROPE_DIFF: (0, DIFF_ROT // 2), MODE_ROPE_MLA: (1, MLA_ROPE // 2),
                 MODE_ROPE_MOBA: (2, MOBA_ROT // 2)}
_ROPE_PERIOD = (DIFF_HALF_DIM, MLA_ROPE, MOBA_HEAD_DIM)
_ROPE_ROT = (DIFF_ROT, MLA_ROPE, MOBA_ROT)

IN_TILE_MODES = ([MODE_ROPE_DIFF, MODE_ROPE_DIFF, MODE_PLAIN, MODE_RMS, MODE_CKV,
                  MODE_ROPE_MOBA, MODE_ROPE_MOBA, MODE_PLAIN] + [MODE_SIGMOID] * 12)
IN_TILE_SCALES = [DIFF_HALF_DIM ** -0.5] + [1.0] * 4 + [MOBA_HEAD_DIM ** -0.5] + [1.0] * 14
IN_COLS_PADDED = PROJ_TN * len(IN_TILE_MODES)

VMEM_LIMIT = 56 * 1024 * 1024

ATT_TQ = 256
PROJ_TM = 2048
MERGE_TM = 256
EXPERT_TM = 512
ROW_TM = 256
ROUTER_TM = 1024


def _cparams(sem):
    return pltpu.CompilerParams(dimension_semantics=sem, vmem_limit_bytes=VMEM_LIMIT)


def _rope_table_kernel(pos_ref, c_ref, cos_ref, sin_ref):
    pos = pos_ref[...].astype(F32)
    for p in range(3):
        ang = pos * c_ref[p:p + 1, :]
        cos_ref[p] = jnp.cos(ang)
        sin_ref[p] = jnp.sin(ang) * c_ref[3 + p:4 + p, :]


def _rope_tables(positions):
    t = positions.size
    lane = jnp.arange(LANES)
    rows = []
    signs = []
    for period, rot in zip(_ROPE_PERIOD, _ROPE_ROT):
        half = rot // 2
        inv_freq = ROPE_THETA ** (-jnp.arange(0, rot, 2, dtype=F32) / rot)
        cp = lane % period
        active = cp < rot
        rows.append(jnp.where(active, inv_freq[cp % half], 0.0))
        signs.append(jnp.where(active, jnp.where(cp < half, -1.0, 1.0), 0.0))
    consts = jnp.stack(rows + signs + [jnp.zeros((LANES,), F32)] * 2).astype(F32)
    tm = min(1024, t)
    cos, sin = pl.pallas_call(
        _rope_table_kernel,
        grid=(t // tm,),
        in_specs=[pl.BlockSpec((tm, 1), lambda i: (i, 0)),
                  pl.BlockSpec((8, LANES), lambda i: (0, 0))],
        out_specs=[pl.BlockSpec((3, tm, LANES), lambda i: (0, i, 0)),
                   pl.BlockSpec((3, tm, LANES), lambda i: (0, i, 0))],
        out_shape=[jax.ShapeDtypeStruct((3, t, LANES), F32)] * 2,
        compiler_params=_cparams(("parallel",)),
        name="rope_tables",
    )(positions.reshape(t, 1), consts)
    return cos, sin


def _proj_kernel(mode_ref, scale_ref, x_ref, w_ref, cos_ref, sin_ref, g_ref, o_ref, *, w_is_nk):
    j = pl.program_id(1)
    mode = mode_ref[j]
    if w_is_nk:
        y = lax.dot_general(x_ref[...], w_ref[...], (((1,), (1,)), ((), ())), preferred_element_type=F32)
    else:
        y = jnp.dot(x_ref[...], w_ref[...], preferred_element_type=F32)
    y = y * scale_ref[j]
    tm, tn = y.shape
    lane = lax.broadcasted_iota(I32, (tm, LANES), 1)

    def rope_chunk(yc, table, half):
        first = (lane % _ROPE_PERIOD[table]) < half
        swapped = jnp.where(first, pltpu.roll(yc, LANES - half, 1), pltpu.roll(yc, half, 1))
        return yc * cos_ref[table] + swapped * sin_ref[table]

    @pl.when(mode == MODE_PLAIN)
    def _():
        o_ref[...] = y.astype(o_ref.dtype)

    for rope_mode, (table, half) in _ROPE_OF_MODE.items():
        @pl.when(mode == rope_mode)
        def _(table=table, half=half):
            for c in range(tn // LANES):
                sl = slice(c * LANES, (c + 1) * LANES)
                o_ref[:, sl] = rope_chunk(y[:, sl], table, half).astype(o_ref.dtype)

    @pl.when(mode == MODE_RMS)
    def _():
        r = lax.rsqrt(jnp.mean(y * y, axis=-1, keepdims=True) + NORM_EPS)
        o_ref[...] = (y * r * g_ref[0:1, :]).astype(o_ref.dtype)

    @pl.when(mode == MODE_CKV)
    def _():
        ckv = y[:, :MLA_KV_RANK]
        r = lax.rsqrt(jnp.mean(ckv * ckv, axis=-1, keepdims=True) + NORM_EPS)
        o_ref[:, :MLA_KV_RANK] = (ckv * r * g_ref[1:2, :MLA_KV_RANK]).astype(o_ref.dtype)
        sl = slice(MLA_KV_RANK, MLA_KV_RANK + LANES)
        table, half = _ROPE_OF_MODE[MODE_ROPE_MLA]
        o_ref[:, sl] = rope_chunk(y[:, sl], table, half).astype(o_ref.dtype)
        o_ref[:, MLA_KV_RANK + LANES:] = jnp.zeros((tm, tn - MLA_KV_RANK - LANES), o_ref.dtype)

    @pl.when(mode == MODE_SIGMOID)
    def _():
        o_ref[...] = (0.5 * jnp.tanh(0.5 * y) + 0.5).astype(o_ref.dtype)


def _proj(x, x_col_block, k_dim, w, modes, scales, cos, sin, gains, tm, w_is_nk=False):
    t = x.shape[0]
    n = w.shape[0] if w_is_nk else w.shape[1]
    assert n % PROJ_TN == 0 and len(modes) == len(scales) == n // PROJ_TN
    assert w.shape == ((n, k_dim) if w_is_nk else (k_dim, n))
    w_spec = (pl.BlockSpec((PROJ_TN, k_dim), lambda i, j, m: (j, 0)) if w_is_nk
              else pl.BlockSpec((k_dim, PROJ_TN), lambda i, j, m: (0, j)))
    grid_spec = pltpu.PrefetchScalarGridSpec(
        num_scalar_prefetch=1,
        grid=(t // tm, n // PROJ_TN),
        in_specs=[pl.BlockSpec(memory_space=pltpu.SMEM),
                  pl.BlockSpec((tm, k_dim), lambda i, j, m: (i, x_col_block)),
                  w_spec,
                  pl.BlockSpec((3, tm, LANES), lambda i, j, m: (0, i, 0)),
                  pl.BlockSpec((3, tm, LANES), lambda i, j, m: (0, i, 0)),
                  pl.BlockSpec((8, PROJ_TN), lambda i, j, m: (0, 0))],
        out_specs=pl.BlockSpec((tm, PROJ_TN), lambda i, j, m: (i, j)),
    )
    return pl.pallas_call(
        functools.partial(_proj_kernel, w_is_nk=w_is_nk),
        grid_spec=grid_spec,
        out_shape=jax.ShapeDtypeStruct((t, n), BF16),
        compiler_params=_cparams(("parallel", "arbitrary")),
        name="proj",
    )(jnp.asarray(modes, I32), jnp.asarray(scales, F32), x, w, cos, sin, gains)


_NT_DIMS = (((1,), (1,)), ((), ()))


def _causal_mask(tq):
    row = lax.broadcasted_iota(I32, (tq, tq), 0)
    col = lax.broadcasted_iota(I32, (tq, tq), 1)
    return col <= row


def _score_strip(s_ref, i, tq, block_scores, block_mask):
    causal = _causal_mask(tq)
    for j in range(i + 1):
        sc = block_scores(j)
        if j == i:
            sc = jnp.where(causal, sc, NEG_BIG)
        elif block_mask is not None:
            sc = jnp.where(block_mask(j), sc, NEG_BIG)
        s_ref[:, j * tq:(j + 1) * tq] = sc


def _softmax_times_v(s_ref, p_ref, v_ref, n):
    tq = s_ref.shape[0]
    nch = n // LANES
    mrun = s_ref[:, 0:LANES]
    for c in range(1, nch):
        mrun = jnp.maximum(mrun, s_ref[:, c * LANES:(c + 1) * LANES])
    m = jnp.broadcast_to(jnp.max(mrun, axis=-1, keepdims=True), (tq, LANES))
    lrun = jnp.zeros((tq, LANES), F32)
    for c in range(nch):
        sl = slice(c * LANES, (c + 1) * LANES)
        p = jnp.exp(s_ref[:, sl] - m)
        lrun = lrun + p
        p_ref[:, sl] = p.astype(BF16)
    l = jnp.sum(lrun, axis=-1, keepdims=True)
    o = jnp.dot(p_ref[:, :n], v_ref[0:n, :], preferred_element_type=F32)
    return o / l


def _diff_attn_kernel(lam_ref, g_ref, q_ref, k_ref, v_ref, o_ref, s1_ref, p1_ref, s2_ref, p2_ref,
                      *, tq, lambda_init):
    nq = q_ref.shape[0] // tq
    lp = lam_ref[...]
    lam = (jnp.exp(jnp.sum(lp[0:1] * lp[1:2], axis=-1, keepdims=True))
           - jnp.exp(jnp.sum(lp[2:3] * lp[3:4], axis=-1, keepdims=True)) + lambda_init)
    lane = lax.broadcasted_iota(I32, (tq, LANES), 1)
    for i in range(nq):
        q = q_ref[i * tq:(i + 1) * tq, :]
        q1 = jnp.where(lane < DIFF_HALF_DIM, q, jnp.zeros_like(q))
        q2 = jnp.where(lane >= DIFF_HALF_DIM, q, jnp.zeros_like(q))
        kblk = lambda j: k_ref[j * tq:(j + 1) * tq, :]
        b = i % 2
        _score_strip(s1_ref.at[b], i, tq,
                     lambda j: lax.dot_general(q1, kblk(j), _NT_DIMS, preferred_element_type=F32), None)
        _score_strip(s2_ref.at[b], i, tq,
                     lambda j: lax.dot_general(q2, kblk(j), _NT_DIMS, preferred_element_type=F32), None)
        n = (i + 1) * tq
        o = (_softmax_times_v(s1_ref.at[b], p1_ref.at[b], v_ref, n)
             - lam * _softmax_times_v(s2_ref.at[b], p2_ref.at[b], v_ref, n))
        r = lax.rsqrt(jnp.mean(o * o, axis=-1, keepdims=True) + NORM_EPS)
        o_ref[i * tq:(i + 1) * tq, :] = (o * r * g_ref[...] * (1.0 - lambda_init)).astype(o_ref.dtype)


def _diff_attention(proj, lam_parts, subln_g, b, s, lambda_init):
    t = b * s
    tq = min(ATT_TQ, s)
    kern = functools.partial(_diff_attn_kernel, tq=tq, lambda_init=lambda_init)
    return pl.pallas_call(
        kern,
        grid=(b, DIFF_HEADS),
        in_specs=[pl.BlockSpec((4, DIFF_HALF_DIM), lambda bi, h: (0, 0)),
                  pl.BlockSpec((1, LANES), lambda bi, h: (0, 0)),
                  pl.BlockSpec((s, LANES), lambda bi, h: (bi, h)),
                  pl.BlockSpec((s, LANES), lambda bi, h: (bi, DIFF_HEADS + h)),
                  pl.BlockSpec((s, LANES), lambda bi, h: (bi, 2 * DIFF_HEADS + h))],
        out_specs=pl.BlockSpec((s, LANES), lambda bi, h: (bi, h)),
        out_shape=jax.ShapeDtypeStruct((t, A_WIDTH), BF16),
        scratch_shapes=[pltpu.VMEM((2, tq, s), F32), pltpu.VMEM((2, tq, s), BF16),
                        pltpu.VMEM((2, tq, s), F32), pltpu.VMEM((2, tq, s), BF16)],
        compiler_params=_cparams(("parallel", "parallel")),
        name="diff_attention",
    )(lam_parts, subln_g.reshape(1, LANES), proj, proj, proj)


def _mla_attn_kernel(qn_ref, qr_ref, kn_ref, kpe_ref, v_ref, o_ref, s_ref, p_ref, *, tq):
    h = pl.program_id(1)
    nq = qn_ref.shape[0] // tq
    lane = lax.broadcasted_iota(I32, (tq, LANES), 1)
    mine = (lane >= MLA_ROPE) == (h % 2 == 1)
    for i in range(nq):
        qn = qn_ref[i * tq:(i + 1) * tq, :]
        qr = qr_ref[i * tq:(i + 1) * tq, :]
        qr = jnp.where(mine, qr, jnp.zeros_like(qr))

        qcat = jnp.concatenate([qn, qr], axis=1)

        def block_scores(j, qcat=qcat):
            rows = slice(j * tq, (j + 1) * tq)
            kcat = jnp.concatenate([kn_ref[rows, :], kpe_ref[rows, :]], axis=1)
            return lax.dot_general(qcat, kcat, _NT_DIMS, preferred_element_type=F32)

        _score_strip(s_ref.at[i % 2], i, tq, block_scores, None)
        o = _softmax_times_v(s_ref.at[i % 2], p_ref.at[i % 2], v_ref, (i + 1) * tq)
        o_ref[i * tq:(i + 1) * tq, :] = o.astype(o_ref.dtype)


def _mla_attention(qbuf, kvbuf, proj, b, s):
    t = b * s
    tq = min(ATT_TQ, s)
    kpe_block = (4 * PROJ_TN + MLA_KV_RANK) // LANES
    return pl.pallas_call(
        functools.partial(_mla_attn_kernel, tq=tq),
        grid=(b, MLA_HEADS),
        in_specs=[pl.BlockSpec((s, LANES), lambda bi, h: (bi, h)),
                  pl.BlockSpec((s, LANES), lambda bi, h: (bi, MLA_HEADS + h // 2)),
                  pl.BlockSpec((s, LANES), lambda bi, h: (bi, h)),
                  pl.BlockSpec((s, LANES), lambda bi, h: (bi, kpe_block)),
                  pl.BlockSpec((s, LANES), lambda bi, h: (bi, MLA_HEADS + h))],
        out_specs=pl.BlockSpec((s, LANES), lambda bi, h: (bi, h)),
        out_shape=jax.ShapeDtypeStruct((t, B_WIDTH), BF16),
        scratch_shapes=[pltpu.VMEM((2, tq, s), F32), pltpu.VMEM((2, tq, s), BF16)],
        compiler_params=_cparams(("parallel", "parallel")),
        name="mla_attention",
    )(qbuf, qbuf, kvbuf, proj, kvbuf)


def _moba_attn_kernel(q_ref, k_ref, v_ref, o_ref, s_ref, p_ref, kmean_ref):
    tq = MOBA_BLOCK
    nb = q_ref.shape[0] // tq
    kmean_ref[...] = jnp.zeros(kmean_ref.shape, F32)
    for j in range(nb):
        kb = k_ref[j * tq:(j + 1) * tq, :].astype(F32)
        kmean_ref[j:j + 1, :] = jnp.sum(kb, axis=0, keepdims=True) * (1.0 / tq)
    km = kmean_ref[...]
    km_hi = km.astype(BF16)
    km_lo = (km - km_hi.astype(F32)).astype(BF16)
    nbp = -(-nb // 8) * 8
    blk = lax.broadcasted_iota(I32, (nbp, tq), 0)
    for i in range(nb):
        q = q_ref[i * tq:(i + 1) * tq, :]
        gate = (lax.dot_general(km_hi, q, _NT_DIMS, preferred_element_type=F32)
                + lax.dot_general(km_lo, q, _NT_DIMS, preferred_element_type=F32))[:nbp, :]
        fully_past = blk < i
        g = jnp.where(fully_past, gate, -jnp.inf)
        cnt = jnp.zeros(g.shape, F32)
        for jp in range(i):
            row = g[jp:jp + 1, :]
            ahead = (row > g) | ((row == g) & (jp < blk))
            cnt = cnt + jnp.where(ahead, 1.0, 0.0)
        kept_t = jnp.where((cnt < MOBA_TOPK) & fully_past, 1.0, 0.0)
        kept_t = jnp.concatenate([kept_t, jnp.zeros((LANES - nbp, tq), F32)], axis=0)
        kept = kept_t.T > 0.5
        _score_strip(s_ref.at[i % 2], i, tq,
                     lambda j, q=q: lax.dot_general(q, k_ref[j * tq:(j + 1) * tq, :], _NT_DIMS,
                                                    preferred_element_type=F32),
                     lambda j, kept=kept: kept[:, j:j + 1])
        o = _softmax_times_v(s_ref.at[i % 2], p_ref.at[i % 2], v_ref, (i + 1) * tq)
        o_ref[i * tq:(i + 1) * tq, :] = o.astype(o_ref.dtype)


def _moba_attention(proj, b, s):
    assert s % MOBA_BLOCK == 0 and s // MOBA_BLOCK <= LANES
    t = b * s
    base = 5 * PROJ_TN // LANES
    return pl.pallas_call(
        _moba_attn_kernel,
        grid=(b, MOBA_HEADS),
        in_specs=[pl.BlockSpec((s, LANES), lambda bi, h: (bi, base + h)),
                  pl.BlockSpec((s, LANES), lambda bi, h: (bi, base + MOBA_HEADS + h)),
                  pl.BlockSpec((s, LANES), lambda bi, h: (bi, base + 2 * MOBA_HEADS + h))],
        out_specs=pl.BlockSpec((s, LANES), lambda bi, h: (bi, h)),
        out_shape=jax.ShapeDtypeStruct((t, C_WIDTH), BF16),
        scratch_shapes=[pltpu.VMEM((2, MOBA_BLOCK, s), F32), pltpu.VMEM((2, MOBA_BLOCK, s), BF16),
                        pltpu.VMEM((LANES, LANES), F32)],
        compiler_params=_cparams(("parallel", "parallel")),
        name="moba_attention",
    )(proj, proj, proj)


def _layer_norm_rows(z, g, b):
    mu = jnp.mean(z, axis=-1, keepdims=True)
    zc = z - mu
    var = jnp.mean(zc * zc, axis=-1, keepdims=True)
    return zc * lax.rsqrt(var + NORM_EPS) * g + b


def _pack_halves(z):
    n = z.shape[1] // 2
    lo = lax.bitcast_convert_type(z[:, :n].astype(BF16).astype(F32), U32)
    hi = lax.bitcast_convert_type(z[:, n:].astype(BF16).astype(F32), U32)
    return (lo >> 16) | hi


def _unpack_halves(u):
    lo = lax.bitcast_convert_type(u << 16, F32)
    hi = lax.bitcast_convert_type(u & jnp.uint32(0xFFFF0000), F32)
    return lo, hi


ROW_WORDS = D_MODEL // 2
ROW_SUBLANES = ROW_WORDS // LANES


def _store_row_tiles(ref, packed):
    tm = packed.shape[0]
    for a in range(ROW_SUBLANES):
        ref[pl.ds(a, tm, stride=ROW_SUBLANES), :] = packed[:, a * LANES:(a + 1) * LANES]


def _load_row_tile_chunk(ref, a, tm):
    return ref[pl.ds(a, tm, stride=ROW_SUBLANES), :]


def _merge_kernel(oa_ref, ob_ref, oc_ref, g0_ref, g1_ref, g2_ref, x_ref, wb_ref, wo_ref,
                  lng_ref, lnb_ref, xo_ref, xb_ref, xp_ref):
    ya = jnp.dot(oa_ref[...], wb_ref[:A_WIDTH, :], preferred_element_type=F32)
    yb = jnp.dot(ob_ref[...], wb_ref[A_WIDTH:A_WIDTH + B_WIDTH, :], preferred_element_type=F32)
    yc = jnp.dot(oc_ref[...], wb_ref[A_WIDTH + B_WIDTH:, :], preferred_element_type=F32)
    y = g0_ref[...].astype(F32) * ya + g1_ref[...].astype(F32) * yb + g2_ref[...].astype(F32) * yc
    mix = jnp.dot(y.astype(BF16), wo_ref[...], preferred_element_type=F32)
    z = _layer_norm_rows(DEEPNORM_ALPHA * x_ref[...] + mix, lng_ref[...], lnb_ref[...])
    xo_ref[...] = z
    xb_ref[...] = z.astype(BF16)
    _store_row_tiles(xp_ref, _pack_halves(z))


def _merge(oa, ob, oc, proj, x, w_branch, w_out, ln_g, ln_b):
    t = x.shape[0]
    tm = min(MERGE_TM, t)
    gate_base = 8 * PROJ_TN // D_MODEL
    row = lambda i: (i, 0)
    whole = lambda i: (0, 0)
    resident = dict(pipeline_mode=pl.Buffered(1))
    return pl.pallas_call(
        _merge_kernel,
        grid=(t // tm,),
        in_specs=[pl.BlockSpec((tm, A_WIDTH), row),
                  pl.BlockSpec((tm, B_WIDTH), row),
                  pl.BlockSpec((tm, C_WIDTH), row),
                  pl.BlockSpec((tm, D_MODEL), lambda i: (i, gate_base)),
                  pl.BlockSpec((tm, D_MODEL), lambda i: (i, gate_base + 1)),
                  pl.BlockSpec((tm, D_MODEL), lambda i: (i, gate_base + 2)),
                  pl.BlockSpec((tm, D_MODEL), row),
                  pl.BlockSpec((D_MODEL, D_MODEL), whole, **resident),
                  pl.BlockSpec((D_MODEL, D_MODEL), whole, **resident),
                  pl.BlockSpec((1, D_MODEL), whole),
                  pl.BlockSpec((1, D_MODEL), whole)],
        out_specs=[pl.BlockSpec((tm, D_MODEL), row),
                   pl.BlockSpec((tm, D_MODEL), row),
                   pl.BlockSpec((tm * ROW_SUBLANES, LANES), row)],
        out_shape=[jax.ShapeDtypeStruct((t, D_MODEL), F32),
                   jax.ShapeDtypeStruct((t, D_MODEL), BF16),
                   jax.ShapeDtypeStruct((t * ROW_SUBLANES, LANES), U32)],
        compiler_params=_cparams(("parallel",)),
        name="merge_outproj_ln",
    )(oa, ob, oc, proj, proj, proj, x, w_branch, w_out, ln_g.reshape(1, -1), ln_b.reshape(1, -1))


def _split_bf16(a):
    hi = a.astype(BF16)
    lo = (a - hi.astype(F32)).astype(BF16)
    return hi, lo


def _router_kernel(x_ref, wt_ref, bias_ref, upper_ref, lower_ref,
                   e8_ref, r8_ref, w8_ref, cnt_ref, carry_ref):
    i = pl.program_id(0)

    @pl.when(i == 0)
    def _():
        carry_ref[...] = jnp.zeros(carry_ref.shape, F32)

    xh, xl = _split_bf16(x_ref[...])
    wh, wl = _split_bf16(wt_ref[...])
    logits = (lax.dot_general(wh, xh, _NT_DIMS, preferred_element_type=F32)
              + lax.dot_general(wh, xl, _NT_DIMS, preferred_element_type=F32)
              + lax.dot_general(wl, xh, _NT_DIMS, preferred_element_type=F32))
    scores = 1.0 / (1.0 + jnp.exp(-logits))
    choice = scores + bias_ref[:, 0:1]
    tm = choice.shape[1]
    sub = lax.broadcasted_iota(I32, (GROUP_SIZE, tm), 0)

    group_rows = []
    for g in range(N_GROUPS):
        cg = choice[g * GROUP_SIZE:(g + 1) * GROUP_SIZE, :]
        m1 = jnp.max(cg, axis=0, keepdims=True)
        first = jnp.min(jnp.where(cg == m1, sub, GROUP_SIZE), axis=0, keepdims=True)
        m2 = jnp.max(jnp.where(sub == first, -jnp.inf, cg), axis=0, keepdims=True)
        group_rows.append(m1 + m2)
    gs = jnp.concatenate(group_rows, axis=0)
    ahead = jnp.zeros(gs.shape, F32)
    for gp in range(N_GROUPS):
        rowv = gs[gp:gp + 1, :]
        ahead = ahead + jnp.where((rowv > gs) | ((rowv == gs) & (gp < sub)), 1.0, 0.0)
    keep_group = ahead < TOPK_GROUPS
    masked = jnp.concatenate(
        [jnp.where(keep_group[g:g + 1, :], choice[g * GROUP_SIZE:(g + 1) * GROUP_SIZE, :], -jnp.inf)
         for g in range(N_GROUPS)], axis=0)

    eidx = lax.broadcasted_iota(I32, masked.shape, 0)
    remaining = masked
    self32 = jnp.zeros(masked.shape, F32)
    for _ in range(TOPK):
        top = jnp.max(remaining, axis=0, keepdims=True)
        first = jnp.min(jnp.where(remaining == top, eidx, N_EXPERTS), axis=0, keepdims=True)
        hit = eidx == first
        self32 = jnp.where(hit, 1.0, self32)
        remaining = jnp.where(hit, -jnp.inf, remaining)
    sel = self32 > 0.5
    picked = jnp.where(sel, scores, 0.0)
    gates = picked / jnp.sum(picked, axis=0, keepdims=True) * ROUTED_SCALE
    selb = self32.astype(BF16)

    rank = jnp.dot(selb, upper_ref[...], preferred_element_type=F32) + carry_ref[:, 0:1]
    slot = jnp.dot(lower_ref[...], selb, preferred_element_type=F32)
    carry_ref[...] = carry_ref[...] + jnp.sum(self32, axis=1, keepdims=True)
    cnt_ref[...] = carry_ref[...].astype(I32)

    eidf = eidx.astype(F32)
    e_rows, r_rows, w_rows = [], [], []
    for k in range(TOPK):
        hit = sel & (slot == float(k))
        e_rows.append(jnp.sum(jnp.where(hit, eidf, 0.0), axis=0, keepdims=True))
        r_rows.append(jnp.sum(jnp.where(hit, rank, 0.0), axis=0, keepdims=True))
        w_rows.append(jnp.sum(jnp.where(hit, gates, 0.0), axis=0, keepdims=True))
    e8_ref[...] = jnp.concatenate(e_rows, axis=0).astype(I32)
    r8_ref[...] = jnp.concatenate(r_rows, axis=0).astype(I32)
    w8_ref[...] = jnp.concatenate(w_rows, axis=0)


def _router(x, router_w, router_bias):
    t = x.shape[0]
    tm = min(ROUTER_TM, t)
    upper = (jnp.arange(tm)[:, None] < jnp.arange(tm)[None, :]).astype(BF16)
    lower = (jnp.arange(N_EXPERTS)[None, :] < jnp.arange(N_EXPERTS)[:, None]).astype(BF16)
    bias = jnp.broadcast_to(router_bias.astype(F32)[:, None], (N_EXPERTS, LANES))
    whole = lambda i: (0, 0)
    col = lambda i: (0, i)
    return pl.pallas_call(
        _router_kernel,
        grid=(t // tm,),
        in_specs=[pl.BlockSpec((tm, D_MODEL), lambda i: (i, 0)),
                  pl.BlockSpec((N_EXPERTS, D_MODEL), whole),
                  pl.BlockSpec((N_EXPERTS, LANES), whole),
                  pl.BlockSpec((tm, tm), whole),
                  pl.BlockSpec((N_EXPERTS, N_EXPERTS), whole)],
        out_specs=[pl.BlockSpec((TOPK, tm), col),
                   pl.BlockSpec((TOPK, tm), col),
                   pl.BlockSpec((TOPK, tm), col),
                   pl.BlockSpec((N_EXPERTS, LANES), whole)],
        out_shape=[jax.ShapeDtypeStruct((TOPK, t), I32),
                   jax.ShapeDtypeStruct((TOPK, t), I32),
                   jax.ShapeDtypeStruct((TOPK, t), F32),
                   jax.ShapeDtypeStruct((N_EXPERTS, LANES), I32)],
        scratch_shapes=[pltpu.VMEM((N_EXPERTS, LANES), F32)],
        compiler_params=_cparams(("arbitrary",)),
        name="router",
    )(x, router_w.T, bias, upper, lower)


def _row_tile(ref, r):
    return ref.at[pl.ds(pl.multiple_of(r * ROW_SUBLANES, ROW_SUBLANES), ROW_SUBLANES)]


def _dispatch_kernel(pos_ref, x_ref, xs_ref, sem):
    tm = x_ref.shape[0] // ROW_SUBLANES

    def row_copy(t, k):
        return pltpu.make_async_copy(_row_tile(x_ref, t), _row_tile(xs_ref, pos_ref[t * TOPK + k]), sem)

    def start(t, c):
        for k in range(TOPK):
            row_copy(t, k).start(priority=k % 2)
        return c
    lax.fori_loop(0, tm, start, 0)

    def wait(t, c):
        for k in range(TOPK):
            row_copy(t, k).wait()
        return c
    lax.fori_loop(0, tm, wait, 0)


def _dispatch(xp, pos8, n_rows):
    t = xp.shape[0] // ROW_SUBLANES
    tm = min(ROW_TM, t)
    return pl.pallas_call(
        _dispatch_kernel,
        grid=(t // tm,),
        in_specs=[pl.BlockSpec((tm * TOPK,), lambda i: (i,), memory_space=pltpu.SMEM),
                  pl.BlockSpec((tm * ROW_SUBLANES, LANES), lambda i: (i, 0))],
        out_specs=pl.BlockSpec(memory_space=pl.ANY),
        scratch_shapes=[pltpu.SemaphoreType.DMA(())],
        out_shape=jax.ShapeDtypeStruct((n_rows * ROW_SUBLANES, LANES), U32),
        compiler_params=_cparams(("arbitrary",)),
        name="dispatch",
    )(pos8, xp)


def _expert_kernel(te_ref, rows_ref, fresh_ref, par_ref, nxt_ref, nv_ref, xs_ref, wg_hbm, wu_hbm, wd_hbm,
                   ys_ref, wgf_ref, wuf_ref, wdf_ref, wgb_ref, wub_ref, wdb_ref, sem, *, first_expert):
    i = pl.program_id(0)

    def weight_copies(e, s):
        return [pltpu.make_async_copy(hbm.at[first_expert + e], buf.at[s], sem.at[s])
                for hbm, buf in ((wg_hbm, wgf_ref), (wu_hbm, wuf_ref), (wd_hbm, wdf_ref))]

    @pl.when(i < nv_ref[0])
    def _():
        @pl.when(fresh_ref[i] == 1)
        def _():
            s = par_ref[i]

            @pl.when(i == 0)
            def _():
                for c in weight_copies(te_ref[0], 0):
                    c.start()

            for c in weight_copies(te_ref[i], s):
                c.wait()

            @pl.when(nxt_ref[i] >= 0)
            def _():
                for c in weight_copies(nxt_ref[i], 1 - s):
                    c.start()

            wgb_ref[...] = wgf_ref[s].astype(BF16)
            wub_ref[...] = wuf_ref[s].astype(BF16)
            wdb_ref[...] = wdf_ref[s].astype(BF16)

        half = D_MODEL // 2

        def ffn(tm):
            u = jnp.concatenate([_load_row_tile_chunk(xs_ref, a, tm) for a in range(ROW_SUBLANES)], axis=1)
            live = lax.broadcasted_iota(I32, (tm, 1), 0) < rows_ref[i]
            lo, hi = _unpack_halves(jnp.where(live, u, jnp.zeros_like(u)))
            xl = lo.astype(BF16)
            xh = hi.astype(BF16)
            gate = (jnp.dot(xl, wgb_ref[:half, :], preferred_element_type=F32)
                    + jnp.dot(xh, wgb_ref[half:, :], preferred_element_type=F32))
            up = (jnp.dot(xl, wub_ref[:half, :], preferred_element_type=F32)
                  + jnp.dot(xh, wub_ref[half:, :], preferred_element_type=F32))
            hid = (gate / (1.0 + jnp.exp(-gate))) * up
            y = jnp.dot(hid.astype(BF16), wdb_ref[...], preferred_element_type=F32)
            _store_row_tiles(ys_ref, _pack_halves(y))

        @pl.when(rows_ref[i] > EXPERT_TM // 2)
        def _():
            ffn(EXPERT_TM)

        @pl.when(rows_ref[i] <= EXPERT_TM // 2)
        def _():
            ffn(EXPERT_TM // 2)


def _experts(xs, schedule, n_valid, w_gate, w_up, w_down, layer):
    n_rows = xs.shape[0] // ROW_SUBLANES
    n_tiles = n_rows // EXPERT_TM
    row = lambda i, te, tr, fr, pa, nx, nv: (jnp.minimum(i, nv[0] - 1), 0)
    grid_spec = pltpu.PrefetchScalarGridSpec(
        num_scalar_prefetch=6,
        grid=(n_tiles,),
        in_specs=[pl.BlockSpec((EXPERT_TM * ROW_SUBLANES, LANES), row),
                  pl.BlockSpec(memory_space=pl.ANY),
                  pl.BlockSpec(memory_space=pl.ANY),
                  pl.BlockSpec(memory_space=pl.ANY)],
        out_specs=pl.BlockSpec((EXPERT_TM * ROW_SUBLANES, LANES), row),
        scratch_shapes=[pltpu.VMEM((2, D_MODEL, EXPERT_FF), F32), pltpu.VMEM((2, D_MODEL, EXPERT_FF), F32),
                        pltpu.VMEM((2, EXPERT_FF, D_MODEL), F32),
                        pltpu.VMEM((D_MODEL, EXPERT_FF), BF16), pltpu.VMEM((D_MODEL, EXPERT_FF), BF16),
                        pltpu.VMEM((EXPERT_FF, D_MODEL), BF16),
                        pltpu.SemaphoreType.DMA((2,))],
    )
    return pl.pallas_call(
        functools.partial(_expert_kernel, first_expert=layer * N_EXPERTS),
        grid_spec=grid_spec,
        out_shape=jax.ShapeDtypeStruct((n_rows * ROW_SUBLANES, LANES), U32),
        compiler_params=_cparams(("arbitrary",)),
        name="experts",
    )(*schedule, n_valid, xs, w_gate, w_up, w_down)


def _combine_kernel(pos_ref, pos_next_ref, w8_ref, ys_ref, x_ref, xb_ref, wsg_ref, wsu_ref, wsd_ref,
                    lng_ref, lnb_ref, xo_ref, xob_ref, buf_ref, sem):
    i = pl.program_id(0)
    tm = x_ref.shape[0]
    slot = i % 2

    def row_copy(p_ref, s, t, k):
        return pltpu.make_async_copy(_row_tile(ys_ref, p_ref[t * TOPK + k]), _row_tile(buf_ref.at[s, k], t),
                                     sem.at[s])

    def start_all(p_ref, s):
        def start(t, c):
            for k in range(TOPK):
                row_copy(p_ref, s, t, k).start(priority=k % 2)
            return c
        lax.fori_loop(0, tm, start, 0)

    @pl.when(i == 0)
    def _():
        start_all(pos_ref, 0)

    @pl.when(i + 1 < pl.num_programs(0))
    def _():
        start_all(pos_next_ref, 1 - slot)

    xb = xb_ref[...]
    sg = jnp.dot(xb, wsg_ref[...], preferred_element_type=F32)
    su = jnp.dot(xb, wsu_ref[...], preferred_element_type=F32)
    hid = (sg / (1.0 + jnp.exp(-sg))) * su
    shared = jnp.dot(hid.astype(BF16), wsd_ref[...], preferred_element_type=F32)

    def wait(t, c):
        for k in range(TOPK):
            row_copy(pos_ref, slot, t, k).wait()
        return c
    lax.fori_loop(0, tm, wait, 0)

    w8 = w8_ref[...]
    wk = [jnp.broadcast_to(w8[:, k:k + 1], (tm, LANES)) for k in range(TOPK)]
    lo_parts, hi_parts = [], []
    for a in range(ROW_SUBLANES):
        acc_lo = jnp.zeros((tm, LANES), F32)
        acc_hi = jnp.zeros((tm, LANES), F32)
        for k in range(TOPK):
            lo, hi = _unpack_halves(_load_row_tile_chunk(buf_ref.at[slot, k], a, tm))
            acc_lo = acc_lo + wk[k] * lo
            acc_hi = acc_hi + wk[k] * hi
        lo_parts.append(acc_lo)
        hi_parts.append(acc_hi)
    routed = jnp.concatenate(lo_parts + hi_parts, axis=1)
    z = _layer_norm_rows(DEEPNORM_ALPHA * x_ref[...] + (routed + shared), lng_ref[...], lnb_ref[...])
    xo_ref[...] = z
    xob_ref[...] = z.astype(BF16)


def _combine(ys, pos8, w8t, x, xb, ws_gate, ws_up, ws_down, ln_g, ln_b):
    t = x.shape[0]
    tm = min(ROW_TM, t)
    row = lambda i: (i, 0)
    whole = lambda i: (0, 0)
    n_steps = t // tm
    grid_spec = pl.GridSpec(
        grid=(n_steps,),
        in_specs=[pl.BlockSpec((tm * TOPK,), lambda i: (i,), memory_space=pltpu.SMEM),
                  pl.BlockSpec((tm * TOPK,), lambda i: (jnp.minimum(i + 1, n_steps - 1),),
                               memory_space=pltpu.SMEM),
                  pl.BlockSpec((tm, TOPK), row),
                  pl.BlockSpec(memory_space=pl.ANY),
                  pl.BlockSpec((tm, D_MODEL), row),
                  pl.BlockSpec((tm, D_MODEL), row),
                  pl.BlockSpec((D_MODEL, EXPERT_FF), whole),
                  pl.BlockSpec((D_MODEL, EXPERT_FF), whole),
                  pl.BlockSpec((EXPERT_FF, D_MODEL), whole),
                  pl.BlockSpec((1, D_MODEL), whole),
                  pl.BlockSpec((1, D_MODEL), whole)],
        out_specs=[pl.BlockSpec((tm, D_MODEL), row),
                   pl.BlockSpec((tm, D_MODEL), row)],
        scratch_shapes=[pltpu.VMEM((2, TOPK, tm * ROW_SUBLANES, LANES), U32),
                        pltpu.SemaphoreType.DMA((2,))],
    )
    return pl.pallas_call(
        _combine_kernel,
        grid_spec=grid_spec,
        out_shape=[jax.ShapeDtypeStruct((t, D_MODEL), F32),
                   jax.ShapeDtypeStruct((t, D_MODEL), BF16)],
        compiler_params=_cparams(("arbitrary",)),
        name="combine_shared_ln",
    )(pos8, pos8, w8t, ys, x, xb, ws_gate, ws_up, ws_down, ln_g.reshape(1, -1), ln_b.reshape(1, -1))


_CKV_TILE = 4
_TAIL_SHIFT = PROJ_TN - MLA_KV_RANK - MLA_ROPE


def _in_proj_weight_kernel(prev_ref, cur_ref, o_ref):
    j = pl.program_id(1)

    @pl.when(j < _CKV_TILE)
    def _():
        o_ref[...] = cur_ref[0].astype(BF16)

    @pl.when(j == _CKV_TILE)
    def _():
        w = cur_ref[0]
        kpe = w[MLA_KV_RANK:MLA_KV_RANK + MLA_ROPE, :]
        pad = jnp.zeros((_TAIL_SHIFT - MLA_ROPE, w.shape[1]), F32)
        o_ref[...] = jnp.concatenate([w[:MLA_KV_RANK, :], kpe, kpe, pad], axis=0).astype(BF16)

    @pl.when(j > _CKV_TILE)
    def _():
        keep = PROJ_TN - _TAIL_SHIFT
        o_ref[...] = jnp.concatenate([prev_ref[0][keep:, :], cur_ref[0][:keep, :]], axis=0).astype(BF16)


def _in_proj_weight(w_in_t_all, layer):
    d = w_in_t_all.shape[2]
    return pl.pallas_call(
        _in_proj_weight_kernel,
        grid=(1, IN_COLS_PADDED // PROJ_TN),
        in_specs=[pl.BlockSpec((1, PROJ_TN, d), lambda i, j: (layer, jnp.maximum(j - 1, 0), 0)),
                  pl.BlockSpec((1, PROJ_TN, d), lambda i, j: (layer, j, 0))],
        out_specs=pl.BlockSpec((PROJ_TN, d), lambda i, j: (j, 0)),
        out_shape=jax.ShapeDtypeStruct((IN_COLS_PADDED, d), BF16),
        compiler_params=_cparams(("parallel", "parallel")),
        name="in_proj_weight",
    )(w_in_t_all, w_in_t_all)


def _uq_weight(w_uq):
    w = w_uq.reshape(MLA_Q_RANK, MLA_HEADS, MLA_NOPE + MLA_ROPE)
    return jnp.concatenate([w[:, :, :MLA_NOPE].reshape(MLA_Q_RANK, -1),
                            w[:, :, MLA_NOPE:].reshape(MLA_Q_RANK, -1)], axis=1).astype(BF16)


def _ukv_weight(w_ukv):
    w = w_ukv.reshape(MLA_KV_RANK, MLA_HEADS, MLA_NOPE + MLA_V)
    return jnp.concatenate([w[:, :, :MLA_NOPE].reshape(MLA_KV_RANK, -1),
                            w[:, :, MLA_NOPE:].reshape(MLA_KV_RANK, -1)], axis=1).astype(BF16)


def _expert_tiling_kernel(cnt_ref, meta_ref, off_ref):
    ntp = meta_ref.shape[1]
    cnt = cnt_ref[...].astype(F32)
    tiles = jnp.floor((cnt + (EXPERT_TM - 1)) * (1.0 / EXPERT_TM))
    row = lax.broadcasted_iota(I32, (N_EXPERTS, N_EXPERTS), 0)
    col = lax.broadcasted_iota(I32, (N_EXPERTS, N_EXPERTS), 1)
    incl = jnp.where(col <= row, 1.0, 0.0).astype(BF16)
    ends = jnp.dot(incl, tiles.astype(BF16), preferred_element_type=F32)
    first = ends - tiles
    off_ref[...] = (first * EXPERT_TM).astype(I32)

    def widen(a):
        return jnp.concatenate([a] * (ntp // LANES), axis=1)
    tile_id = lax.broadcasted_iota(I32, (N_EXPERTS, ntp), 1).astype(F32)
    ends_w, first_w, cnt_w = widen(ends), widen(first), widen(cnt)
    tile_expert = jnp.minimum(jnp.sum(jnp.where(ends_w <= tile_id, 1.0, 0.0), axis=0, keepdims=True),
                              N_EXPERTS - 1.0)
    mine = (first_w <= tile_id) & (tile_id < ends_w)
    rows = jnp.clip(cnt_w - (tile_id - first_w) * EXPERT_TM, 0.0, float(EXPERT_TM))
    tile_rows = jnp.sum(jnp.where(mine, rows, 0.0), axis=0, keepdims=True)
    n_valid = widen(ends[N_EXPERTS - 1:N_EXPERTS, :])
    tiles_w = widen(tiles)
    expert_id = lax.broadcasted_iota(I32, (N_EXPERTS, ntp), 0).astype(F32)
    fresh = jnp.sum(jnp.where(mine & (tile_id == first_w), 1.0, 0.0), axis=0, keepdims=True)
    before = jnp.sum(jnp.where((ends_w <= tile_id) & (tiles_w > 0.0), 1.0, 0.0), axis=0, keepdims=True)
    parity = before - 2.0 * jnp.floor(before * 0.5)
    seg_end = jnp.sum(jnp.where(mine, ends_w, 0.0), axis=0, keepdims=True)
    follows = (first_w == seg_end) & (tiles_w > 0.0)
    has_next = jnp.sum(jnp.where(follows, 1.0, 0.0), axis=0, keepdims=True)
    nxt = jnp.where(has_next > 0.0,
                    jnp.sum(jnp.where(follows, expert_id, 0.0), axis=0, keepdims=True), -1.0)
    meta_ref[...] = jnp.concatenate(
        [tile_expert, tile_rows, n_valid, fresh, parity, nxt, jnp.zeros((2, ntp), F32)], axis=0).astype(I32)


def _expert_tiling(counts, n_tiles):
    ntp = -(-n_tiles // LANES) * LANES
    meta, off = pl.pallas_call(
        _expert_tiling_kernel,
        out_shape=[jax.ShapeDtypeStruct((8, ntp), I32), jax.ShapeDtypeStruct((N_EXPERTS, LANES), I32)],
        name="expert_tiling",
    )(counts)
    schedule = tuple(meta[r, :n_tiles] for r in (0, 1, 3, 4, 5))
    return off[:, 0], schedule, meta[2, :1]


def kernel(x, positions, w_in, diff_lambda_q1, diff_lambda_k1, diff_lambda_q2, diff_lambda_k2, diff_subln_g, mla_q_norm_g, mla_w_uq, mla_kv_norm_g, mla_w_ukv, w_branch, w_out, ln_mix_g, ln_mix_b, router_w, router_bias, expert_w_gate, expert_w_up, expert_w_down, shared_w_gate, shared_w_up, shared_w_down, ln_ffn_g, ln_ffn_b):
    b, s, d = x.shape
    t = b * s
    depth = w_in.shape[0]
    cos, sin = _rope_tables(positions)
    xf = x.reshape(t, d)
    xb = xf.astype(BF16)
    w_in_t = jnp.swapaxes(w_in, 1, 2)
    n_tiles = t * TOPK // EXPERT_TM + N_EXPERTS
    proj_tm = min(PROJ_TM, t)

    for l in range(depth):
        gains = jnp.zeros((8, PROJ_TN), F32)
        gains = gains.at[0, :].set(mla_q_norm_g[l]).at[1, :MLA_KV_RANK].set(mla_kv_norm_g[l])
        proj = _proj(xb, 0, d, _in_proj_weight(w_in_t, l), IN_TILE_MODES, IN_TILE_SCALES,
                     cos, sin, gains, proj_tm, w_is_nk=True)
        qbuf = _proj(proj, 3, MLA_Q_RANK, _uq_weight(mla_w_uq[l]),
                     [MODE_PLAIN, MODE_PLAIN, MODE_ROPE_MLA], [(MLA_NOPE + MLA_ROPE) ** -0.5] * 3,
                     cos, sin, gains, proj_tm)
        kvbuf = _proj(proj, 4 * PROJ_TN // MLA_KV_RANK, MLA_KV_RANK, _ukv_weight(mla_w_ukv[l]),
                      [MODE_PLAIN] * 4, [1.0] * 4, cos, sin, gains, proj_tm)

        lambda_init = 0.8 - 0.6 * math.exp(-0.3 * l)
        lam_parts = jnp.stack([diff_lambda_q1[l], diff_lambda_k1[l],
                               diff_lambda_q2[l], diff_lambda_k2[l]]).astype(F32)
        oa = _diff_attention(proj, lam_parts, diff_subln_g[l], b, s, lambda_init)
        ob = _mla_attention(qbuf, kvbuf, proj, b, s)
        oc = _moba_attention(proj, b, s)

        x1, x1b, x1p = _merge(oa, ob, oc, proj, xf, w_branch[l].astype(BF16), w_out[l].astype(BF16),
                              ln_mix_g[l], ln_mix_b[l])

        e8, r8, w8, counts = _router(x1, router_w[l], router_bias[l])
        offsets, schedule, n_valid = _expert_tiling(counts, n_tiles)
        expert_ids = jnp.arange(N_EXPERTS, dtype=I32)[:, None, None]
        pos8 = jnp.sum(jnp.where(e8[None] == expert_ids, offsets[:, None, None], 0), axis=0) + r8
        pos8 = pos8.T.reshape(-1)
        xs = _dispatch(x1p, pos8, n_tiles * EXPERT_TM)
        ys = _experts(xs, schedule, n_valid,
                      expert_w_gate.reshape(depth * N_EXPERTS, d, EXPERT_FF),
                      expert_w_up.reshape(depth * N_EXPERTS, d, EXPERT_FF),
                      expert_w_down.reshape(depth * N_EXPERTS, EXPERT_FF, d), l)
        xf, xb = _combine(ys, pos8, w8.T, x1, x1b, shared_w_gate[l].astype(BF16),
                          shared_w_up[l].astype(BF16), shared_w_down[l].astype(BF16),
                          ln_ffn_g[l], ln_ffn_b[l])
    return xf.reshape(b, s, d)
```

```python
import functools
import math

import jax
import jax.numpy as jnp
from jax import lax
from jax.experimental import pallas as pl
from jax.experimental.pallas import tpu as pltpu

F32 = jnp.float32
BF16 = jnp.bfloat16
I32 = jnp.int32
U32 = jnp.uint32

D_MODEL = 2048
DEPTH = 2
ROPE_THETA = 500000.0
NORM_EPS = 1e-5

DIFF_HEADS = 4
DIFF_HALF_DIM = 64
DIFF_ROT = 16
MLA_HEADS = 8
MLA_Q_RANK = 512
MLA_KV_RANK = 256
MLA_NOPE = 128
MLA_ROPE = 64
MLA_V = 128
MOBA_HEADS = 4
MOBA_HEAD_DIM = 128
MOBA_BLOCK = 256
MOBA_TOPK = 3
MOBA_ROT = 32
A_WIDTH = 512
B_WIDTH = 1024
C_WIDTH = 512

N_EXPERTS = 64
N_GROUPS = 8
GROUP_SIZE = N_EXPERTS // N_GROUPS
TOPK_GROUPS = 4
TOPK = 8
EXPERT_FF = 512
ROUTED_SCALE = 2.5

DEEPNORM_ALPHA = (2 * DEPTH) ** 0.25

LANES = 128
NEG_BIG = -1e30

PROJ_TN = 512
MODE_PLAIN, MODE_ROPE_DIFF, MODE_ROPE_MLA, MODE_ROPE_MOBA, MODE_RMS, MODE_CKV, MODE_SIGMOID = range(7)
_ROPE_OF_MODE = {MODE_ROPE_DIFF: (0, DIFF_ROT // 2), MODE_ROPE_MLA: (1, MLA_ROPE // 2),
                 MODE_ROPE_MOBA: (2, MOBA_ROT // 2)}
_ROPE_PERIOD = (DIFF_HALF_DIM, MLA_ROPE, MOBA_HEAD_DIM)
_ROPE_ROT = (DIFF_ROT, MLA_ROPE, MOBA_ROT)

IN_TILE_MODES = ([MODE_ROPE_DIFF, MODE_ROPE_DIFF, MODE_PLAIN, MODE_RMS, MODE_CKV,
                  MODE_ROPE_MOBA, MODE_ROPE_MOBA, MODE_PLAIN] + [MODE_SIGMOID] * 12)
IN_TILE_SCALES = [DIFF_HALF_DIM ** -0.5] + [1.0] * 4 + [MOBA_HEAD_DIM ** -0.5] + [1.0] * 14
IN_COLS_PADDED = PROJ_TN * len(IN_TILE_MODES)

VMEM_LIMIT = 56 * 1024 * 1024

ATT_TQ = 256
PROJ_TM = 2048
MERGE_TM = 256
EXPERT_TM = 512
ROW_TM = 256
ROUTER_TM = 1024


def _cparams(sem):
    return pltpu.CompilerParams(dimension_semantics=sem, vmem_limit_bytes=VMEM_LIMIT)


def _rope_table_kernel(pos_ref, c_ref, cos_ref, sin_ref):
    pos = pos_ref[...].astype(F32)
    for p in range(3):
        ang = pos * c_ref[p:p + 1, :]
        cos_ref[p] = jnp.cos(ang)
        sin_ref[p] = jnp.sin(ang) * c_ref[3 + p:4 + p, :]


def _rope_tables(positions):
    t = positions.size
    lane = jnp.arange(LANES)
    rows = []
    signs = []
    for period, rot in zip(_ROPE_PERIOD, _ROPE_ROT):
        half = rot // 2
        inv_freq = ROPE_THETA ** (-jnp.arange(0, rot, 2, dtype=F32) / rot)
        cp = lane % period
        active = cp < rot
        rows.append(jnp.where(active, inv_freq[cp % half], 0.0))
        signs.append(jnp.where(active, jnp.where(cp < half, -1.0, 1.0), 0.0))
    consts = jnp.stack(rows + signs + [jnp.zeros((LANES,), F32)] * 2).astype(F32)
    tm = min(1024, t)
    cos, sin = pl.pallas_call(
        _rope_table_kernel,
        grid=(t // tm,),
        in_specs=[pl.BlockSpec((tm, 1), lambda i: (i, 0)),
                  pl.BlockSpec((8, LANES), lambda i: (0, 0))],
        out_specs=[pl.BlockSpec((3, tm, LANES), lambda i: (0, i, 0)),
                   pl.BlockSpec((3, tm, LANES), lambda i: (0, i, 0))],
        out_shape=[jax.ShapeDtypeStruct((3, t, LANES), F32)] * 2,
        compiler_params=_cparams(("parallel",)),
        name="rope_tables",
    )(positions.reshape(t, 1), consts)
    return cos, sin


def _proj_kernel(mode_ref, scale_ref, x_ref, w_ref, cos_ref, sin_ref, g_ref, o_ref, *, w_is_nk):
    j = pl.program_id(1)
    mode = mode_ref[j]
    if w_is_nk:
        y = lax.dot_general(x_ref[...], w_ref[...], (((1,), (1,)), ((), ())), preferred_element_type=F32)
    else:
        y = jnp.dot(x_ref[...], w_ref[...], preferred_element_type=F32)
    y = y * scale_ref[j]
    tm, tn = y.shape
    lane = lax.broadcasted_iota(I32, (tm, LANES), 1)

    def rope_chunk(yc, table, half):
        first = (lane % _ROPE_PERIOD[table]) < half
        swapped = jnp.where(first, pltpu.roll(yc, LANES - half, 1), pltpu.roll(yc, half, 1))
        return yc * cos_ref[table] + swapped * sin_ref[table]

    @pl.when(mode == MODE_PLAIN)
    def _():
        o_ref[...] = y.astype(o_ref.dtype)

    for rope_mode, (table, half) in _ROPE_OF_MODE.items():
        @pl.when(mode == rope_mode)
        def _(table=table, half=half):
            for c in range(tn // LANES):
                sl = slice(c * LANES, (c + 1) * LANES)
                o_ref[:, sl] = rope_chunk(y[:, sl], table, half).astype(o_ref.dtype)

    @pl.when(mode == MODE_RMS)
    def _():
        r = lax.rsqrt(jnp.mean(y * y, axis=-1, keepdims=True) + NORM_EPS)
        o_ref[...] = (y * r * g_ref[0:1, :]).astype(o_ref.dtype)

    @pl.when(mode == MODE_CKV)
    def _():
        ckv = y[:, :MLA_KV_RANK]
        r = lax.rsqrt(jnp.mean(ckv * ckv, axis=-1, keepdims=True) + NORM_EPS)
        o_ref[:, :MLA_KV_RANK] = (ckv * r * g_ref[1:2, :MLA_KV_RANK]).astype(o_ref.dtype)
        sl = slice(MLA_KV_RANK, MLA_KV_RANK + LANES)
        table, half = _ROPE_OF_MODE[MODE_ROPE_MLA]
        o_ref[:, sl] = rope_chunk(y[:, sl], table, half).astype(o_ref.dtype)
        o_ref[:, MLA_KV_RANK + LANES:] = jnp.zeros((tm, tn - MLA_KV_RANK - LANES), o_ref.dtype)

    @pl.when(mode == MODE_SIGMOID)
    def _():
        o_ref[...] = (0.5 * jnp.tanh(0.5 * y) + 0.5).astype(o_ref.dtype)


def _proj(x, x_col_block, k_dim, w, modes, scales, cos, sin, gains, tm, w_is_nk=False):
    t = x.shape[0]
    n = w.shape[0] if w_is_nk else w.shape[1]
    assert n % PROJ_TN == 0 and len(modes) == len(scales) == n // PROJ_TN
    assert w.shape == ((n, k_dim) if w_is_nk else (k_dim, n))
    w_spec = (pl.BlockSpec((PROJ_TN, k_dim), lambda i, j, m: (j, 0)) if w_is_nk
              else pl.BlockSpec((k_dim, PROJ_TN), lambda i, j, m: (0, j)))
    grid_spec = pltpu.PrefetchScalarGridSpec(
        num_scalar_prefetch=1,
        grid=(t // tm, n // PROJ_TN),
        in_specs=[pl.BlockSpec(memory_space=pltpu.SMEM),
                  pl.BlockSpec((tm, k_dim), lambda i, j, m: (i, x_col_block)),
                  w_spec,
                  pl.BlockSpec((3, tm, LANES), lambda i, j, m: (0, i, 0)),
                  pl.BlockSpec((3, tm, LANES), lambda i, j, m: (0, i, 0)),
                  pl.BlockSpec((8, PROJ_TN), lambda i, j, m: (0, 0))],
        out_specs=pl.BlockSpec((tm, PROJ_TN), lambda i, j, m: (i, j)),
    )
    return pl.pallas_call(
        functools.partial(_proj_kernel, w_is_nk=w_is_nk),
        grid_spec=grid_spec,
        out_shape=jax.ShapeDtypeStruct((t, n), BF16),
        compiler_params=_cparams(("parallel", "arbitrary")),
        name="proj",
    )(jnp.asarray(modes, I32), jnp.asarray(scales, F32), x, w, cos, sin, gains)


_NT_DIMS = (((1,), (1,)), ((), ()))


def _causal_mask(tq):
    row = lax.broadcasted_iota(I32, (tq, tq), 0)
    col = lax.broadcasted_iota(I32, (tq, tq), 1)
    return col <= row


def _score_strip(s_ref, i, tq, block_scores, block_mask):
    causal = _causal_mask(tq)
    for j in range(i + 1):
        sc = block_scores(j)
        if j == i:
            sc = jnp.where(causal, sc, NEG_BIG)
        elif block_mask is not None:
            sc = jnp.where(block_mask(j), sc, NEG_BIG)
        s_ref[:, j * tq:(j + 1) * tq] = sc


def _softmax_times_v(s_ref, p_ref, v_ref, n):
    tq = s_ref.shape[0]
    nch = n // LANES
    mrun = s_ref[:, 0:LANES]
    for c in range(1, nch):
        mrun = jnp.maximum(mrun, s_ref[:, c * LANES:(c + 1) * LANES])
    m = jnp.broadcast_to(jnp.max(mrun, axis=-1, keepdims=True), (tq, LANES))
    lrun = jnp.zeros((tq, LANES), F32)
    for c in range(nch):
        sl = slice(c * LANES, (c + 1) * LANES)
        p = jnp.exp(s_ref[:, sl] - m)
        lrun = lrun + p
        p_ref[:, sl] = p.astype(BF16)
    l = jnp.sum(lrun, axis=-1, keepdims=True)
    o = jnp.dot(p_ref[:, :n], v_ref[0:n, :], preferred_element_type=F32)
    return o / l


def _diff_attn_kernel(lam_ref, g_ref, q_ref, k_ref, v_ref, o_ref, s1_ref, p1_ref, s2_ref, p2_ref,
                      *, tq, lambda_init):
    nq = q_ref.shape[0] // tq
    lp = lam_ref[...]
    lam = (jnp.exp(jnp.sum(lp[0:1] * lp[1:2], axis=-1, keepdims=True))
           - jnp.exp(jnp.sum(lp[2:3] * lp[3:4], axis=-1, keepdims=True)) + lambda_init)
    lane = lax.broadcasted_iota(I32, (tq, LANES), 1)
    for i in range(nq):
        q = q_ref[i * tq:(i + 1) * tq, :]
        q1 = jnp.where(lane < DIFF_HALF_DIM, q, jnp.zeros_like(q))
        q2 = jnp.where(lane >= DIFF_HALF_DIM, q, jnp.zeros_like(q))
        kblk = lambda j: k_ref[j * tq:(j + 1) * tq, :]
        b = i % 2
        _score_strip(s1_ref.at[b], i, tq,
                     lambda j: lax.dot_general(q1, kblk(j), _NT_DIMS, preferred_element_type=F32), None)
        _score_strip(s2_ref.at[b], i, tq,
                     lambda j: lax.dot_general(q2, kblk(j), _NT_DIMS, preferred_element_type=F32), None)
        n = (i + 1) * tq
        o = (_softmax_times_v(s1_ref.at[b], p1_ref.at[b], v_ref, n)
             - lam * _softmax_times_v(s2_ref.at[b], p2_ref.at[b], v_ref, n))
        r = lax.rsqrt(jnp.mean(o * o, axis=-1, keepdims=True) + NORM_EPS)
        o_ref[i * tq:(i + 1) * tq, :] = (o * r * g_ref[...] * (1.0 - lambda_init)).astype(o_ref.dtype)


def _diff_attention(proj, lam_parts, subln_g, b, s, lambda_init):
    t = b * s
    tq = min(ATT_TQ, s)
    kern = functools.partial(_diff_attn_kernel, tq=tq, lambda_init=lambda_init)
    return pl.pallas_call(
        kern,
        grid=(b, DIFF_HEADS),
        in_specs=[pl.BlockSpec((4, DIFF_HALF_DIM), lambda bi, h: (0, 0)),
                  pl.BlockSpec((1, LANES), lambda bi, h: (0, 0)),
                  pl.BlockSpec((s, LANES), lambda bi, h: (bi, h)),
                  pl.BlockSpec((s, LANES), lambda bi, h: (bi, DIFF_HEADS + h)),
                  pl.BlockSpec((s, LANES), lambda bi, h: (bi, 2 * DIFF_HEADS + h))],
        out_specs=pl.BlockSpec((s, LANES), lambda bi, h: (bi, h)),
        out_shape=jax.ShapeDtypeStruct((t, A_WIDTH), BF16),
        scratch_shapes=[pltpu.VMEM((2, tq, s), F32), pltpu.VMEM((2, tq, s), BF16),
                        pltpu.VMEM((2, tq, s), F32), pltpu.VMEM((2, tq, s), BF16)],
        compiler_params=_cparams(("parallel", "parallel")),
        name="diff_attention",
    )(lam_parts, subln_g.reshape(1, LANES), proj, proj, proj)


def _mla_attn_kernel(qn_ref, qr_ref, kn_ref, kpe_ref, v_ref, o_ref, s_ref, p_ref, *, tq):
    h = pl.program_id(1)
    nq = qn_ref.shape[0] // tq
    lane = lax.broadcasted_iota(I32, (tq, LANES), 1)
    mine = (lane >= MLA_ROPE) == (h % 2 == 1)
    for i in range(nq):
        qn = qn_ref[i * tq:(i + 1) * tq, :]
        qr = qr_ref[i * tq:(i + 1) * tq, :]
        qr = jnp.where(mine, qr, jnp.zeros_like(qr))

        qcat = jnp.concatenate([qn, qr], axis=1)

        def block_scores(j, qcat=qcat):
            rows = slice(j * tq, (j + 1) * tq)
            kcat = jnp.concatenate([kn_ref[rows, :], kpe_ref[rows, :]], axis=1)
            return lax.dot_general(qcat, kcat, _NT_DIMS, preferred_element_type=F32)

        _score_strip(s_ref.at[i % 2], i, tq, block_scores, None)
        o = _softmax_times_v(s_ref.at[i % 2], p_ref.at[i % 2], v_ref, (i + 1) * tq)
        o_ref[i * tq:(i + 1) * tq, :] = o.astype(o_ref.dtype)


def _mla_attention(qbuf, kvbuf, proj, b, s):
    t = b * s
    tq = min(ATT_TQ, s)
    kpe_block = (4 * PROJ_TN + MLA_KV_RANK) // LANES
    return pl.pallas_call(
        functools.partial(_mla_attn_kernel, tq=tq),
        grid=(b, MLA_HEADS),
        in_specs=[pl.BlockSpec((s, LANES), lambda bi, h: (bi, h)),
                  pl.BlockSpec((s, LANES), lambda bi, h: (bi, MLA_HEADS + h // 2)),
                  pl.BlockSpec((s, LANES), lambda bi, h: (bi, h)),
                  pl.BlockSpec((s, LANES), lambda bi, h: (bi, kpe_block)),
                  pl.BlockSpec((s, LANES), lambda bi, h: (bi, MLA_HEADS + h))],
        out_specs=pl.BlockSpec((s, LANES), lambda bi, h: (bi, h)),
        out_shape=jax.ShapeDtypeStruct((t, B_WIDTH), BF16),
        scratch_shapes=[pltpu.VMEM((2, tq, s), F32), pltpu.VMEM((2, tq, s), BF16)],
        compiler_params=_cparams(("parallel", "parallel")),
        name="mla_attention",
    )(qbuf, qbuf, kvbuf, proj, kvbuf)


def _moba_attn_kernel(q_ref, k_ref, v_ref, o_ref, s_ref, p_ref, kmean_ref):
    tq = MOBA_BLOCK
    nb = q_ref.shape[0] // tq
    kmean_ref[...] = jnp.zeros(kmean_ref.shape, F32)
    for j in range(nb):
        kb = k_ref[j * tq:(j + 1) * tq, :].astype(F32)
        kmean_ref[j:j + 1, :] = jnp.sum(kb, axis=0, keepdims=True) * (1.0 / tq)
    km = kmean_ref[...]
    km_hi = km.astype(BF16)
    km_lo = (km - km_hi.astype(F32)).astype(BF16)
    nbp = -(-nb // 8) * 8
    blk = lax.broadcasted_iota(I32, (nbp, tq), 0)
    for i in range(nb):
        q = q_ref[i * tq:(i + 1) * tq, :]
        gate = (lax.dot_general(km_hi, q, _NT_DIMS, preferred_element_type=F32)
                + lax.dot_general(km_lo, q, _NT_DIMS, preferred_element_type=F32))[:nbp, :]
        fully_past = blk < i
        g = jnp.where(fully_past, gate, -jnp.inf)
        cnt = jnp.zeros(g.shape, F32)
        for jp in range(i):
            row = g[jp:jp + 1, :]
            ahead = (row > g) | ((row == g) & (jp < blk))
            cnt = cnt + jnp.where(ahead, 1.0, 0.0)
        kept_t = jnp.where((cnt < MOBA_TOPK) & fully_past, 1.0, 0.0)
        kept_t = jnp.concatenate([kept_t, jnp.zeros((LANES - nbp, tq), F32)], axis=0)
        kept = kept_t.T > 0.5
        _score_strip(s_ref.at[i % 2], i, tq,
                     lambda j, q=q: lax.dot_general(q, k_ref[j * tq:(j + 1) * tq, :], _NT_DIMS,
                                                    preferred_element_type=F32),
                     lambda j, kept=kept: kept[:, j:j + 1])
        o = _softmax_times_v(s_ref.at[i % 2], p_ref.at[i % 2], v_ref, (i + 1) * tq)
        o_ref[i * tq:(i + 1) * tq, :] = o.astype(o_ref.dtype)


def _moba_attention(proj, b, s):
    assert s % MOBA_BLOCK == 0 and s // MOBA_BLOCK <= LANES
    t = b * s
    base = 5 * PROJ_TN // LANES
    return pl.pallas_call(
        _moba_attn_kernel,
        grid=(b, MOBA_HEADS),
        in_specs=[pl.BlockSpec((s, LANES), lambda bi, h: (bi, base + h)),
                  pl.BlockSpec((s, LANES), lambda bi, h: (bi, base + MOBA_HEADS + h)),
                  pl.BlockSpec((s, LANES), lambda bi, h: (bi, base + 2 * MOBA_HEADS + h))],
        out_specs=pl.BlockSpec((s, LANES), lambda bi, h: (bi, h)),
        out_shape=jax.ShapeDtypeStruct((t, C_WIDTH), BF16),
        scratch_shapes=[pltpu.VMEM((2, MOBA_BLOCK, s), F32), pltpu.VMEM((2, MOBA_BLOCK, s), BF16),
                        pltpu.VMEM((LANES, LANES), F32)],
        compiler_params=_cparams(("parallel", "parallel")),
        name="moba_attention",
    )(proj, proj, proj)


def _layer_norm_rows(z, g, b):
    mu = jnp.mean(z, axis=-1, keepdims=True)
    zc = z - mu
    var = jnp.mean(zc * zc, axis=-1, keepdims=True)
    return zc * lax.rsqrt(var + NORM_EPS) * g + b


def _pack_halves(z):
    n = z.shape[1] // 2
    lo = lax.bitcast_convert_type(z[:, :n].astype(BF16).astype(F32), U32)
    hi = lax.bitcast_convert_type(z[:, n:].astype(BF16).astype(F32), U32)
    return (lo >> 16) | hi


def _unpack_halves(u):
    lo = lax.bitcast_convert_type(u << 16, F32)
    hi = lax.bitcast_convert_type(u & jnp.uint32(0xFFFF0000), F32)
    return lo, hi


ROW_WORDS = D_MODEL // 2
ROW_SUBLANES = ROW_WORDS // LANES


def _store_row_tiles(ref, packed):
    tm = packed.shape[0]
    for a in range(ROW_SUBLANES):
        ref[pl.ds(a, tm, stride=ROW_SUBLANES), :] = packed[:, a * LANES:(a + 1) * LANES]


def _load_row_tile_chunk(ref, a, tm):
    return ref[pl.ds(a, tm, stride=ROW_SUBLANES), :]


def _merge_kernel(oa_ref, ob_ref, oc_ref, g0_ref, g1_ref, g2_ref, x_ref, wb_ref, wo_ref,
                  lng_ref, lnb_ref, xo_ref, xb_ref, xp_ref):
    ya = jnp.dot(oa_ref[...], wb_ref[:A_WIDTH, :], preferred_element_type=F32)
    yb = jnp.dot(ob_ref[...], wb_ref[A_WIDTH:A_WIDTH + B_WIDTH, :], preferred_element_type=F32)
    yc = jnp.dot(oc_ref[...], wb_ref[A_WIDTH + B_WIDTH:, :], preferred_element_type=F32)
    y = g0_ref[...].astype(F32) * ya + g1_ref[...].astype(F32) * yb + g2_ref[...].astype(F32) * yc
    mix = jnp.dot(y.astype(BF16), wo_ref[...], preferred_element_type=F32)
    z = _layer_norm_rows(DEEPNORM_ALPHA * x_ref[...] + mix, lng_ref[...], lnb_ref[...])
    xo_ref[...] = z
    xb_ref[...] = z.astype(BF16)
    _store_row_tiles(xp_ref, _pack_halves(z))


def _merge(oa, ob, oc, proj, x, w_branch, w_out, ln_g, ln_b):
    t = x.shape[0]
    tm = min(MERGE_TM, t)
    gate_base = 8 * PROJ_TN // D_MODEL
    row = lambda i: (i, 0)
    whole = lambda i: (0, 0)
    resident = dict(pipeline_mode=pl.Buffered(1))
    return pl.pallas_call(
        _merge_kernel,
        grid=(t // tm,),
        in_specs=[pl.BlockSpec((tm, A_WIDTH), row),
                  pl.BlockSpec((tm, B_WIDTH), row),
                  pl.BlockSpec((tm, C_WIDTH), row),
                  pl.BlockSpec((tm, D_MODEL), lambda i: (i, gate_base)),
                  pl.BlockSpec((tm, D_MODEL), lambda i: (i, gate_base + 1)),
                  pl.BlockSpec((tm, D_MODEL), lambda i: (i, gate_base + 2)),
                  pl.BlockSpec((tm, D_MODEL), row),
                  pl.BlockSpec((D_MODEL, D_MODEL), whole, **resident),
                  pl.BlockSpec((D_MODEL, D_MODEL), whole, **resident),
                  pl.BlockSpec((1, D_MODEL), whole),
                  pl.BlockSpec((1, D_MODEL), whole)],
        out_specs=[pl.BlockSpec((tm, D_MODEL), row),
                   pl.BlockSpec((tm, D_MODEL), row),
                   pl.BlockSpec((tm * ROW_SUBLANES, LANES), row)],
        out_shape=[jax.ShapeDtypeStruct((t, D_MODEL), F32),
                   jax.ShapeDtypeStruct((t, D_MODEL), BF16),
                   jax.ShapeDtypeStruct((t * ROW_SUBLANES, LANES), U32)],
        compiler_params=_cparams(("parallel",)),
        name="merge_outproj_ln",
    )(oa, ob, oc, proj, proj, proj, x, w_branch, w_out, ln_g.reshape(1, -1), ln_b.reshape(1, -1))


def _split_bf16(a):
    hi = a.astype(BF16)
    lo = (a - hi.astype(F32)).astype(BF16)
    return hi, lo


def _router_kernel(x_ref, wt_ref, bias_ref, upper_ref, lower_ref,
                   e8_ref, r8_ref, w8_ref, cnt_ref, carry_ref):
    i = pl.program_id(0)

    @pl.when(i == 0)
    def _():
        carry_ref[...] = jnp.zeros(carry_ref.shape, F32)

    xh, xl = _split_bf16(x_ref[...])
    wh, wl = _split_bf16(wt_ref[...])
    logits = (lax.dot_general(wh, xh, _NT_DIMS, preferred_element_type=F32)
              + lax.dot_general(wh, xl, _NT_DIMS, preferred_element_type=F32)
              + lax.dot_general(wl, xh, _NT_DIMS, preferred_element_type=F32))
    scores = 1.0 / (1.0 + jnp.exp(-logits))
    choice = scores + bias_ref[:, 0:1]
    tm = choice.shape[1]
    sub = lax.broadcasted_iota(I32, (GROUP_SIZE, tm), 0)

    group_rows = []
    for g in range(N_GROUPS):
        cg = choice[g * GROUP_SIZE:(g + 1) * GROUP_SIZE, :]
        m1 = jnp.max(cg, axis=0, keepdims=True)
        first = jnp.min(jnp.where(cg == m1, sub, GROUP_SIZE), axis=0, keepdims=True)
        m2 = jnp.max(jnp.where(sub == first, -jnp.inf, cg), axis=0, keepdims=True)
        group_rows.append(m1 + m2)
    gs = jnp.concatenate(group_rows, axis=0)
    ahead = jnp.zeros(gs.shape, F32)
    for gp in range(N_GROUPS):
        rowv = gs[gp:gp + 1, :]
        ahead = ahead + jnp.where((rowv > gs) | ((rowv == gs) & (gp < sub)), 1.0, 0.0)
    keep_group = ahead < TOPK_GROUPS
    masked = jnp.concatenate(
        [jnp.where(keep_group[g:g + 1, :], choice[g * GROUP_SIZE:(g + 1) * GROUP_SIZE, :], -jnp.inf)
         for g in range(N_GROUPS)], axis=0)

    eidx = lax.broadcasted_iota(I32, masked.shape, 0)
    remaining = masked
    self32 = jnp.zeros(masked.shape, F32)
    for _ in range(TOPK):
        top = jnp.max(remaining, axis=0, keepdims=True)
        first = jnp.min(jnp.where(remaining == top, eidx, N_EXPERTS), axis=0, keepdims=True)
        hit = eidx == first
        self32 = jnp.where(hit, 1.0, self32)
        remaining = jnp.where(hit, -jnp.inf, remaining)
    sel = self32 > 0.5
    picked = jnp.where(sel, scores, 0.0)
    gates = picked / jnp.sum(picked, axis=0, keepdims=True) * ROUTED_SCALE
    selb = self32.astype(BF16)

    rank = jnp.dot(selb, upper_ref[...], preferred_element_type=F32) + carry_ref[:, 0:1]
    slot = jnp.dot(lower_ref[...], selb, preferred_element_type=F32)
    carry_ref[...] = carry_ref[...] + jnp.sum(self32, axis=1, keepdims=True)
    cnt_ref[...] = carry_ref[...].astype(I32)

    eidf = eidx.astype(F32)
    e_rows, r_rows, w_rows = [], [], []
    for k in range(TOPK):
        hit = sel & (slot == float(k))
        e_rows.append(jnp.sum(jnp.where(hit, eidf, 0.0), axis=0, keepdims=True))
        r_rows.append(jnp.sum(jnp.where(hit, rank, 0.0), axis=0, keepdims=True))
        w_rows.append(jnp.sum(jnp.where(hit, gates, 0.0), axis=0, keepdims=True))
    e8_ref[...] = jnp.concatenate(e_rows, axis=0).astype(I32)
    r8_ref[...] = jnp.concatenate(r_rows, axis=0).astype(I32)
    w8_ref[...] = jnp.concatenate(w_rows, axis=0)


def _router(x, router_w, router_bias):
    t = x.shape[0]
    tm = min(ROUTER_TM, t)
    upper = (jnp.arange(tm)[:, None] < jnp.arange(tm)[None, :]).astype(BF16)
    lower = (jnp.arange(N_EXPERTS)[None, :] < jnp.arange(N_EXPERTS)[:, None]).astype(BF16)
    bias = jnp.broadcast_to(router_bias.astype(F32)[:, None], (N_EXPERTS, LANES))
    whole = lambda i: (0, 0)
    col = lambda i: (0, i)
    return pl.pallas_call(
        _router_kernel,
        grid=(t // tm,),
        in_specs=[pl.BlockSpec((tm, D_MODEL), lambda i: (i, 0)),
                  pl.BlockSpec((N_EXPERTS, D_MODEL), whole),
                  pl.BlockSpec((N_EXPERTS, LANES), whole),
                  pl.BlockSpec((tm, tm), whole),
                  pl.BlockSpec((N_EXPERTS, N_EXPERTS), whole)],
        out_specs=[pl.BlockSpec((TOPK, tm), col),
                   pl.BlockSpec((TOPK, tm), col),
                   pl.BlockSpec((TOPK, tm), col),
                   pl.BlockSpec((N_EXPERTS, LANES), whole)],
        out_shape=[jax.ShapeDtypeStruct((TOPK, t), I32),
                   jax.ShapeDtypeStruct((TOPK, t), I32),
                   jax.ShapeDtypeStruct((TOPK, t), F32),
                   jax.ShapeDtypeStruct((N_EXPERTS, LANES), I32)],
        scratch_shapes=[pltpu.VMEM((N_EXPERTS, LANES), F32)],
        compiler_params=_cparams(("arbitrary",)),
        name="router",
    )(x, router_w.T, bias, upper, lower)


def _row_tile(ref, r):
    return ref.at[pl.ds(pl.multiple_of(r * ROW_SUBLANES, ROW_SUBLANES), ROW_SUBLANES)]


def _dispatch_kernel(pos_ref, x_ref, xs_ref, sem):
    tm = x_ref.shape[0] // ROW_SUBLANES

    def row_copy(t, k):
        return pltpu.make_async_copy(_row_tile(x_ref, t), _row_tile(xs_ref, pos_ref[t * TOPK + k]), sem)

    def start(t, c):
        for k in range(TOPK):
            row_copy(t, k).start(priority=k % 2)
        return c
    lax.fori_loop(0, tm, start, 0)

    for k in range(TOPK):
        pltpu.make_async_copy(x_ref, xs_ref.at[pl.ds(0, tm * ROW_SUBLANES)], sem).wait()


def _dispatch(xp, pos8, n_rows):
    t = xp.shape[0] // ROW_SUBLANES
    tm = min(ROW_TM, t)
    return pl.pallas_call(
        _dispatch_kernel,
        grid=(t // tm,),
        in_specs=[pl.BlockSpec((tm * TOPK,), lambda i: (i,), memory_space=pltpu.SMEM),
                  pl.BlockSpec((tm * ROW_SUBLANES, LANES), lambda i: (i, 0))],
        out_specs=pl.BlockSpec(memory_space=pl.ANY),
        scratch_shapes=[pltpu.SemaphoreType.DMA(())],
        out_shape=jax.ShapeDtypeStruct((n_rows * ROW_SUBLANES, LANES), U32),
        compiler_params=_cparams(("arbitrary",)),
        name="dispatch",
    )(pos8, xp)


def _expert_kernel(te_ref, rows_ref, fresh_ref, par_ref, nxt_ref, nv_ref, xs_ref, wg_hbm, wu_hbm, wd_hbm,
                   ys_ref, wgf_ref, wuf_ref, wdf_ref, wgb_ref, wub_ref, wdb_ref, sem, *, first_expert):
    i = pl.program_id(0)

    def weight_copies(e, s):
        return [pltpu.make_async_copy(hbm.at[first_expert + e], buf.at[s], sem.at[s])
                for hbm, buf in ((wg_hbm, wgf_ref), (wu_hbm, wuf_ref), (wd_hbm, wdf_ref))]

    @pl.when(i < nv_ref[0])
    def _():
        @pl.when(fresh_ref[i] == 1)
        def _():
            s = par_ref[i]

            @pl.when(i == 0)
            def _():
                for c in weight_copies(te_ref[0], 0):
                    c.start()

            for c in weight_copies(te_ref[i], s):
                c.wait()

            @pl.when(nxt_ref[i] >= 0)
            def _():
                for c in weight_copies(nxt_ref[i], 1 - s):
                    c.start()

            wgb_ref[...] = wgf_ref[s].astype(BF16)
            wub_ref[...] = wuf_ref[s].astype(BF16)
            wdb_ref[...] = wdf_ref[s].astype(BF16)

        half = D_MODEL // 2

        def ffn(tm):
            u = jnp.concatenate([_load_row_tile_chunk(xs_ref, a, tm) for a in range(ROW_SUBLANES)], axis=1)
            live = lax.broadcasted_iota(I32, (tm, 1), 0) < rows_ref[i]
            lo, hi = _unpack_halves(jnp.where(live, u, jnp.zeros_like(u)))
            xl = lo.astype(BF16)
            xh = hi.astype(BF16)
            gate = (jnp.dot(xl, wgb_ref[:half, :], preferred_element_type=F32)
                    + jnp.dot(xh, wgb_ref[half:, :], preferred_element_type=F32))
            up = (jnp.dot(xl, wub_ref[:half, :], preferred_element_type=F32)
                  + jnp.dot(xh, wub_ref[half:, :], preferred_element_type=F32))
            hid = (gate / (1.0 + jnp.exp(-gate))) * up
            y = jnp.dot(hid.astype(BF16), wdb_ref[...], preferred_element_type=F32)
            _store_row_tiles(ys_ref, _pack_halves(y))

        @pl.when(rows_ref[i] > EXPERT_TM // 2)
        def _():
            ffn(EXPERT_TM)

        @pl.when(rows_ref[i] <= EXPERT_TM // 2)
        def _():
            ffn(EXPERT_TM // 2)


def _experts(xs, schedule, n_valid, w_gate, w_up, w_down, layer):
    n_rows = xs.shape[0] // ROW_SUBLANES
    n_tiles = n_rows // EXPERT_TM
    row = lambda i, te, tr, fr, pa, nx, nv: (jnp.minimum(i, nv[0] - 1), 0)
    grid_spec = pltpu.PrefetchScalarGridSpec(
        num_scalar_prefetch=6,
        grid=(n_tiles,),
        in_specs=[pl.BlockSpec((EXPERT_TM * ROW_SUBLANES, LANES), row),
                  pl.BlockSpec(memory_space=pl.ANY),
                  pl.BlockSpec(memory_space=pl.ANY),
                  pl.BlockSpec(memory_space=pl.ANY)],
        out_specs=pl.BlockSpec((EXPERT_TM * ROW_SUBLANES, LANES), row),
        scratch_shapes=[pltpu.VMEM((2, D_MODEL, EXPERT_FF), F32), pltpu.VMEM((2, D_MODEL, EXPERT_FF), F32),
                        pltpu.VMEM((2, EXPERT_FF, D_MODEL), F32),
                        pltpu.VMEM((D_MODEL, EXPERT_FF), BF16), pltpu.VMEM((D_MODEL, EXPERT_FF), BF16),
                        pltpu.VMEM((EXPERT_FF, D_MODEL), BF16),
                        pltpu.SemaphoreType.DMA((2,))],
    )
    return pl.pallas_call(
        functools.partial(_expert_kernel, first_expert=layer * N_EXPERTS),
        grid_spec=grid_spec,
        out_shape=jax.ShapeDtypeStruct((n_rows * ROW_SUBLANES, LANES), U32),
        compiler_params=_cparams(("arbitrary",)),
        name="experts",
    )(*schedule, n_valid, xs, w_gate, w_up, w_down)


def _combine_kernel(pos_ref, pos_next_ref, w8_ref, ys_ref, x_ref, xb_ref, wsg_ref, wsu_ref, wsd_ref,
                    lng_ref, lnb_ref, xo_ref, xob_ref, buf_ref, sem):
    i = pl.program_id(0)
    tm = x_ref.shape[0]
    slot = i % 2

    def row_copy(p_ref, s, t, k):
        return pltpu.make_async_copy(_row_tile(ys_ref, p_ref[t * TOPK + k]), _row_tile(buf_ref.at[s, k], t),
                                     sem.at[s])

    def start_all(p_ref, s):
        def start(t, c):
            for k in range(TOPK):
                row_copy(p_ref, s, t, k).start(priority=k % 2)
            return c
        lax.fori_loop(0, tm, start, 0)

    @pl.when(i == 0)
    def _():
        start_all(pos_ref, 0)

    @pl.when(i + 1 < pl.num_programs(0))
    def _():
        start_all(pos_next_ref, 1 - slot)

    xb = xb_ref[...]
    sg = jnp.dot(xb, wsg_ref[...], preferred_element_type=F32)
    su = jnp.dot(xb, wsu_ref[...], preferred_element_type=F32)
    hid = (sg / (1.0 + jnp.exp(-sg))) * su
    shared = jnp.dot(hid.astype(BF16), wsd_ref[...], preferred_element_type=F32)

    for k in range(TOPK):
        pltpu.make_async_copy(ys_ref.at[pl.ds(0, tm * ROW_SUBLANES)], buf_ref.at[slot, k], sem.at[slot]).wait()

    w8 = w8_ref[...]
    wk = [jnp.broadcast_to(w8[:, k:k + 1], (tm, LANES)) for k in range(TOPK)]
    lo_parts, hi_parts = [], []
    for a in range(ROW_SUBLANES):
        acc_lo = jnp.zeros((tm, LANES), F32)
        acc_hi = jnp.zeros((tm, LANES), F32)
        for k in range(TOPK):
            lo, hi = _unpack_halves(_load_row_tile_chunk(buf_ref.at[slot, k], a, tm))
            acc_lo = acc_lo + wk[k] * lo
            acc_hi = acc_hi + wk[k] * hi
        lo_parts.append(acc_lo)
        hi_parts.append(acc_hi)
    routed = jnp.concatenate(lo_parts + hi_parts, axis=1)
    z = _layer_norm_rows(DEEPNORM_ALPHA * x_ref[...] + (routed + shared), lng_ref[...], lnb_ref[...])
    xo_ref[...] = z
    xob_ref[...] = z.astype(BF16)


def _combine(ys, pos8, w8t, x, xb, ws_gate, ws_up, ws_down, ln_g, ln_b):
    t = x.shape[0]
    tm = min(ROW_TM, t)
    row = lambda i: (i, 0)
    whole = lambda i: (0, 0)
    n_steps = t // tm
    grid_spec = pl.GridSpec(
        grid=(n_steps,),
        in_specs=[pl.BlockSpec((tm * TOPK,), lambda i: (i,), memory_space=pltpu.SMEM),
                  pl.BlockSpec((tm * TOPK,), lambda i: (jnp.minimum(i + 1, n_steps - 1),),
                               memory_space=pltpu.SMEM),
                  pl.BlockSpec((tm, TOPK), row),
                  pl.BlockSpec(memory_space=pl.ANY),
                  pl.BlockSpec((tm, D_MODEL), row),
                  pl.BlockSpec((tm, D_MODEL), row),
                  pl.BlockSpec((D_MODEL, EXPERT_FF), whole),
                  pl.BlockSpec((D_MODEL, EXPERT_FF), whole),
                  pl.BlockSpec((EXPERT_FF, D_MODEL), whole),
                  pl.BlockSpec((1, D_MODEL), whole),
                  pl.BlockSpec((1, D_MODEL), whole)],
        out_specs=[pl.BlockSpec((tm, D_MODEL), row),
                   pl.BlockSpec((tm, D_MODEL), row)],
        scratch_shapes=[pltpu.VMEM((2, TOPK, tm * ROW_SUBLANES, LANES), U32),
                        pltpu.SemaphoreType.DMA((2,))],
    )
    return pl.pallas_call(
        _combine_kernel,
        grid_spec=grid_spec,
        out_shape=[jax.ShapeDtypeStruct((t, D_MODEL), F32),
                   jax.ShapeDtypeStruct((t, D_MODEL), BF16)],
        compiler_params=_cparams(("arbitrary",)),
        name="combine_shared_ln",
    )(pos8, pos8, w8t, ys, x, xb, ws_gate, ws_up, ws_down, ln_g.reshape(1, -1), ln_b.reshape(1, -1))


_CKV_TILE = 4
_TAIL_SHIFT = PROJ_TN - MLA_KV_RANK - MLA_ROPE


def _in_proj_weight_kernel(prev_ref, cur_ref, o_ref):
    j = pl.program_id(1)

    @pl.when(j < _CKV_TILE)
    def _():
        o_ref[...] = cur_ref[0].astype(BF16)

    @pl.when(j == _CKV_TILE)
    def _():
        w = cur_ref[0]
        kpe = w[MLA_KV_RANK:MLA_KV_RANK + MLA_ROPE, :]
        pad = jnp.zeros((_TAIL_SHIFT - MLA_ROPE, w.shape[1]), F32)
        o_ref[...] = jnp.concatenate([w[:MLA_KV_RANK, :], kpe, kpe, pad], axis=0).astype(BF16)

    @pl.when(j > _CKV_TILE)
    def _():
        keep = PROJ_TN - _TAIL_SHIFT
        o_ref[...] = jnp.concatenate([prev_ref[0][keep:, :], cur_ref[0][:keep, :]], axis=0).astype(BF16)


def _in_proj_weight(w_in_t_all, layer):
    d = w_in_t_all.shape[2]
    return pl.pallas_call(
        _in_proj_weight_kernel,
        grid=(1, IN_COLS_PADDED // PROJ_TN),
        in_specs=[pl.BlockSpec((1, PROJ_TN, d), lambda i, j: (layer, jnp.maximum(j - 1, 0), 0)),
                  pl.BlockSpec((1, PROJ_TN, d), lambda i, j: (layer, j, 0))],
        out_specs=pl.BlockSpec((PROJ_TN, d), lambda i, j: (j, 0)),
        out_shape=jax.ShapeDtypeStruct((IN_COLS_PADDED, d), BF16),
        compiler_params=_cparams(("parallel", "parallel")),
        name="in_proj_weight",
    )(w_in_t_all, w_in_t_all)


def _uq_weight(w_uq):
    w = w_uq.reshape(MLA_Q_RANK, MLA_HEADS, MLA_NOPE + MLA_ROPE)
    return jnp.concatenate([w[:, :, :MLA_NOPE].reshape(MLA_Q_RANK, -1),
                            w[:, :, MLA_NOPE:].reshape(MLA_Q_RANK, -1)], axis=1).astype(BF16)


def _ukv_weight(w_ukv):
    w = w_ukv.reshape(MLA_KV_RANK, MLA_HEADS, MLA_NOPE + MLA_V)
    return jnp.concatenate([w[:, :, :MLA_NOPE].reshape(MLA_KV_RANK, -1),
                            w[:, :, MLA_NOPE:].reshape(MLA_KV_RANK, -1)], axis=1).astype(BF16)


def _expert_tiling_kernel(cnt_ref, meta_ref, off_ref):
    ntp = meta_ref.shape[1]
    cnt = cnt_ref[...].astype(F32)
    tiles = jnp.floor((cnt + (EXPERT_TM - 1)) * (1.0 / EXPERT_TM))
    row = lax.broadcasted_iota(I32, (N_EXPERTS, N_EXPERTS), 0)
    col = lax.broadcasted_iota(I32, (N_EXPERTS, N_EXPERTS), 1)
    incl = jnp.where(col <= row, 1.0, 0.0).astype(BF16)
    ends = jnp.dot(incl, tiles.astype(BF16), preferred_element_type=F32)
    first = ends - tiles
    off_ref[...] = (first * EXPERT_TM).astype(I32)

    def widen(a):
        return jnp.concatenate([a] * (ntp // LANES), axis=1)
    tile_id = lax.broadcasted_iota(I32, (N_EXPERTS, ntp), 1).astype(F32)
    ends_w, first_w, cnt_w = widen(ends), widen(first), widen(cnt)
    tile_expert = jnp.minimum(jnp.sum(jnp.where(ends_w <= tile_id, 1.0, 0.0), axis=0, keepdims=True),
                              N_EXPERTS - 1.0)
    mine = (first_w <= tile_id) & (tile_id < ends_w)
    rows = jnp.clip(cnt_w - (tile_id - first_w) * EXPERT_TM, 0.0, float(EXPERT_TM))
    tile_rows = jnp.sum(jnp.where(mine, rows, 0.0), axis=0, keepdims=True)
    n_valid = widen(ends[N_EXPERTS - 1:N_EXPERTS, :])
    tiles_w = widen(tiles)
    expert_id = lax.broadcasted_iota(I32, (N_EXPERTS, ntp), 0).astype(F32)
    fresh = jnp.sum(jnp.where(mine & (tile_id == first_w), 1.0, 0.0), axis=0, keepdims=True)
    before = jnp.sum(jnp.where((ends_w <= tile_id) & (tiles_w > 0.0), 1.0, 0.0), axis=0, keepdims=True)
    parity = before - 2.0 * jnp.floor(before * 0.5)
    seg_end = jnp.sum(jnp.where(mine, ends_w, 0.0), axis=0, keepdims=True)
    follows = (first_w == seg_end) & (tiles_w > 0.0)
    has_next = jnp.sum(jnp.where(follows, 1.0, 0.0), axis=0, keepdims=True)
    nxt = jnp.where(has_next > 0.0,
                    jnp.sum(jnp.where(follows, expert_id, 0.0), axis=0, keepdims=True), -1.0)
    meta_ref[...] = jnp.concatenate(
        [tile_expert, tile_rows, n_valid, fresh, parity, nxt, jnp.zeros((2, ntp), F32)], axis=0).astype(I32)


def _expert_tiling(counts, n_tiles):
    ntp = -(-n_tiles // LANES) * LANES
    meta, off = pl.pallas_call(
        _expert_tiling_kernel,
        out_shape=[jax.ShapeDtypeStruct((8, ntp), I32), jax.ShapeDtypeStruct((N_EXPERTS, LANES), I32)],
        name="expert_tiling",
    )(counts)
    schedule = tuple(meta[r, :n_tiles] for r in (0, 1, 3, 4, 5))
    return off[:, 0], schedule, meta[2, :1]


def kernel(x, positions, w_in, diff_lambda_q1, diff_lambda_k1, diff_lambda_q2, diff_lambda_k2, diff_subln_g, mla_q_norm_g, mla_w_uq, mla_kv_norm_g, mla_w_ukv, w_branch, w_out, ln_mix_g, ln_mix_b, router_w, router_bias, expert_w_gate, expert_w_up, expert_w_down, shared_w_gate, shared_w_up, shared_w_down, ln_ffn_g, ln_ffn_b):
    b, s, d = x.shape
    t = b * s
    depth = w_in.shape[0]
    cos, sin = _rope_tables(positions)
    xf = x.reshape(t, d)
    xb = xf.astype(BF16)
    w_in_t = jnp.swapaxes(w_in, 1, 2)
    n_tiles = t * TOPK // EXPERT_TM + N_EXPERTS
    proj_tm = min(PROJ_TM, t)

    for l in range(depth):
        gains = jnp.zeros((8, PROJ_TN), F32)
        gains = gains.at[0, :].set(mla_q_norm_g[l]).at[1, :MLA_KV_RANK].set(mla_kv_norm_g[l])
        proj = _proj(xb, 0, d, _in_proj_weight(w_in_t, l), IN_TILE_MODES, IN_TILE_SCALES,
                     cos, sin, gains, proj_tm, w_is_nk=True)
        qbuf = _proj(proj, 3, MLA_Q_RANK, _uq_weight(mla_w_uq[l]),
                     [MODE_PLAIN, MODE_PLAIN, MODE_ROPE_MLA], [(MLA_NOPE + MLA_ROPE) ** -0.5] * 3,
                     cos, sin, gains, proj_tm)
        kvbuf = _proj(proj, 4 * PROJ_TN // MLA_KV_RANK, MLA_KV_RANK, _ukv_weight(mla_w_ukv[l]),
                      [MODE_PLAIN] * 4, [1.0] * 4, cos, sin, gains, proj_tm)

        lambda_init = 0.8 - 0.6 * math.exp(-0.3 * l)
        lam_parts = jnp.stack([diff_lambda_q1[l], diff_lambda_k1[l],
                               diff_lambda_q2[l], diff_lambda_k2[l]]).astype(F32)
        oa = _diff_attention(proj, lam_parts, diff_subln_g[l], b, s, lambda_init)
        ob = _mla_attention(qbuf, kvbuf, proj, b, s)
        oc = _moba_attention(proj, b, s)

        x1, x1b, x1p = _merge(oa, ob, oc, proj, xf, w_branch[l].astype(BF16), w_out[l].astype(BF16),
                              ln_mix_g[l], ln_mix_b[l])

        e8, r8, w8, counts = _router(x1, router_w[l], router_bias[l])
        offsets, schedule, n_valid = _expert_tiling(counts, n_tiles)
        expert_ids = jnp.arange(N_EXPERTS, dtype=I32)[:, None, None]
        pos8 = jnp.sum(jnp.where(e8[None] == expert_ids, offsets[:, None, None], 0), axis=0) + r8
        pos8 = pos8.T.reshape(-1)
        xs = _dispatch(x1p, pos8, n_tiles * EXPERT_TM)
        ys = _experts(xs, schedule, n_valid,
                      expert_w_gate.reshape(depth * N_EXPERTS, d, EXPERT_FF),
                      expert_w_up.reshape(depth * N_EXPERTS, d, EXPERT_FF),
                      expert_w_down.reshape(depth * N_EXPERTS, EXPERT_FF, d), l)
        xf, xb = _combine(ys, pos8, w8.T, x1, x1b, shared_w_gate[l].astype(BF16),
                          shared_w_up[l].astype(BF16), shared_w_down[l].astype(BF16),
                          ln_ffn_g[l], ln_ffn_b[l])
    return xf.reshape(b, s, d)
```

```python
import functools
import math

import jax
import jax.numpy as jnp
from jax import lax
from jax.experimental import pallas as pl
from jax.experimental.pallas import tpu as pltpu

F32 = jnp.float32
BF16 = jnp.bfloat16
I32 = jnp.int32
U32 = jnp.uint32

D_MODEL = 2048
DEPTH = 2
ROPE_THETA = 500000.0
NORM_EPS = 1e-5

DIFF_HEADS = 4
DIFF_HALF_DIM = 64
DIFF_ROT = 16
MLA_HEADS = 8
MLA_Q_RANK = 512
MLA_KV_RANK = 256
MLA_NOPE = 128
MLA_ROPE = 64
MLA_V = 128
MOBA_HEADS = 4
MOBA_HEAD_DIM = 128
MOBA_BLOCK = 256
MOBA_TOPK = 3
MOBA_ROT = 32
A_WIDTH = 512
B_WIDTH = 1024
C_WIDTH = 512

N_EXPERTS = 64
N_GROUPS = 8
GROUP_SIZE = N_EXPERTS // N_GROUPS
TOPK_GROUPS = 4
TOPK = 8
EXPERT_FF = 512
ROUTED_SCALE = 2.5

DEEPNORM_ALPHA = (2 * DEPTH) ** 0.25

LANES = 128
NEG_BIG = -1e30

PROJ_TN = 512
MODE_PLAIN, MODE_ROPE_DIFF, MODE_ROPE_MLA, MODE_ROPE_MOBA, MODE_RMS, MODE_CKV, MODE_SIGMOID = range(7)
_ROPE_OF_MODE = {MODE_ROPE_DIFF: (0, DIFF_ROT // 2), MODE_ROPE_MLA: (1, MLA_ROPE // 2),
                 MODE_ROPE_MOBA: (2, MOBA_ROT // 2)}
_ROPE_PERIOD = (DIFF_HALF_DIM, MLA_ROPE, MOBA_HEAD_DIM)
_ROPE_ROT = (DIFF_ROT, MLA_ROPE, MOBA_ROT)

IN_TILE_MODES = ([MODE_ROPE_DIFF, MODE_ROPE_DIFF, MODE_PLAIN, MODE_RMS, MODE_CKV,
                  MODE_ROPE_MOBA, MODE_ROPE_MOBA, MODE_PLAIN] + [MODE_SIGMOID] * 12)
IN_TILE_SCALES = [DIFF_HALF_DIM ** -0.5] + [1.0] * 4 + [MOBA_HEAD_DIM ** -0.5] + [1.0] * 14
IN_COLS_PADDED = PROJ_TN * len(IN_TILE_MODES)

VMEM_LIMIT = 56 * 1024 * 1024

ATT_TQ = 256
PROJ_TM = 2048
MERGE_TM = 256
EXPERT_TM = 512
ROW_TM = 256
ROUTER_TM = 1024


def _cparams(sem):
    return pltpu.CompilerParams(dimension_semantics=sem, vmem_limit_bytes=VMEM_LIMIT)


def _rope_table_kernel(pos_ref, c_ref, cos_ref, sin_ref):
    pos = pos_ref[...].astype(F32)
    for p in range(3):
        ang = pos * c_ref[p:p + 1, :]
        cos_ref[p] = jnp.cos(ang)
        sin_ref[p] = jnp.sin(ang) * c_ref[3 + p:4 + p, :]


def _rope_tables(positions):
    t = positions.size
    lane = jnp.arange(LANES)
    rows = []
    signs = []
    for period, rot in zip(_ROPE_PERIOD, _ROPE_ROT):
        half = rot // 2
        inv_freq = ROPE_THETA ** (-jnp.arange(0, rot, 2, dtype=F32) / rot)
        cp = lane % period
        active = cp < rot
        rows.append(jnp.where(active, inv_freq[cp % half], 0.0))
        signs.append(jnp.where(active, jnp.where(cp < half, -1.0, 1.0), 0.0))
    consts = jnp.stack(rows + signs + [jnp.zeros((LANES,), F32)] * 2).astype(F32)
    tm = min(1024, t)
    cos, sin = pl.pallas_call(
        _rope_table_kernel,
        grid=(t // tm,),
        in_specs=[pl.BlockSpec((tm, 1), lambda i: (i, 0)),
                  pl.BlockSpec((8, LANES), lambda i: (0, 0))],
        out_specs=[pl.BlockSpec((3, tm, LANES), lambda i: (0, i, 0)),
                   pl.BlockSpec((3, tm, LANES), lambda i: (0, i, 0))],
        out_shape=[jax.ShapeDtypeStruct((3, t, LANES), F32)] * 2,
        compiler_params=_cparams(("parallel",)),
        name="rope_tables",
    )(positions.reshape(t, 1), consts)
    return cos, sin


def _proj_kernel(mode_ref, scale_ref, x_ref, w_ref, cos_ref, sin_ref, g_ref, o_ref, *, w_is_nk):
    j = pl.program_id(1)
    mode = mode_ref[j]
    if w_is_nk:
        y = lax.dot_general(x_ref[...], w_ref[...], (((1,), (1,)), ((), ())), preferred_element_type=F32)
    else:
        y = jnp.dot(x_ref[...], w_ref[...], preferred_element_type=F32)
    y = y * scale_ref[j]
    tm, tn = y.shape
    lane = lax.broadcasted_iota(I32, (tm, LANES), 1)

    def rope_chunk(yc, table, half):
        first = (lane % _ROPE_PERIOD[table]) < half
        swapped = jnp.where(first, pltpu.roll(yc, LANES - half, 1), pltpu.roll(yc, half, 1))
        return yc * cos_ref[table] + swapped * sin_ref[table]

    @pl.when(mode == MODE_PLAIN)
    def _():
        o_ref[...] = y.astype(o_ref.dtype)

    for rope_mode, (table, half) in _ROPE_OF_MODE.items():
        @pl.when(mode == rope_mode)
        def _(table=table, half=half):
            for c in range(tn // LANES):
                sl = slice(c * LANES, (c + 1) * LANES)
                o_ref[:, sl] = rope_chunk(y[:, sl], table, half).astype(o_ref.dtype)

    @pl.when(mode == MODE_RMS)
    def _():
        r = lax.rsqrt(jnp.mean(y * y, axis=-1, keepdims=True) + NORM_EPS)
        o_ref[...] = (y * r * g_ref[0:1, :]).astype(o_ref.dtype)

    @pl.when(mode == MODE_CKV)
    def _():
        ckv = y[:, :MLA_KV_RANK]
        r = lax.rsqrt(jnp.mean(ckv * ckv, axis=-1, keepdims=True) + NORM_EPS)
        o_ref[:, :MLA_KV_RANK] = (ckv * r * g_ref[1:2, :MLA_KV_RANK]).astype(o_ref.dtype)
        sl = slice(MLA_KV_RANK, MLA_KV_RANK + LANES)
        table, half = _ROPE_OF_MODE[MODE_ROPE_MLA]
        o_ref[:, sl] = rope_chunk(y[:, sl], table, half).astype(o_ref.dtype)
        o_ref[:, MLA_KV_RANK + LANES:] = jnp.zeros((tm, tn - MLA_KV_RANK - LANES), o_ref.dtype)

    @pl.when(mode == MODE_SIGMOID)
    def _():
        o_ref[...] = (0.5 * jnp.tanh(0.5 * y) + 0.5).astype(o_ref.dtype)


def _proj(x, x_col_block, k_dim, w, modes, scales, cos, sin, gains, tm, w_is_nk=False):
    t = x.shape[0]
    n = w.shape[0] if w_is_nk else w.shape[1]
    assert n % PROJ_TN == 0 and len(modes) == len(scales) == n // PROJ_TN
    assert w.shape == ((n, k_dim) if w_is_nk else (k_dim, n))
    w_spec = (pl.BlockSpec((PROJ_TN, k_dim), lambda i, j, m: (j, 0)) if w_is_nk
              else pl.BlockSpec((k_dim, PROJ_TN), lambda i, j, m: (0, j)))
    grid_spec = pltpu.PrefetchScalarGridSpec(
        num_scalar_prefetch=1,
        grid=(t // tm, n // PROJ_TN),
        in_specs=[pl.BlockSpec(memory_space=pltpu.SMEM),
                  pl.BlockSpec((tm, k_dim), lambda i, j, m: (i, x_col_block)),
                  w_spec,
                  pl.BlockSpec((3, tm, LANES), lambda i, j, m: (0, i, 0)),
                  pl.BlockSpec((3, tm, LANES), lambda i, j, m: (0, i, 0)),
                  pl.BlockSpec((8, PROJ_TN), lambda i, j, m: (0, 0))],
        out_specs=pl.BlockSpec((tm, PROJ_TN), lambda i, j, m: (i, j)),
    )
    return pl.pallas_call(
        functools.partial(_proj_kernel, w_is_nk=w_is_nk),
        grid_spec=grid_spec,
        out_shape=jax.ShapeDtypeStruct((t, n), BF16),
        compiler_params=_cparams(("parallel", "arbitrary")),
        name="proj",
    )(jnp.asarray(modes, I32), jnp.asarray(scales, F32), x, w, cos, sin, gains)


_NT_DIMS = (((1,), (1,)), ((), ()))


def _causal_mask(tq):
    row = lax.broadcasted_iota(I32, (tq, tq), 0)
    col = lax.broadcasted_iota(I32, (tq, tq), 1)
    return col <= row


def _score_strip(s_ref, i, tq, block_scores, block_mask):
    causal = _causal_mask(tq)
    for j in range(i + 1):
        sc = block_scores(j)
        if j == i:
            sc = jnp.where(causal, sc, NEG_BIG)
        elif block_mask is not None:
            sc = jnp.where(block_mask(j), sc, NEG_BIG)
        s_ref[:, j * tq:(j + 1) * tq] = sc


def _softmax_times_v(s_ref, p_ref, v_ref, n):
    tq = s_ref.shape[0]
    nch = n // LANES
    mrun = s_ref[:, 0:LANES]
    for c in range(1, nch):
        mrun = jnp.maximum(mrun, s_ref[:, c * LANES:(c + 1) * LANES])
    m = jnp.broadcast_to(jnp.max(mrun, axis=-1, keepdims=True), (tq, LANES))
    lrun = jnp.zeros((tq, LANES), F32)
    for c in range(nch):
        sl = slice(c * LANES, (c + 1) * LANES)
        p = jnp.exp(s_ref[:, sl] - m)
        lrun = lrun + p
        p_ref[:, sl] = p.astype(BF16)
    l = jnp.sum(lrun, axis=-1, keepdims=True)
    o = jnp.dot(p_ref[:, :n], v_ref[0:n, :], preferred_element_type=F32)
    return o / l


def _diff_attn_kernel(lam_ref, g_ref, q_ref, k_ref, v_ref, o_ref, s1_ref, p1_ref, s2_ref, p2_ref,
                      *, tq, lambda_init):
    nq = q_ref.shape[0] // tq
    lp = lam_ref[...]
    lam = (jnp.exp(jnp.sum(lp[0:1] * lp[1:2], axis=-1, keepdims=True))
           - jnp.exp(jnp.sum(lp[2:3] * lp[3:4], axis=-1, keepdims=True)) + lambda_init)
    lane = lax.broadcasted_iota(I32, (tq, LANES), 1)
    for i in range(nq):
        q = q_ref[i * tq:(i + 1) * tq, :]
        q1 = jnp.where(lane < DIFF_HALF_DIM, q, jnp.zeros_like(q))
        q2 = jnp.where(lane >= DIFF_HALF_DIM, q, jnp.zeros_like(q))
        kblk = lambda j: k_ref[j * tq:(j + 1) * tq, :]
        b = i % 2
        _score_strip(s1_ref.at[b], i, tq,
                     lambda j: lax.dot_general(q1, kblk(j), _NT_DIMS, preferred_element_type=F32), None)
        _score_strip(s2_ref.at[b], i, tq,
                     lambda j: lax.dot_general(q2, kblk(j), _NT_DIMS, preferred_element_type=F32), None)
        n = (i + 1) * tq
        o = (_softmax_times_v(s1_ref.at[b], p1_ref.at[b], v_ref, n)
             - lam * _softmax_times_v(s2_ref.at[b], p2_ref.at[b], v_ref, n))
        r = lax.rsqrt(jnp.mean(o * o, axis=-1, keepdims=True) + NORM_EPS)
        o_ref[i * tq:(i + 1) * tq, :] = (o * r * g_ref[...] * (1.0 - lambda_init)).astype(o_ref.dtype)


def _diff_attention(proj, lam_parts, subln_g, b, s, lambda_init):
    t = b * s
    tq = min(ATT_TQ, s)
    kern = functools.partial(_diff_attn_kernel, tq=tq, lambda_init=lambda_init)
    return pl.pallas_call(
        kern,
        grid=(b, DIFF_HEADS),
        in_specs=[pl.BlockSpec((4, DIFF_HALF_DIM), lambda bi, h: (0, 0)),
                  pl.BlockSpec((1, LANES), lambda bi, h: (0, 0)),
                  pl.BlockSpec((s, LANES), lambda bi, h: (bi, h)),
                  pl.BlockSpec((s, LANES), lambda bi, h: (bi, DIFF_HEADS + h)),
                  pl.BlockSpec((s, LANES), lambda bi, h: (bi, 2 * DIFF_HEADS + h))],
        out_specs=pl.BlockSpec((s, LANES), lambda bi, h: (bi, h)),
        out_shape=jax.ShapeDtypeStruct((t, A_WIDTH), BF16),
        scratch_shapes=[pltpu.VMEM((2, tq, s), F32), pltpu.VMEM((2, tq, s), BF16),
                        pltpu.VMEM((2, tq, s), F32), pltpu.VMEM((2, tq, s), BF16)],
        compiler_params=_cparams(("parallel", "parallel")),
        name="diff_attention",
    )(lam_parts, subln_g.reshape(1, LANES), proj, proj, proj)


def _mla_attn_kernel(qn_ref, qr_ref, kn_ref, kpe_ref, v_ref, o_ref, s_ref, p_ref, *, tq):
    h = pl.program_id(1)
    nq = qn_ref.shape[0] // tq
    lane = lax.broadcasted_iota(I32, (tq, LANES), 1)
    mine = (lane >= MLA_ROPE) == (h % 2 == 1)
    for i in range(nq):
        qn = qn_ref[i * tq:(i + 1) * tq, :]
        qr = qr_ref[i * tq:(i + 1) * tq, :]
        qr = jnp.where(mine, qr, jnp.zeros_like(qr))

        qcat = jnp.concatenate([qn, qr], axis=1)

        def block_scores(j, qcat=qcat):
            rows = slice(j * tq, (j + 1) * tq)
            kcat = jnp.concatenate([kn_ref[rows, :], kpe_ref[rows, :]], axis=1)
            return lax.dot_general(qcat, kcat, _NT_DIMS, preferred_element_type=F32)

        _score_strip(s_ref.at[i % 2], i, tq, block_scores, None)
        o = _softmax_times_v(s_ref.at[i % 2], p_ref.at[i % 2], v_ref, (i + 1) * tq)
        o_ref[i * tq:(i + 1) * tq, :] = o.astype(o_ref.dtype)


def _mla_attention(qbuf, kvbuf, proj, b, s):
    t = b * s
    tq = min(ATT_TQ, s)
    kpe_block = (4 * PROJ_TN + MLA_KV_RANK) // LANES
    return pl.pallas_call(
        functools.partial(_mla_attn_kernel, tq=tq),
        grid=(b, MLA_HEADS),
        in_specs=[pl.BlockSpec((s, LANES), lambda bi, h: (bi, h)),
                  pl.BlockSpec((s, LANES), lambda bi, h: (bi, MLA_HEADS + h // 2)),
                  pl.BlockSpec((s, LANES), lambda bi, h: (bi, h)),
                  pl.BlockSpec((s, LANES), lambda bi, h: (bi, kpe_block)),
                  pl.BlockSpec((s, LANES), lambda bi, h: (bi, MLA_HEADS + h))],
        out_specs=pl.BlockSpec((s, LANES), lambda bi, h: (bi, h)),
        out_shape=jax.ShapeDtypeStruct((t, B_WIDTH), BF16),
        scratch_shapes=[pltpu.VMEM((2, tq, s), F32), pltpu.VMEM((2, tq, s), BF16)],
        compiler_params=_cparams(("parallel", "parallel")),
        name="mla_attention",
    )(qbuf, qbuf, kvbuf, proj, kvbuf)


def _moba_attn_kernel(q_ref, k_ref, v_ref, o_ref, s_ref, p_ref, kmean_ref):
    tq = MOBA_BLOCK
    nb = q_ref.shape[0] // tq
    kmean_ref[...] = jnp.zeros(kmean_ref.shape, F32)
    for j in range(nb):
        kb = k_ref[j * tq:(j + 1) * tq, :].astype(F32)
        kmean_ref[j:j + 1, :] = jnp.sum(kb, axis=0, keepdims=True) * (1.0 / tq)
    km = kmean_ref[...]
    km_hi = km.astype(BF16)
    km_lo = (km - km_hi.astype(F32)).astype(BF16)
    nbp = -(-nb // 8) * 8
    blk = lax.broadcasted_iota(I32, (nbp, tq), 0)
    for i in range(nb):
        q = q_ref[i * tq:(i + 1) * tq, :]
        gate = (lax.dot_general(km_hi, q, _NT_DIMS, preferred_element_type=F32)
                + lax.dot_general(km_lo, q, _NT_DIMS, preferred_element_type=F32))[:nbp, :]
        fully_past = blk < i
        g = jnp.where(fully_past, gate, -jnp.inf)
        cnt = jnp.zeros(g.shape, F32)
        for jp in range(i):
            row = g[jp:jp + 1, :]
            ahead = (row > g) | ((row == g) & (jp < blk))
            cnt = cnt + jnp.where(ahead, 1.0, 0.0)
        kept_t = jnp.where((cnt < MOBA_TOPK) & fully_past, 1.0, 0.0)
        kept_t = jnp.concatenate([kept_t, jnp.zeros((LANES - nbp, tq), F32)], axis=0)
        kept = kept_t.T > 0.5
        _score_strip(s_ref.at[i % 2], i, tq,
                     lambda j, q=q: lax.dot_general(q, k_ref[j * tq:(j + 1) * tq, :], _NT_DIMS,
                                                    preferred_element_type=F32),
                     lambda j, kept=kept: kept[:, j:j + 1])
        o = _softmax_times_v(s_ref.at[i % 2], p_ref.at[i % 2], v_ref, (i + 1) * tq)
        o_ref[i * tq:(i + 1) * tq, :] = o.astype(o_ref.dtype)


def _moba_attention(proj, b, s):
    assert s % MOBA_BLOCK == 0 and s // MOBA_BLOCK <= LANES
    t = b * s
    base = 5 * PROJ_TN // LANES
    return pl.pallas_call(
        _moba_attn_kernel,
        grid=(b, MOBA_HEADS),
        in_specs=[pl.BlockSpec((s, LANES), lambda bi, h: (bi, base + h)),
                  pl.BlockSpec((s, LANES), lambda bi, h: (bi, base + MOBA_HEADS + h)),
                  pl.BlockSpec((s, LANES), lambda bi, h: (bi, base + 2 * MOBA_HEADS + h))],
        out_specs=pl.BlockSpec((s, LANES), lambda bi, h: (bi, h)),
        out_shape=jax.ShapeDtypeStruct((t, C_WIDTH), BF16),
        scratch_shapes=[pltpu.VMEM((2, MOBA_BLOCK, s), F32), pltpu.VMEM((2, MOBA_BLOCK, s), BF16),
                        pltpu.VMEM((LANES, LANES), F32)],
        compiler_params=_cparams(("parallel", "parallel")),
        name="moba_attention",
    )(proj, proj, proj)


def _layer_norm_rows(z, g, b):
    mu = jnp.mean(z, axis=-1, keepdims=True)
    zc = z - mu
    var = jnp.mean(zc * zc, axis=-1, keepdims=True)
    return zc * lax.rsqrt(var + NORM_EPS) * g + b


def _pack_halves(z):
    n = z.shape[1] // 2
    lo = lax.bitcast_convert_type(z[:, :n].astype(BF16).astype(F32), U32)
    hi = lax.bitcast_convert_type(z[:, n:].astype(BF16).astype(F32), U32)
    return (lo >> 16) | hi


def _unpack_halves(u):
    lo = lax.bitcast_convert_type(u << 16, F32)
    hi = lax.bitcast_convert_type(u & jnp.uint32(0xFFFF0000), F32)
    return lo, hi


ROW_WORDS = D_MODEL // 2
ROW_SUBLANES = ROW_WORDS // LANES


def _store_row_tiles(ref, packed):
    tm = packed.shape[0]
    for a in range(ROW_SUBLANES):
        ref[pl.ds(a, tm, stride=ROW_SUBLANES), :] = packed[:, a * LANES:(a + 1) * LANES]


def _load_row_tile_chunk(ref, a, tm):
    return ref[pl.ds(a, tm, stride=ROW_SUBLANES), :]


def _merge_kernel(oa_ref, ob_ref, oc_ref, g0_ref, g1_ref, g2_ref, x_ref, wb_ref, wo_ref,
                  lng_ref, lnb_ref, xo_ref, xb_ref, xp_ref):
    ya = jnp.dot(oa_ref[...], wb_ref[:A_WIDTH, :], preferred_element_type=F32)
    yb = jnp.dot(ob_ref[...], wb_ref[A_WIDTH:A_WIDTH + B_WIDTH, :], preferred_element_type=F32)
    yc = jnp.dot(oc_ref[...], wb_ref[A_WIDTH + B_WIDTH:, :], preferred_element_type=F32)
    y = g0_ref[...].astype(F32) * ya + g1_ref[...].astype(F32) * yb + g2_ref[...].astype(F32) * yc
    mix = jnp.dot(y.astype(BF16), wo_ref[...], preferred_element_type=F32)
    z = _layer_norm_rows(DEEPNORM_ALPHA * x_ref[...] + mix, lng_ref[...], lnb_ref[...])
    xo_ref[...] = z
    xb_ref[...] = z.astype(BF16)
    _store_row_tiles(xp_ref, _pack_halves(z))


def _merge(oa, ob, oc, proj, x, w_branch, w_out, ln_g, ln_b):
    t = x.shape[0]
    tm = min(MERGE_TM, t)
    gate_base = 8 * PROJ_TN // D_MODEL
    row = lambda i: (i, 0)
    whole = lambda i: (0, 0)
    resident = dict(pipeline_mode=pl.Buffered(1))
    return pl.pallas_call(
        _merge_kernel,
        grid=(t // tm,),
        in_specs=[pl.BlockSpec((tm, A_WIDTH), row),
                  pl.BlockSpec((tm, B_WIDTH), row),
                  pl.BlockSpec((tm, C_WIDTH), row),
                  pl.BlockSpec((tm, D_MODEL), lambda i: (i, gate_base)),
                  pl.BlockSpec((tm, D_MODEL), lambda i: (i, gate_base + 1)),
                  pl.BlockSpec((tm, D_MODEL), lambda i: (i, gate_base + 2)),
                  pl.BlockSpec((tm, D_MODEL), row),
                  pl.BlockSpec((D_MODEL, D_MODEL), whole, **resident),
                  pl.BlockSpec((D_MODEL, D_MODEL), whole, **resident),
                  pl.BlockSpec((1, D_MODEL), whole),
                  pl.BlockSpec((1, D_MODEL), whole)],
        out_specs=[pl.BlockSpec((tm, D_MODEL), row),
                   pl.BlockSpec((tm, D_MODEL), row),
                   pl.BlockSpec((tm * ROW_SUBLANES, LANES), row)],
        out_shape=[jax.ShapeDtypeStruct((t, D_MODEL), F32),
                   jax.ShapeDtypeStruct((t, D_MODEL), BF16),
                   jax.ShapeDtypeStruct((t * ROW_SUBLANES, LANES), U32)],
        compiler_params=_cparams(("parallel",)),
        name="merge_outproj_ln",
    )(oa, ob, oc, proj, proj, proj, x, w_branch, w_out, ln_g.reshape(1, -1), ln_b.reshape(1, -1))


def _split_bf16(a):
    hi = a.astype(BF16)
    lo = (a - hi.astype(F32)).astype(BF16)
    return hi, lo


def _router_kernel(x_ref, wt_ref, bias_ref, upper_ref, lower_ref,
                   e8_ref, r8_ref, w8_ref, cnt_ref, carry_ref):
    i = pl.program_id(0)

    @pl.when(i == 0)
    def _():
        carry_ref[...] = jnp.zeros(carry_ref.shape, F32)

    xh, xl = _split_bf16(x_ref[...])
    wh, wl = _split_bf16(wt_ref[...])
    logits = (lax.dot_general(wh, xh, _NT_DIMS, preferred_element_type=F32)
              + lax.dot_general(wh, xl, _NT_DIMS, preferred_element_type=F32)
              + lax.dot_general(wl, xh, _NT_DIMS, preferred_element_type=F32))
    scores = 1.0 / (1.0 + jnp.exp(-logits))
    choice = scores + bias_ref[:, 0:1]
    tm = choice.shape[1]
    sub = lax.broadcasted_iota(I32, (GROUP_SIZE, tm), 0)

    group_rows = []
    for g in range(N_GROUPS):
        cg = choice[g * GROUP_SIZE:(g + 1) * GROUP_SIZE, :]
        m1 = jnp.max(cg, axis=0, keepdims=True)
        first = jnp.min(jnp.where(cg == m1, sub, GROUP_SIZE), axis=0, keepdims=True)
        m2 = jnp.max(jnp.where(sub == first, -jnp.inf, cg), axis=0, keepdims=True)
        group_rows.append(m1 + m2)
    gs = jnp.concatenate(group_rows, axis=0)
    ahead = jnp.zeros(gs.shape, F32)
    for gp in range(N_GROUPS):
        rowv = gs[gp:gp + 1, :]
        ahead = ahead + jnp.where((rowv > gs) | ((rowv == gs) & (gp < sub)), 1.0, 0.0)
    keep_group = ahead < TOPK_GROUPS
    masked = jnp.concatenate(
        [jnp.where(keep_group[g:g + 1, :], choice[g * GROUP_SIZE:(g + 1) * GROUP_SIZE, :], -jnp.inf)
         for g in range(N_GROUPS)], axis=0)

    eidx = lax.broadcasted_iota(I32, masked.shape, 0)
    remaining = masked
    self32 = jnp.zeros(masked.shape, F32)
    for _ in range(TOPK):
        top = jnp.max(remaining, axis=0, keepdims=True)
        first = jnp.min(jnp.where(remaining == top, eidx, N_EXPERTS), axis=0, keepdims=True)
        hit = eidx == first
        self32 = jnp.where(hit, 1.0, self32)
        remaining = jnp.where(hit, -jnp.inf, remaining)
    sel = self32 > 0.5
    picked = jnp.where(sel, scores, 0.0)
    gates = picked / jnp.sum(picked, axis=0, keepdims=True) * ROUTED_SCALE
    selb = self32.astype(BF16)

    rank = jnp.dot(selb, upper_ref[...], preferred_element_type=F32) + carry_ref[:, 0:1]
    slot = jnp.dot(lower_ref[...], selb, preferred_element_type=F32)
    carry_ref[...] = carry_ref[...] + jnp.sum(self32, axis=1, keepdims=True)
    cnt_ref[...] = carry_ref[...].astype(I32)

    eidf = eidx.astype(F32)
    e_rows, r_rows, w_rows = [], [], []
    for k in range(TOPK):
        hit = sel & (slot == float(k))
        e_rows.append(jnp.sum(jnp.where(hit, eidf, 0.0), axis=0, keepdims=True))
        r_rows.append(jnp.sum(jnp.where(hit, rank, 0.0), axis=0, keepdims=True))
        w_rows.append(jnp.sum(jnp.where(hit, gates, 0.0), axis=0, keepdims=True))
    e8_ref[...] = jnp.concatenate(e_rows, axis=0).astype(I32)
    r8_ref[...] = jnp.concatenate(r_rows, axis=0).astype(I32)
    w8_ref[...] = jnp.concatenate(w_rows, axis=0)


def _router(x, router_w, router_bias):
    t = x.shape[0]
    tm = min(ROUTER_TM, t)
    upper = (jnp.arange(tm)[:, None] < jnp.arange(tm)[None, :]).astype(BF16)
    lower = (jnp.arange(N_EXPERTS)[None, :] < jnp.arange(N_EXPERTS)[:, None]).astype(BF16)
    bias = jnp.broadcast_to(router_bias.astype(F32)[:, None], (N_EXPERTS, LANES))
    whole = lambda i: (0, 0)
    col = lambda i: (0, i)
    return pl.pallas_call(
        _router_kernel,
        grid=(t // tm,),
        in_specs=[pl.BlockSpec((tm, D_MODEL), lambda i: (i, 0)),
                  pl.BlockSpec((N_EXPERTS, D_MODEL), whole),
                  pl.BlockSpec((N_EXPERTS, LANES), whole),
                  pl.BlockSpec((tm, tm), whole),
                  pl.BlockSpec((N_EXPERTS, N_EXPERTS), whole)],
        out_specs=[pl.BlockSpec((TOPK, tm), col),
                   pl.BlockSpec((TOPK, tm), col),
                   pl.BlockSpec((TOPK, tm), col),
                   pl.BlockSpec((N_EXPERTS, LANES), whole)],
        out_shape=[jax.ShapeDtypeStruct((TOPK, t), I32),
                   jax.ShapeDtypeStruct((TOPK, t), I32),
                   jax.ShapeDtypeStruct((TOPK, t), F32),
                   jax.ShapeDtypeStruct((N_EXPERTS, LANES), I32)],
        scratch_shapes=[pltpu.VMEM((N_EXPERTS, LANES), F32)],
        compiler_params=_cparams(("arbitrary",)),
        name="router",
    )(x, router_w.T, bias, upper, lower)


def _row_tile(ref, r):
    return ref.at[pl.ds(pl.multiple_of(r * ROW_SUBLANES, ROW_SUBLANES), ROW_SUBLANES)]


def _dispatch_kernel(pos_ref, x_ref, xs_ref, sem):
    tm = x_ref.shape[0] // ROW_SUBLANES

    def row_copy(t, k):
        return pltpu.make_async_copy(_row_tile(x_ref, t), _row_tile(xs_ref, pos_ref[t * TOPK + k]), sem)

    def start(t, c):
        for k in range(TOPK):
            row_copy(t, k).start(priority=k % 2)
        return c
    lax.fori_loop(0, tm, start, 0)

    for k in range(TOPK):
        pltpu.make_async_copy(x_ref, xs_ref.at[pl.ds(0, tm * ROW_SUBLANES)], sem).wait()


def _dispatch(xp, pos8, n_rows):
    t = xp.shape[0] // ROW_SUBLANES
    tm = min(ROW_TM, t)
    return pl.pallas_call(
        _dispatch_kernel,
        grid=(t // tm,),
        in_specs=[pl.BlockSpec((tm * TOPK,), lambda i: (i,), memory_space=pltpu.SMEM),
                  pl.BlockSpec((tm * ROW_SUBLANES, LANES), lambda i: (i, 0))],
        out_specs=pl.BlockSpec(memory_space=pl.ANY),
        scratch_shapes=[pltpu.SemaphoreType.DMA(())],
        out_shape=jax.ShapeDtypeStruct((n_rows * ROW_SUBLANES, LANES), U32),
        compiler_params=_cparams(("arbitrary",)),
        name="dispatch",
    )(pos8, xp)


def _expert_kernel(te_ref, rows_ref, fresh_ref, par_ref, nxt_ref, nv_ref, xs_ref, wg_hbm, wu_hbm, wd_hbm,
                   ys_ref, wgf_ref, wuf_ref, wdf_ref, wgb_ref, wub_ref, wdb_ref, sem, *, first_expert):
    i = pl.program_id(0)

    def weight_copies(e, s):
        return [pltpu.make_async_copy(hbm.at[first_expert + e], buf.at[s], sem.at[s])
                for hbm, buf in ((wg_hbm, wgf_ref), (wu_hbm, wuf_ref), (wd_hbm, wdf_ref))]

    @pl.when(i < nv_ref[0])
    def _():
        @pl.when(fresh_ref[i] == 1)
        def _():
            s = par_ref[i]

            @pl.when(i == 0)
            def _():
                for c in weight_copies(te_ref[0], 0):
                    c.start()

            for c in weight_copies(te_ref[i], s):
                c.wait()

            @pl.when(nxt_ref[i] >= 0)
            def _():
                for c in weight_copies(nxt_ref[i], 1 - s):
                    c.start()

            wgb_ref[...] = wgf_ref[s].astype(BF16)
            wub_ref[...] = wuf_ref[s].astype(BF16)
            wdb_ref[...] = wdf_ref[s].astype(BF16)

        half = D_MODEL // 2

        def ffn(tm):
            u = jnp.concatenate([_load_row_tile_chunk(xs_ref, a, tm) for a in range(ROW_SUBLANES)], axis=1)
            live = lax.broadcasted_iota(I32, (tm, 1), 0) < rows_ref[i]
            lo, hi = _unpack_halves(jnp.where(live, u, jnp.zeros_like(u)))
            xl = lo.astype(BF16)
            xh = hi.astype(BF16)
            gate = (jnp.dot(xl, wgb_ref[:half, :], preferred_element_type=F32)
                    + jnp.dot(xh, wgb_ref[half:, :], preferred_element_type=F32))
            up = (jnp.dot(xl, wub_ref[:half, :], preferred_element_type=F32)
                  + jnp.dot(xh, wub_ref[half:, :], preferred_element_type=F32))
            hid = (gate / (1.0 + jnp.exp(-gate))) * up
            y = jnp.dot(hid.astype(BF16), wdb_ref[...], preferred_element_type=F32)
            _store_row_tiles(ys_ref, _pack_halves(y))

        @pl.when(rows_ref[i] > EXPERT_TM // 2)
        def _():
            ffn(EXPERT_TM)

        @pl.when(rows_ref[i] <= EXPERT_TM // 2)
        def _():
            ffn(EXPERT_TM // 2)


def _experts(xs, schedule, n_valid, w_gate, w_up, w_down, layer):
    n_rows = xs.shape[0] // ROW_SUBLANES
    n_tiles = n_rows // EXPERT_TM
    row = lambda i, te, tr, fr, pa, nx, nv: (jnp.minimum(i, nv[0] - 1), 0)
    grid_spec = pltpu.PrefetchScalarGridSpec(
        num_scalar_prefetch=6,
        grid=(n_tiles,),
        in_specs=[pl.BlockSpec((EXPERT_TM * ROW_SUBLANES, LANES), row),
                  pl.BlockSpec(memory_space=pl.ANY),
                  pl.BlockSpec(memory_space=pl.ANY),
                  pl.BlockSpec(memory_space=pl.ANY)],
        out_specs=pl.BlockSpec((EXPERT_TM * ROW_SUBLANES, LANES), row),
        scratch_shapes=[pltpu.VMEM((2, D_MODEL, EXPERT_FF), F32), pltpu.VMEM((2, D_MODEL, EXPERT_FF), F32),
                        pltpu.VMEM((2, EXPERT_FF, D_MODEL), F32),
                        pltpu.VMEM((D_MODEL, EXPERT_FF), BF16), pltpu.VMEM((D_MODEL, EXPERT_FF), BF16),
                        pltpu.VMEM((EXPERT_FF, D_MODEL), BF16),
                        pltpu.SemaphoreType.DMA((2,))],
    )
    return pl.pallas_call(
        functools.partial(_expert_kernel, first_expert=layer * N_EXPERTS),
        grid_spec=grid_spec,
        out_shape=jax.ShapeDtypeStruct((n_rows * ROW_SUBLANES, LANES), U32),
        compiler_params=_cparams(("arbitrary",)),
        name="experts",
    )(*schedule, n_valid, xs, w_gate, w_up, w_down)


def _combine_kernel(pos_ref, pos_next_ref, w8_ref, ys_ref, x_ref, xb_ref, wsg_ref, wsu_ref, wsd_ref,
                    lng_ref, lnb_ref, xo_ref, xob_ref, buf_ref, sem):
    i = pl.program_id(0)
    tm = x_ref.shape[0]
    slot = i % 2

    def row_copy(p_ref, s, t, k):
        return pltpu.make_async_copy(_row_tile(ys_ref, p_ref[t * TOPK + k]), _row_tile(buf_ref.at[s, k], t),
                                     sem.at[s])

    def start_rows(p_ref, s, t0, t1):
        def start(t, c):
            for k in range(TOPK):
                row_copy(p_ref, s, t, k).start(priority=k % 2)
            return c
        lax.fori_loop(t0, t1, start, 0)

    @pl.when(i == 0)
    def _():
        start_rows(pos_ref, 0, 0, tm)

    has_next = i + 1 < pl.num_programs(0)

    def start_next_half(h):
        @pl.when(has_next)
        def _():
            start_rows(pos_next_ref, 1 - slot, h * (tm // 2), (h + 1) * (tm // 2))

    start_next_half(0)
    xb = xb_ref[...]
    sg = jnp.dot(xb, wsg_ref[...], preferred_element_type=F32)
    su = jnp.dot(xb, wsu_ref[...], preferred_element_type=F32)
    hid = (sg / (1.0 + jnp.exp(-sg))) * su
    shared = jnp.dot(hid.astype(BF16), wsd_ref[...], preferred_element_type=F32)

    for k in range(TOPK):
        pltpu.make_async_copy(ys_ref.at[pl.ds(0, tm * ROW_SUBLANES)], buf_ref.at[slot, k], sem.at[slot]).wait()

    start_next_half(1)
    w8 = w8_ref[...]
    wk = [jnp.broadcast_to(w8[:, k:k + 1], (tm, LANES)) for k in range(TOPK)]
    lo_parts, hi_parts = [], []
    for a in range(ROW_SUBLANES):
        acc_lo = jnp.zeros((tm, LANES), F32)
        acc_hi = jnp.zeros((tm, LANES), F32)
        for k in range(TOPK):
            lo, hi = _unpack_halves(_load_row_tile_chunk(buf_ref.at[slot, k], a, tm))
            acc_lo = acc_lo + wk[k] * lo
            acc_hi = acc_hi + wk[k] * hi
        lo_parts.append(acc_lo)
        hi_parts.append(acc_hi)
    routed = jnp.concatenate(lo_parts + hi_parts, axis=1)
    z = _layer_norm_rows(DEEPNORM_ALPHA * x_ref[...] + (routed + shared), lng_ref[...], lnb_ref[...])
    xo_ref[...] = z
    xob_ref[...] = z.astype(BF16)


def _combine(ys, pos8, w8t, x, xb, ws_gate, ws_up, ws_down, ln_g, ln_b):
    t = x.shape[0]
    tm = min(ROW_TM, t)
    row = lambda i: (i, 0)
    whole = lambda i: (0, 0)
    n_steps = t // tm
    grid_spec = pl.GridSpec(
        grid=(n_steps,),
        in_specs=[pl.BlockSpec((tm * TOPK,), lambda i: (i,), memory_space=pltpu.SMEM),
                  pl.BlockSpec((tm * TOPK,), lambda i: (jnp.minimum(i + 1, n_steps - 1),),
                               memory_space=pltpu.SMEM),
                  pl.BlockSpec((tm, TOPK), row),
                  pl.BlockSpec(memory_space=pl.ANY),
                  pl.BlockSpec((tm, D_MODEL), row),
                  pl.BlockSpec((tm, D_MODEL), row),
                  pl.BlockSpec((D_MODEL, EXPERT_FF), whole),
                  pl.BlockSpec((D_MODEL, EXPERT_FF), whole),
                  pl.BlockSpec((EXPERT_FF, D_MODEL), whole),
                  pl.BlockSpec((1, D_MODEL), whole),
                  pl.BlockSpec((1, D_MODEL), whole)],
        out_specs=[pl.BlockSpec((tm, D_MODEL), row),
                   pl.BlockSpec((tm, D_MODEL), row)],
        scratch_shapes=[pltpu.VMEM((2, TOPK, tm * ROW_SUBLANES, LANES), U32),
                        pltpu.SemaphoreType.DMA((2,))],
    )
    return pl.pallas_call(
        _combine_kernel,
        grid_spec=grid_spec,
        out_shape=[jax.ShapeDtypeStruct((t, D_MODEL), F32),
                   jax.ShapeDtypeStruct((t, D_MODEL), BF16)],
        compiler_params=_cparams(("arbitrary",)),
        name="combine_shared_ln",
    )(pos8, pos8, w8t, ys, x, xb, ws_gate, ws_up, ws_down, ln_g.reshape(1, -1), ln_b.reshape(1, -1))


_CKV_TILE = 4
_TAIL_SHIFT = PROJ_TN - MLA_KV_RANK - MLA_ROPE


def _in_proj_weight_kernel(prev_ref, cur_ref, o_ref):
    j = pl.program_id(1)

    @pl.when(j < _CKV_TILE)
    def _():
        o_ref[...] = cur_ref[0].astype(BF16)

    @pl.when(j == _CKV_TILE)
    def _():
        w = cur_ref[0]
        kpe = w[MLA_KV_RANK:MLA_KV_RANK + MLA_ROPE, :]
        pad = jnp.zeros((_TAIL_SHIFT - MLA_ROPE, w.shape[1]), F32)
        o_ref[...] = jnp.concatenate([w[:MLA_KV_RANK, :], kpe, kpe, pad], axis=0).astype(BF16)

    @pl.when(j > _CKV_TILE)
    def _():
        keep = PROJ_TN - _TAIL_SHIFT
        o_ref[...] = jnp.concatenate([prev_ref[0][keep:, :], cur_ref[0][:keep, :]], axis=0).astype(BF16)


def _in_proj_weight(w_in_t_all, layer):
    d = w_in_t_all.shape[2]
    return pl.pallas_call(
        _in_proj_weight_kernel,
        grid=(1, IN_COLS_PADDED // PROJ_TN),
        in_specs=[pl.BlockSpec((1, PROJ_TN, d), lambda i, j: (layer, jnp.maximum(j - 1, 0), 0)),
                  pl.BlockSpec((1, PROJ_TN, d), lambda i, j: (layer, j, 0))],
        out_specs=pl.BlockSpec((PROJ_TN, d), lambda i, j: (j, 0)),
        out_shape=jax.ShapeDtypeStruct((IN_COLS_PADDED, d), BF16),
        compiler_params=_cparams(("parallel", "parallel")),
        name="in_proj_weight",
    )(w_in_t_all, w_in_t_all)


def _uq_weight(w_uq):
    w = w_uq.reshape(MLA_Q_RANK, MLA_HEADS, MLA_NOPE + MLA_ROPE)
    return jnp.concatenate([w[:, :, :MLA_NOPE].reshape(MLA_Q_RANK, -1),
                            w[:, :, MLA_NOPE:].reshape(MLA_Q_RANK, -1)], axis=1).astype(BF16)


def _ukv_weight(w_ukv):
    w = w_ukv.reshape(MLA_KV_RANK, MLA_HEADS, MLA_NOPE + MLA_V)
    return jnp.concatenate([w[:, :, :MLA_NOPE].reshape(MLA_KV_RANK, -1),
                            w[:, :, MLA_NOPE:].reshape(MLA_KV_RANK, -1)], axis=1).astype(BF16)


def _expert_tiling_kernel(cnt_ref, meta_ref, off_ref):
    ntp = meta_ref.shape[1]
    cnt = cnt_ref[...].astype(F32)
    tiles = jnp.floor((cnt + (EXPERT_TM - 1)) * (1.0 / EXPERT_TM))
    row = lax.broadcasted_iota(I32, (N_EXPERTS, N_EXPERTS), 0)
    col = lax.broadcasted_iota(I32, (N_EXPERTS, N_EXPERTS), 1)
    incl = jnp.where(col <= row, 1.0, 0.0).astype(BF16)
    ends = jnp.dot(incl, tiles.astype(BF16), preferred_element_type=F32)
    first = ends - tiles
    off_ref[...] = (first * EXPERT_TM).astype(I32)

    def widen(a):
        return jnp.concatenate([a] * (ntp // LANES), axis=1)
    tile_id = lax.broadcasted_iota(I32, (N_EXPERTS, ntp), 1).astype(F32)
    ends_w, first_w, cnt_w = widen(ends), widen(first), widen(cnt)
    tile_expert = jnp.minimum(jnp.sum(jnp.where(ends_w <= tile_id, 1.0, 0.0), axis=0, keepdims=True),
                              N_EXPERTS - 1.0)
    mine = (first_w <= tile_id) & (tile_id < ends_w)
    rows = jnp.clip(cnt_w - (tile_id - first_w) * EXPERT_TM, 0.0, float(EXPERT_TM))
    tile_rows = jnp.sum(jnp.where(mine, rows, 0.0), axis=0, keepdims=True)
    n_valid = widen(ends[N_EXPERTS - 1:N_EXPERTS, :])
    tiles_w = widen(tiles)
    expert_id = lax.broadcasted_iota(I32, (N_EXPERTS, ntp), 0).astype(F32)
    fresh = jnp.sum(jnp.where(mine & (tile_id == first_w), 1.0, 0.0), axis=0, keepdims=True)
    before = jnp.sum(jnp.where((ends_w <= tile_id) & (tiles_w > 0.0), 1.0, 0.0), axis=0, keepdims=True)
    parity = before - 2.0 * jnp.floor(before * 0.5)
    seg_end = jnp.sum(jnp.where(mine, ends_w, 0.0), axis=0, keepdims=True)
    follows = (first_w == seg_end) & (tiles_w > 0.0)
    has_next = jnp.sum(jnp.where(follows, 1.0, 0.0), axis=0, keepdims=True)
    nxt = jnp.where(has_next > 0.0,
                    jnp.sum(jnp.where(follows, expert_id, 0.0), axis=0, keepdims=True), -1.0)
    meta_ref[...] = jnp.concatenate(
        [tile_expert, tile_rows, n_valid, fresh, parity, nxt, jnp.zeros((2, ntp), F32)], axis=0).astype(I32)


def _expert_tiling(counts, n_tiles):
    ntp = -(-n_tiles // LANES) * LANES
    meta, off = pl.pallas_call(
        _expert_tiling_kernel,
        out_shape=[jax.ShapeDtypeStruct((8, ntp), I32), jax.ShapeDtypeStruct((N_EXPERTS, LANES), I32)],
        name="expert_tiling",
    )(counts)
    schedule = tuple(meta[r, :n_tiles] for r in (0, 1, 3, 4, 5))
    return off[:, 0], schedule, meta[2, :1]


def kernel(x, positions, w_in, diff_lambda_q1, diff_lambda_k1, diff_lambda_q2, diff_lambda_k2, diff_subln_g, mla_q_norm_g, mla_w_uq, mla_kv_norm_g, mla_w_ukv, w_branch, w_out, ln_mix_g, ln_mix_b, router_w, router_bias, expert_w_gate, expert_w_up, expert_w_down, shared_w_gate, shared_w_up, shared_w_down, ln_ffn_g, ln_ffn_b):
    b, s, d = x.shape
    t = b * s
    depth = w_in.shape[0]
    cos, sin = _rope_tables(positions)
    xf = x.reshape(t, d)
    xb = xf.astype(BF16)
    w_in_t = jnp.swapaxes(w_in, 1, 2)
    n_tiles = t * TOPK // EXPERT_TM + N_EXPERTS
    proj_tm = min(PROJ_TM, t)

    for l in range(depth):
        gains = jnp.zeros((8, PROJ_TN), F32)
        gains = gains.at[0, :].set(mla_q_norm_g[l]).at[1, :MLA_KV_RANK].set(mla_kv_norm_g[l])
        proj = _proj(xb, 0, d, _in_proj_weight(w_in_t, l), IN_TILE_MODES, IN_TILE_SCALES,
                     cos, sin, gains, proj_tm, w_is_nk=True)
        qbuf = _proj(proj, 3, MLA_Q_RANK, _uq_weight(mla_w_uq[l]),
                     [MODE_PLAIN, MODE_PLAIN, MODE_ROPE_MLA], [(MLA_NOPE + MLA_ROPE) ** -0.5] * 3,
                     cos, sin, gains, proj_tm)
        kvbuf = _proj(proj, 4 * PROJ_TN // MLA_KV_RANK, MLA_KV_RANK, _ukv_weight(mla_w_ukv[l]),
                      [MODE_PLAIN] * 4, [1.0] * 4, cos, sin, gains, proj_tm)

        lambda_init = 0.8 - 0.6 * math.exp(-0.3 * l)
        lam_parts = jnp.stack([diff_lambda_q1[l], diff_lambda_k1[l],
                               diff_lambda_q2[l], diff_lambda_k2[l]]).astype(F32)
        oa = _diff_attention(proj, lam_parts, diff_subln_g[l], b, s, lambda_init)
        ob = _mla_attention(qbuf, kvbuf, proj, b, s)
        oc = _moba_attention(proj, b, s)

        x1, x1b, x1p = _merge(oa, ob, oc, proj, xf, w_branch[l].astype(BF16), w_out[l].astype(BF16),
                              ln_mix_g[l], ln_mix_b[l])

        e8, r8, w8, counts = _router(x1, router_w[l], router_bias[l])
        offsets, schedule, n_valid = _expert_tiling(counts, n_tiles)
        expert_ids = jnp.arange(N_EXPERTS, dtype=I32)[:, None, None]
        pos8 = jnp.sum(jnp.where(e8[None] == expert_ids, offsets[:, None, None], 0), axis=0) + r8
        pos8 = pos8.T.reshape(-1)
        xs = _dispatch(x1p, pos8, n_tiles * EXPERT_TM)
        ys = _experts(xs, schedule, n_valid,
                      expert_w_gate.reshape(depth * N_EXPERTS, d, EXPERT_FF),
                      expert_w_up.reshape(depth * N_EXPERTS, d, EXPERT_FF),
                      expert_w_down.reshape(depth * N_EXPERTS, EXPERT_FF, d), l)
        xf, xb = _combine(ys, pos8, w8.T, x1, x1b, shared_w_gate[l].astype(BF16),
                          shared_w_up[l].astype(BF16), shared_w_down[l].astype(BF16),
                          ln_ffn_g[l], ln_ffn_b[l])
    return xf.reshape(b, s, d)
```
